```python
import math
import jax, jax.numpy as jnp
from jax import lax
import numpy as np

D_MODEL = 2048
BATCH = 4
SEQ = 4096
DEPTH = 2

PLE_DIM = 256
EPS = 1e-6
MIX_WIDTH = D_MODEL
ATTN_WIDTH = D_MODEL // 2
POOL_WIDTH = D_MODEL // 4
CONV_WIDTH = D_MODEL // 4
ATTN_HEAD_DIM = 64
ATTN_V_DIM = 2 * ATTN_HEAD_DIM
N_ATTN_HEADS = ATTN_WIDTH // ATTN_V_DIM
QK_COLS = N_ATTN_HEADS * 2 * ATTN_HEAD_DIM
ROPE_DIM = ATTN_HEAD_DIM // 4
ROPE_THETA = 500000.0
Q_BLOCK = 128
SUBLN_EPS = 1e-5
POOL_WINDOWS = (2, 4, 8, 16)
N_POOL_GROUPS = len(POOL_WINDOWS)
POOL_GROUP_DIM = POOL_WIDTH // N_POOL_GROUPS
CONV_KERNEL = 31
LN_EPS = 1e-5
IN_COLS = 2 * QK_COLS + ATTN_WIDTH + POOL_WIDTH + 2 * CONV_WIDTH
SPLIT_POINTS = (QK_COLS, 2 * QK_COLS, 2 * QK_COLS + ATTN_WIDTH, 2 * QK_COLS + ATTN_WIDTH + POOL_WIDTH)
D_FF = 7 * D_MODEL // 2
N_EXPERTS = 8
TOP_K = 2
N_DENSE = (DEPTH + 1) // 2
N_MOE = DEPTH // 2

kernel_name = "hybrid_diffattn_pool_conformer_moe_ple"


def rmsnorm(x, g, eps=EPS):
    x32 = x.astype(jnp.float32)
    y = x32 * lax.rsqrt(jnp.mean(x32 * x32, axis=-1, keepdims=True) + eps)
    return (y * g.astype(jnp.float32)).astype(x.dtype)


def layernorm(x, g, b, eps=LN_EPS):
    x32 = x.astype(jnp.float32)
    mu = jnp.mean(x32, axis=-1, keepdims=True)
    xc = x32 - mu
    y = xc * lax.rsqrt(jnp.mean(xc * xc, axis=-1, keepdims=True) + eps)
    return (y * g.astype(jnp.float32) + b.astype(jnp.float32)).astype(x.dtype)


def rotary_partial(x, cos, sin):
    half = ROPE_DIM // 2
    x1 = x[..., :half]
    x2 = x[..., half:ROPE_DIM]
    return jnp.concatenate([x1 * cos - x2 * sin, x2 * cos + x1 * sin, x[..., ROPE_DIM:]], axis=-1)


def diff_attention(q, k, v, lam):
    B, S, H, _, Dh = q.shape
    nb = S // Q_BLOCK
    scale = Dh ** -0.5
    qb = jnp.moveaxis(q.reshape(B, nb, Q_BLOCK, H, 2, Dh), 1, 0)
    starts = jnp.arange(nb, dtype=jnp.int32) * Q_BLOCK
    k_idx = jnp.arange(S, dtype=jnp.int32)

    def block(args):
        qblk, start = args
        s = jnp.einsum('bqhcd,bkhcd->bhcqk', qblk, k).astype(jnp.float32) * scale
        q_idx = start + jnp.arange(Q_BLOCK, dtype=jnp.int32)
        mask = k_idx[None, :] <= q_idx[:, None]
        a = jax.nn.softmax(jnp.where(mask, s, -jnp.inf), axis=-1)
        w = a[:, :, 0] - lam * a[:, :, 1]
        return jnp.einsum('bhqk,bkhe->bqhe', w.astype(v.dtype), v)

    out = lax.map(block, (qb, starts))
    return jnp.moveaxis(out, 0, 1).reshape(B, S, H, v.shape[-1])


def causal_mean(x, w):
    S = x.shape[1]
    cs = jnp.cumsum(x.astype(jnp.float32), axis=1)
    lower = jnp.pad(cs, ((0, 0), (w, 0), (0, 0)))[:, :S]
    count = jnp.minimum(jnp.arange(1, S + 1, dtype=jnp.float32), float(w))
    return ((cs - lower) / count[None, :, None]).astype(x.dtype)


def pool_mixer(z, w_pool, scale):
    B, S, _ = z.shape
    zg = z.reshape(B, S, N_POOL_GROUPS, POOL_GROUP_DIM)
    pooled = jnp.stack([causal_mean(zg[:, :, g], POOL_WINDOWS[g]) for g in range(N_POOL_GROUPS)], axis=2)
    y = jnp.einsum('bsgc,gcd->bsgd', pooled - zg, w_pool)
    return y.reshape(B, S, POOL_WIDTH) * scale


def conformer_conv(z, dw, dw_b, ln_g, ln_b, w_pw):
    a, b = jnp.split(z, 2, axis=-1)
    c = a * jax.nn.sigmoid(b)
    c = lax.conv_general_dilated(c, dw[:, None, :], window_strides=(1,), padding=[(CONV_KERNEL - 1, 0)],
                                 dimension_numbers=('NWC', 'WIO', 'NWC'),
                                 feature_group_count=CONV_WIDTH) + dw_b
    c = jax.nn.silu(layernorm(c, ln_g, ln_b))
    return c @ w_pw


def swiglu(x, wg, wu, wd):
    return (jax.nn.silu(x @ wg) * (x @ wu)) @ wd


def moe_swiglu(x, w_router, wg, wu, wd):
    logits = (x @ w_router).astype(jnp.float32)
    top_v, top_i = lax.top_k(logits, TOP_K)
    top_w = jax.nn.softmax(top_v, axis=-1)
    gates = jnp.sum(jax.nn.one_hot(top_i, N_EXPERTS, dtype=jnp.float32) * top_w[..., None], axis=-2).astype(x.dtype)
    y = jnp.zeros_like(x)
    for e in range(N_EXPERTS):
        y = y + gates[..., e:e + 1] * swiglu(x, wg[e], wu[e], wd[e])
    return y


def setup_inputs(seed: int = 0) -> dict:
    key = jax.random.key(seed)
    ks = jax.random.split(key, 32)
    f32 = jnp.float32

    def nrm(k, shape, scale):
        return jax.random.normal(k, shape, f32) * scale

    def gain(k, shape):
        return 1.0 + 0.05 * jax.random.normal(k, shape, f32)

    return {
        "x": nrm(ks[0], (BATCH, SEQ, D_MODEL), 1.0),
        "p": nrm(ks[1], (DEPTH, BATCH, SEQ, PLE_DIM), 1.0),
        "positions": jnp.broadcast_to(jnp.arange(SEQ, dtype=jnp.int32), (BATCH, SEQ)),
        "mix_norm": gain(ks[2], (DEPTH, D_MODEL)),
        "w_in": nrm(ks[3], (DEPTH, D_MODEL, IN_COLS), D_MODEL ** -0.5),
        "lambda_q1": nrm(ks[4], (DEPTH, ATTN_HEAD_DIM), 0.1),
        "lambda_k1": nrm(ks[5], (DEPTH, ATTN_HEAD_DIM), 0.1),
        "lambda_q2": nrm(ks[6], (DEPTH, ATTN_HEAD_DIM), 0.1),
        "lambda_k2": nrm(ks[7], (DEPTH, ATTN_HEAD_DIM), 0.1),
        "subln_gain": gain(ks[8], (DEPTH, ATTN_V_DIM)),
        "w_pool": nrm(ks[9], (DEPTH, N_POOL_GROUPS, POOL_GROUP_DIM, POOL_GROUP_DIM), POOL_GROUP_DIM ** -0.5),
        "pool_scale": gain(ks[10], (DEPTH, POOL_WIDTH)),
        "conv_dw": nrm(ks[11], (DEPTH, CONV_KERNEL, CONV_WIDTH), CONV_KERNEL ** -0.5),
        "conv_dw_bias": nrm(ks[12], (DEPTH, CONV_WIDTH), 0.02),
        "conv_ln_gain": gain(ks[13], (DEPTH, CONV_WIDTH)),
        "conv_ln_bias": nrm(ks[14], (DEPTH, CONV_WIDTH), 0.02),
        "w_conv_pw": nrm(ks[15], (DEPTH, CONV_WIDTH, CONV_WIDTH), CONV_WIDTH ** -0.5),
        "w_out": nrm(ks[16], (DEPTH, MIX_WIDTH, D_MODEL), MIX_WIDTH ** -0.5),
        "ffn_norm": gain(ks[17], (DEPTH, D_MODEL)),
        "w_dense_gate": nrm(ks[18], (N_DENSE, D_MODEL, D_FF), D_MODEL ** -0.5),
        "w_dense_up": nrm(ks[19], (N_DENSE, D_MODEL, D_FF), D_MODEL ** -0.5),
        "w_dense_down": nrm(ks[20], (N_DENSE, D_FF, D_MODEL), D_FF ** -0.5),
        "w_router": nrm(ks[21], (N_MOE, D_MODEL, N_EXPERTS), D_MODEL ** -0.5),
        "w_exp_gate": nrm(ks[22], (N_MOE, N_EXPERTS, D_MODEL, D_FF), D_MODEL ** -0.5),
        "w_exp_up": nrm(ks[23], (N_MOE, N_EXPERTS, D_MODEL, D_FF), D_MODEL ** -0.5),
        "w_exp_down": nrm(ks[24], (N_MOE, N_EXPERTS, D_FF, D_MODEL), D_FF ** -0.5),
        "ple_norm": gain(ks[25], (DEPTH, D_MODEL)),
        "w_ple_gate": nrm(ks[26], (DEPTH, D_MODEL, D_MODEL), D_MODEL ** -0.5),
        "w_ple_proj": nrm(ks[27], (DEPTH, PLE_DIM, D_MODEL), PLE_DIM ** -0.5),
        "final_norm": gain(ks[28], (D_MODEL,)),
    }


def reference(x, p, positions, mix_norm, w_in, lambda_q1, lambda_k1, lambda_q2, lambda_k2,
              subln_gain, w_pool, pool_scale, conv_dw, conv_dw_bias, conv_ln_gain, conv_ln_bias,
              w_conv_pw, w_out, ffn_norm, w_dense_gate, w_dense_up, w_dense_down, w_router,
              w_exp_gate, w_exp_up, w_exp_down, ple_norm, w_ple_gate, w_ple_proj, final_norm):
    B, S, _ = x.shape
    f32 = jnp.float32
    inv_freq = ROPE_THETA ** (-jnp.arange(0, ROPE_DIM, 2, dtype=f32) / ROPE_DIM)
    ang = positions.astype(f32)[..., None] * inv_freq
    cos = jnp.cos(ang)[:, :, None, None, :].astype(x.dtype)
    sin = jnp.sin(ang)[:, :, None, None, :].astype(x.dtype)

    h = x
    for i in range(DEPTH):
        lam_init = 0.8 - 0.6 * math.exp(-0.3 * i)
        u = rmsnorm(h, mix_norm[i])
        z = u @ w_in[i]
        zq, zk, zv, zp, zc = jnp.split(z, list(SPLIT_POINTS), axis=-1)
        q = rotary_partial(zq.reshape(B, S, N_ATTN_HEADS, 2, ATTN_HEAD_DIM), cos, sin)
        k = rotary_partial(zk.reshape(B, S, N_ATTN_HEADS, 2, ATTN_HEAD_DIM), cos, sin)
        v = zv.reshape(B, S, N_ATTN_HEADS, ATTN_V_DIM)
        lam = (jnp.exp(jnp.sum(lambda_q1[i].astype(f32) * lambda_k1[i].astype(f32)))
               - jnp.exp(jnp.sum(lambda_q2[i].astype(f32) * lambda_k2[i].astype(f32))) + lam_init)
        o = diff_attention(q, k, v, lam)
        o_attn = (rmsnorm(o, subln_gain[i], eps=SUBLN_EPS) * (1.0 - lam_init)).reshape(B, S, ATTN_WIDTH)
        o_pool = pool_mixer(zp, w_pool[i], pool_scale[i])
        o_conv = conformer_conv(zc, conv_dw[i], conv_dw_bias[i], conv_ln_gain[i], conv_ln_bias[i], w_conv_pw[i])
        h = h + jnp.concatenate([o_attn, o_pool, o_conv], axis=-1) @ w_out[i]
        u2 = rmsnorm(h, ffn_norm[i])
        j = i // 2
        if i % 2 == 0:
            f = swiglu(u2, w_dense_gate[j], w_dense_up[j], w_dense_down[j])
        else:
            f = moe_swiglu(u2, w_router[j], w_exp_gate[j], w_exp_up[j], w_exp_down[j])
        h = h + f
        g = jax.nn.sigmoid(rmsnorm(h, ple_norm[i]) @ w_ple_gate[i])
        h = h + (p[i] @ w_ple_proj[i]) * g
    return rmsnorm(h, final_norm)
```

```python
import functools
import math

import jax
import jax.numpy as jnp
from jax import lax
from jax.experimental import pallas as pl
from jax.experimental.pallas import tpu as pltpu

F32 = jnp.float32
BF16 = jnp.bfloat16

EPS = 1e-6
SUBLN_EPS = 1e-5
LN_EPS = 1e-5
HEAD_DIM = 64
V_DIM = 2 * HEAD_DIM
ROPE_DIM = HEAD_DIM // 4
ROPE_HALF = ROPE_DIM // 2
ROPE_THETA = 500000.0
POOL_WINDOWS = (2, 4, 8, 16)
POOL_GROUP_DIM = 128
CONV_KERNEL = 31
N_EXPERTS = 8
TOP_K = 2

LANES = 128
V7X_VMEM_BYTES = 64 * 1024 * 1024
VMEM_LIMIT = 56 * 1024 * 1024
POOL_HALO = 16
CONV_HALO = 32


def _cparams(sem):
    return pltpu.CompilerParams(dimension_semantics=sem, vmem_limit_bytes=VMEM_LIMIT)


def _resident(shape, index_map):
    return pl.BlockSpec(shape, index_map, pipeline_mode=pl.Buffered(1))


def _rms(x, gain, eps):
    return x * lax.rsqrt(jnp.mean(x * x, axis=-1, keepdims=True) + eps) * gain


def _rmsnorm_kernel(x_ref, g_ref, o_ref):
    o_ref[...] = _rms(x_ref[...], g_ref[...], EPS).astype(o_ref.dtype)


def _rmsnorm(x, gain, tm):
    t, d = x.shape
    return pl.pallas_call(
        _rmsnorm_kernel,
        grid=(t // tm,),
        in_specs=[pl.BlockSpec((tm, d), lambda i: (i, 0)), pl.BlockSpec((1, d), lambda i: (0, 0))],
        out_specs=pl.BlockSpec((tm, d), lambda i: (i, 0)),
        out_shape=jax.ShapeDtypeStruct((t, d), BF16),
        compiler_params=_cparams(("parallel",)),
        name="rmsnorm",
    )(x, gain.reshape(1, d))


def _inproj_kernel(u_ref, w_ref, c_ref, s1_ref, s2_ref, z_ref, *, n_rot_blocks, n_q_blocks, q_scale):
    j = pl.program_id(1)
    acc = jnp.dot(u_ref[...], w_ref[...], preferred_element_type=F32)
    tn = acc.shape[1]

    @pl.when(j < n_rot_blocks)
    def _():
        reps = tn // LANES
        c = jnp.concatenate([c_ref[...]] * reps, axis=1)
        s1 = jnp.concatenate([s1_ref[...]] * reps, axis=1)
        s2 = jnp.concatenate([s2_ref[...]] * reps, axis=1)
        nxt = pltpu.roll(acc, tn - ROPE_HALF, 1)
        prv = pltpu.roll(acc, ROPE_HALF, 1)
        r = acc * c + nxt * s1 + prv * s2
        scale = jnp.where(j < n_q_blocks, q_scale, 1.0).astype(F32)
        z_ref[...] = (r * scale).astype(z_ref.dtype)

    @pl.when(j >= n_rot_blocks)
    def _():
        z_ref[...] = acc.astype(z_ref.dtype)


def _inproj(u, w, rot_c, rot_s1, rot_s2, qk_cols, tm, tn):
    t, d = u.shape
    n = w.shape[1]
    kern = functools.partial(_inproj_kernel, n_rot_blocks=2 * qk_cols // tn, n_q_blocks=qk_cols // tn,
                             q_scale=HEAD_DIM ** -0.5)
    tab = pl.BlockSpec((tm, LANES), lambda i, j: (i, 0))
    return pl.pallas_call(
        kern,
        grid=(t // tm, n // tn),
        in_specs=[pl.BlockSpec((tm, d), lambda i, j: (i, 0)),
                  pl.BlockSpec((d, tn), lambda i, j: (0, j)),
                  tab, tab, tab],
        out_specs=pl.BlockSpec((tm, tn), lambda i, j: (i, j)),
        out_shape=jax.ShapeDtypeStruct((t, n), BF16),
        compiler_params=_cparams(("parallel", "arbitrary")),
        name="inproj",
    )(u, w, rot_c, rot_s1, rot_s2)


def _attn_kernel(lam_ref, gain_ref, q_ref, k_ref, v_ref, o_ref, qs_ref, m_ref, l_ref, acc_ref,
                 *, tq, lam_init):
    qi = pl.program_id(2)
    q = q_ref[...]
    lane = lax.broadcasted_iota(jnp.int32, q.shape, 1)
    zero = jnp.zeros_like(q)
    qs_ref[0:tq, :] = jnp.where(lane < HEAD_DIM, q, zero)
    qs_ref[tq:2 * tq, :] = jnp.where(lane >= HEAD_DIM, q, zero)
    m_ref[...] = jnp.full(m_ref.shape, -jnp.inf, F32)
    l_ref[...] = jnp.zeros(l_ref.shape, F32)
    acc_ref[...] = jnp.zeros(acc_ref.shape, F32)

    def step(j, masked):
        start = pl.multiple_of(j * tq, tq)
        k = k_ref[pl.ds(start, tq), :]
        v = v_ref[pl.ds(start, tq), :]
        s = lax.dot_general(qs_ref[...], k, (((1,), (1,)), ((), ())), preferred_element_type=F32)
        if masked:
            row = lax.broadcasted_iota(jnp.int32, s.shape, 0)
            col = lax.broadcasted_iota(jnp.int32, s.shape, 1)
            s = jnp.where(col <= (row & (tq - 1)), s, -jnp.inf)
        m_old = m_ref[...]
        m_new = jnp.maximum(m_old, jnp.max(s, axis=-1, keepdims=True))
        alpha = jnp.exp(m_old - m_new)
        p = jnp.exp(s - m_new)
        l_ref[...] = alpha * l_ref[...] + jnp.sum(p, axis=-1, keepdims=True)
        acc_ref[...] = alpha * acc_ref[...] + jnp.dot(p.astype(v.dtype), v, preferred_element_type=F32)
        m_ref[...] = m_new

    def body(j, carry):
        step(j, False)
        return carry

    lax.fori_loop(0, qi, body, 0)
    step(qi, True)

    lp = lam_ref[...]
    lam = (jnp.exp(jnp.sum(lp[0:1] * lp[1:2], axis=-1, keepdims=True))
           - jnp.exp(jnp.sum(lp[2:3] * lp[3:4], axis=-1, keepdims=True)) + lam_init)
    o_all = acc_ref[...] / l_ref[...]
    o = o_all[0:tq] - lam * o_all[tq:2 * tq]
    y = _rms(o, gain_ref[...], SUBLN_EPS) * (1.0 - lam_init)
    o_ref[...] = y.astype(o_ref.dtype)


def _attention(z, lam_params, subln_gain, batch, seq, n_heads, lam_init, tq):
    t = z.shape[0]
    nq = seq // tq
    assert tq & (tq - 1) == 0
    kern = functools.partial(_attn_kernel, tq=tq, lam_init=lam_init)
    return pl.pallas_call(
        kern,
        grid=(batch, n_heads, nq),
        in_specs=[pl.BlockSpec(lam_params.shape, lambda b, h, i: (0, 0)),
                  pl.BlockSpec((1, V_DIM), lambda b, h, i: (0, 0)),
                  pl.BlockSpec((tq, LANES), lambda b, h, i: (b * nq + i, h)),
                  pl.BlockSpec((seq, LANES), lambda b, h, i: (b, n_heads + h)),
                  pl.BlockSpec((seq, LANES), lambda b, h, i: (b, 2 * n_heads + h))],
        out_specs=pl.BlockSpec((tq, LANES), lambda b, h, i: (b * nq + i, h)),
        out_shape=jax.ShapeDtypeStruct((t, n_heads * V_DIM), BF16),
        scratch_shapes=[pltpu.VMEM((2 * tq, LANES), BF16),
                        pltpu.VMEM((2 * tq, 1), F32),
                        pltpu.VMEM((2 * tq, 1), F32),
                        pltpu.VMEM((2 * tq, V_DIM), F32)],
        compiler_params=_cparams(("parallel", "parallel", "arbitrary")),
        name="diff_attention",
    )(lam_params, subln_gain.reshape(1, V_DIM), z, z, z)


def _local_kernel(zp_ref, zph_ref, za_ref, zah_ref, zb_ref, zbh_ref, wpool_ref, pscale_ref,
                  dw_ref, dwb_ref, lng_ref, lnb_ref, wpw_ref, o_ref, pbuf_ref, cbuf_ref,
                  *, ts, tiles_per_seq):
    i = pl.program_id(0)
    tile_in_seq = i % tiles_per_seq
    first = tile_in_seq == 0
    pool_w = zp_ref.shape[1]

    zp = zp_ref[...].astype(F32)
    halo = zph_ref[...].astype(F32)
    pbuf_ref[0:POOL_HALO, :] = jnp.where(first, jnp.zeros_like(halo), halo)
    pbuf_ref[POOL_HALO:POOL_HALO + ts, :] = zp
    pos = tile_in_seq * ts + lax.broadcasted_iota(jnp.int32, (ts, 1), 0)
    for g, w in enumerate(POOL_WINDOWS):
        cols = slice(g * POOL_GROUP_DIM, (g + 1) * POOL_GROUP_DIM)
        zg = zp[:, cols]
        wsum = zg
        for k in range(1, w):
            wsum = wsum + pbuf_ref[POOL_HALO - k:POOL_HALO - k + ts, cols]
        count = jnp.minimum(pos + 1, w).astype(F32)
        d = (wsum / count - zg).astype(BF16)
        y = jnp.dot(d, wpool_ref[g], preferred_element_type=F32) * pscale_ref[:, cols]
        o_ref[:, cols] = y.astype(o_ref.dtype)

    c = za_ref[...].astype(F32) * jax.nn.sigmoid(zb_ref[...].astype(F32))
    ch = zah_ref[...].astype(F32) * jax.nn.sigmoid(zbh_ref[...].astype(F32))
    cbuf_ref[0:CONV_HALO, :] = jnp.where(first, jnp.zeros_like(ch), ch)
    cbuf_ref[CONV_HALO:CONV_HALO + ts, :] = c
    acc = jnp.zeros_like(c) + dwb_ref[...]
    base = CONV_HALO - (CONV_KERNEL - 1)
    for k in range(CONV_KERNEL):
        acc = acc + dw_ref[k:k + 1, :] * cbuf_ref[base + k:base + k + ts, :]
    mu = jnp.mean(acc, axis=-1, keepdims=True)
    xc = acc - mu
    yn = xc * lax.rsqrt(jnp.mean(xc * xc, axis=-1, keepdims=True) + LN_EPS) * lng_ref[...] + lnb_ref[...]
    sw = yn * jax.nn.sigmoid(yn)
    out = jnp.dot(sw.astype(BF16), wpw_ref[...], preferred_element_type=F32)
    o_ref[:, pool_w:] = out.astype(o_ref.dtype)


def _local_mixers(z, w_pool, pool_scale, dw, dw_b, ln_g, ln_b, w_pw, seq, pool_col, ts):
    t = z.shape[0]
    pool_w = pool_scale.shape[0]
    conv_w = dw.shape[1]
    assert pool_w == conv_w and pool_col % pool_w == 0
    pc = pool_col // pool_w
    hp = ts // POOL_HALO
    hc = ts // CONV_HALO
    kern = functools.partial(_local_kernel, ts=ts, tiles_per_seq=seq // ts)
    row = lambda shape: _resident(shape, lambda i: (0,) * len(shape))
    return pl.pallas_call(
        kern,
        grid=(t // ts,),
        in_specs=[pl.BlockSpec((ts, pool_w), lambda i: (i, pc)),
                  pl.BlockSpec((POOL_HALO, pool_w), lambda i: (jnp.maximum(i * hp - 1, 0), pc)),
                  pl.BlockSpec((ts, conv_w), lambda i: (i, pc + 1)),
                  pl.BlockSpec((CONV_HALO, conv_w), lambda i: (jnp.maximum(i * hc - 1, 0), pc + 1)),
                  pl.BlockSpec((ts, conv_w), lambda i: (i, pc + 2)),
                  pl.BlockSpec((CONV_HALO, conv_w), lambda i: (jnp.maximum(i * hc - 1, 0), pc + 2)),
                  row(w_pool.shape), row((1, pool_w)), row(dw.shape), row((1, conv_w)),
                  row((1, conv_w)), row((1, conv_w)), row(w_pw.shape)],
        out_specs=pl.BlockSpec((ts, pool_w + conv_w), lambda i: (i, 0)),
        out_shape=jax.ShapeDtypeStruct((t, pool_w + conv_w), BF16),
        scratch_shapes=[pltpu.VMEM((POOL_HALO + ts, pool_w), F32),
                        pltpu.VMEM((CONV_HALO + ts, conv_w), F32)],
        compiler_params=_cparams(("parallel",)),
        name="local_mixers",
    )(z, z, z, z, z, z, w_pool, pool_scale.reshape(1, pool_w), dw, dw_b.reshape(1, conv_w),
      ln_g.reshape(1, conv_w), ln_b.reshape(1, conv_w), w_pw)


def _outproj_kernel(oa_ref, ol_ref, w_ref, h_ref, g_ref, h1_ref, u_ref):
    ka = oa_ref.shape[1]
    mix = jnp.dot(oa_ref[...], w_ref[0:ka, :], preferred_element_type=F32)
    mix = mix + jnp.dot(ol_ref[...], w_ref[ka:, :], preferred_element_type=F32)
    h1 = h_ref[...] + mix
    h1_ref[...] = h1
    u_ref[...] = _rms(h1, g_ref[...], EPS).astype(u_ref.dtype)


def _outproj(o_attn, o_local, w_out, h, gain, tm):
    t, d = h.shape
    ka, kl = o_attn.shape[1], o_local.shape[1]
    return pl.pallas_call(
        _outproj_kernel,
        grid=(t // tm,),
        in_specs=[pl.BlockSpec((tm, ka), lambda i: (i, 0)),
                  pl.BlockSpec((tm, kl), lambda i: (i, 0)),
                  _resident((ka + kl, d), lambda i: (0, 0)),
                  pl.BlockSpec((tm, d), lambda i: (i, 0)),
                  _resident((1, d), lambda i: (0, 0))],
        out_specs=[pl.BlockSpec((tm, d), lambda i: (i, 0)), pl.BlockSpec((tm, d), lambda i: (i, 0))],
        out_shape=[jax.ShapeDtypeStruct((t, d), F32), jax.ShapeDtypeStruct((t, d), BF16)],
        compiler_params=_cparams(("parallel",)),
        name="outproj",
    )(o_attn, o_local, w_out, h, gain.reshape(1, d))


def _swiglu_partial(x, wg, wu, wd):
    g = jnp.dot(x, wg, preferred_element_type=F32)
    u = jnp.dot(x, wu, preferred_element_type=F32)
    hidden = (g * jax.nn.sigmoid(g)) * u
    return jnp.dot(hidden.astype(wd.dtype), wd, preferred_element_type=F32)


def _dense_ffn_kernel(u_ref, wg_ref, wu_ref, wd_ref, h_ref, g_ref, h2_ref, un_ref, acc_ref):
    f = pl.program_id(1)

    @pl.when(f == 0)
    def _():
        acc_ref[...] = jnp.zeros(acc_ref.shape, F32)

    acc_ref[...] += _swiglu_partial(u_ref[...], wg_ref[...], wu_ref[...], wd_ref[...])

    @pl.when(f == pl.num_programs(1) - 1)
    def _():
        h2 = h_ref[...] + acc_ref[...]
        h2_ref[...] = h2
        un_ref[...] = _rms(h2, g_ref[...], EPS).astype(un_ref.dtype)


def _dense_ffn(u, wg, wu, wd, h, gain, tm, tf):
    t, d = u.shape
    ff = wg.shape[1]
    return pl.pallas_call(
        _dense_ffn_kernel,
        grid=(t // tm, ff // tf),
        in_specs=[pl.BlockSpec((tm, d), lambda i, f: (i, 0)),
                  pl.BlockSpec((d, tf), lambda i, f: (0, f)),
                  pl.BlockSpec((d, tf), lambda i, f: (0, f)),
                  pl.BlockSpec((tf, d), lambda i, f: (f, 0)),
                  pl.BlockSpec((tm, d), lambda i, f: (i, 0)),
                  _resident((1, d), lambda i, f: (0, 0))],
        out_specs=[pl.BlockSpec((tm, d), lambda i, f: (i, 0)), pl.BlockSpec((tm, d), lambda i, f: (i, 0))],
        out_shape=[jax.ShapeDtypeStruct((t, d), F32), jax.ShapeDtypeStruct((t, d), BF16)],
        scratch_shapes=[pltpu.VMEM((tm, d), F32)],
        compiler_params=_cparams(("parallel", "arbitrary")),
        name="dense_ffn",
    )(u, wg, wu, wd, h, gain.reshape(1, d))


def _router_kernel(h_ref, g_ref, w_ref, o_ref):
    u = _rms(h_ref[...], g_ref[...], EPS)
    logits = jnp.dot(u, w_ref[...], preferred_element_type=F32, precision=lax.Precision.HIGHEST)
    lane = lax.broadcasted_iota(jnp.int32, logits.shape, 1)
    neg = jnp.full(logits.shape, -jnp.inf, F32)
    lg = jnp.where(lane < N_EXPERTS, logits, neg)
    m1 = jnp.max(lg, axis=-1, keepdims=True)
    i1 = jnp.min(jnp.where(lg == m1, lane, LANES), axis=-1, keepdims=True)
    lg2 = jnp.where(lane == i1, neg, lg)
    m2 = jnp.max(lg2, axis=-1, keepdims=True)
    i2 = jnp.min(jnp.where(lg2 == m2, lane, LANES), axis=-1, keepdims=True)
    e2 = jnp.exp(m2 - m1)
    g1 = 1.0 / (1.0 + e2)
    g2 = e2 / (1.0 + e2)
    out = jnp.where(lane == 0, g1, jnp.where(lane == 1, g2, jnp.where(
        lane == 2, i1.astype(F32), jnp.where(lane == 3, i2.astype(F32), 0.0))))
    o_ref[...] = out


def _router(h1, gain, w_router, tm):
    t, d = h1.shape
    w = jnp.zeros((d, LANES), F32).at[:, :N_EXPERTS].set(w_router)
    return pl.pallas_call(
        _router_kernel,
        grid=(t // tm,),
        in_specs=[pl.BlockSpec((tm, d), lambda i: (i, 0)),
                  _resident((1, d), lambda i: (0, 0)),
                  _resident((d, LANES), lambda i: (0, 0))],
        out_specs=pl.BlockSpec((tm, LANES), lambda i: (i, 0)),
        out_shape=jax.ShapeDtypeStruct((t, LANES), F32),
        compiler_params=_cparams(("parallel",)),
        name="router",
    )(h1, gain.reshape(1, d), w)


def _moe_ffn_kernel(te_ref, nu_ref, x_ref, wg_ref, wu_ref, wd_ref, y_ref, acc_ref):
    i = pl.program_id(0)
    f = pl.program_id(1)

    @pl.when(i < nu_ref[0])
    def _():
        @pl.when(f == 0)
        def _():
            acc_ref[...] = jnp.zeros(acc_ref.shape, F32)

        acc_ref[...] += _swiglu_partial(x_ref[...], wg_ref[...], wu_ref[...], wd_ref[...])

        @pl.when(f == pl.num_programs(1) - 1)
        def _():
            y_ref[...] = acc_ref[...].astype(y_ref.dtype)


def _moe_ffn(x_sorted, tile_expert, n_used, wg, wu, wd, tm, tf):
    p, d = x_sorted.shape
    ff = wg.shape[2]
    nf = ff // tf

    def row(i, f, te, nu):
        return (jnp.minimum(i, nu[0] - 1), 0)

    def fcol(i, f, nu):
        return jnp.where(i < nu[0], f, nf - 1)

    grid_spec = pltpu.PrefetchScalarGridSpec(
        num_scalar_prefetch=2,
        grid=(p // tm, nf),
        in_specs=[pl.BlockSpec((tm, d), row),
                  pl.BlockSpec((None, d, tf), lambda i, f, te, nu: (te[i], 0, fcol(i, f, nu))),
                  pl.BlockSpec((None, d, tf), lambda i, f, te, nu: (te[i], 0, fcol(i, f, nu))),
                  pl.BlockSpec((None, tf, d), lambda i, f, te, nu: (te[i], fcol(i, f, nu), 0))],
        out_specs=pl.BlockSpec((tm, d), row),
        scratch_shapes=[pltpu.VMEM((tm, d), F32)],
    )
    return pl.pallas_call(
        _moe_ffn_kernel,
        grid_spec=grid_spec,
        out_shape=jax.ShapeDtypeStruct((p, d), BF16),
        compiler_params=_cparams(("arbitrary", "arbitrary")),
        name="moe_ffn",
    )(tile_expert, n_used, x_sorted, wg, wu, wd)


def _combine_kernel(y0_ref, y1_ref, r_ref, h_ref, g_ref, h2_ref, un_ref):
    r = r_ref[...]
    f = r[:, 0:1] * y0_ref[...].astype(F32) + r[:, 1:2] * y1_ref[...].astype(F32)
    h2 = h_ref[...] + f
    h2_ref[...] = h2
    un_ref[...] = _rms(h2, g_ref[...], EPS).astype(un_ref.dtype)


def _combine(y0, y1, route, h1, gain, tm):
    t, d = h1.shape
    blk = pl.BlockSpec((tm, d), lambda i: (i, 0))
    return pl.pallas_call(
        _combine_kernel,
        grid=(t // tm,),
        in_specs=[blk, blk, pl.BlockSpec((tm, LANES), lambda i: (i, 0)), blk,
                  _resident((1, d), lambda i: (0, 0))],
        out_specs=[blk, blk],
        out_shape=[jax.ShapeDtypeStruct((t, d), F32), jax.ShapeDtypeStruct((t, d), BF16)],
        compiler_params=_cparams(("parallel",)),
        name="moe_combine",
    )(y0, y1, route, h1, gain.reshape(1, d))


def _moe(u2, h1, ffn_gain, w_router, wg, wu, wd, next_gain, tm_route, tm, tf):
    t, d = u2.shape
    route = _router(h1, ffn_gain, w_router, tm_route)
    expert = route[:, 2:2 + TOP_K].astype(jnp.int32).reshape(-1)
    onehot = (expert[:, None] == jnp.arange(N_EXPERTS, dtype=jnp.int32)[None, :]).astype(jnp.int32)
    csum = jnp.cumsum(onehot, axis=0)
    rank = jnp.sum(csum * onehot, axis=1) - 1
    counts = csum[-1]
    padded = ((counts + tm - 1) // tm) * tm
    pend = jnp.cumsum(padded)
    pstart = pend - padded
    dest = pstart[expert] + rank
    n_tiles = (TOP_K * t) // tm + N_EXPERTS
    p = n_tiles * tm
    token_of_row = jnp.zeros((p,), jnp.int32).at[dest].set(jnp.arange(TOP_K * t, dtype=jnp.int32) // TOP_K)
    tile_start = jnp.arange(n_tiles, dtype=jnp.int32) * tm
    tile_expert = jnp.minimum(jnp.sum((tile_start[:, None] >= pend[None, :]).astype(jnp.int32), axis=1),
                              N_EXPERTS - 1).astype(jnp.int32)
    n_used = (pend[-1] // tm).astype(jnp.int32).reshape(1)
    x_sorted = jnp.take(u2, token_of_row, axis=0)
    y_sorted = _moe_ffn(x_sorted, tile_expert, n_used, wg, wu, wd, tm, tf)
    dest2 = dest.reshape(t, TOP_K)
    y0 = jnp.take(y_sorted, dest2[:, 0], axis=0)
    y1 = jnp.take(y_sorted, dest2[:, 1], axis=0)
    return _combine(y0, y1, route, h1, next_gain, tm_route)


def _ple_kernel(u_ref, wg_ref, p_ref, wp_ref, h_ref, g_ref, *out_refs, final):
    gate = jax.nn.sigmoid(jnp.dot(u_ref[...], wg_ref[...], preferred_element_type=F32))
    emb = jnp.dot(p_ref[...].astype(BF16), wp_ref[...], preferred_element_type=F32)
    h3 = h_ref[...] + emb * gate
    if final:
        out_refs[0][...] = _rms(h3, g_ref[...], EPS)
    else:
        out_refs[0][...] = h3
        out_refs[1][...] = _rms(h3, g_ref[...], EPS).astype(out_refs[1].dtype)


def _ple(u3, w_gate, p, w_proj, h2, next_gain, final, tm):
    t, d = h2.shape
    pd = p.shape[1]
    blk = pl.BlockSpec((tm, d), lambda i: (i, 0))
    if final:
        out_specs = [blk]
        out_shape = [jax.ShapeDtypeStruct((t, d), F32)]
    else:
        out_specs = [blk, blk]
        out_shape = [jax.ShapeDtypeStruct((t, d), F32), jax.ShapeDtypeStruct((t, d), BF16)]
    return pl.pallas_call(
        functools.partial(_ple_kernel, final=final),
        grid=(t // tm,),
        in_specs=[blk, _resident((d, d), lambda i: (0, 0)),
                  pl.BlockSpec((tm, pd), lambda i: (i, 0)), _resident((pd, d), lambda i: (0, 0)),
                  blk, _resident((1, d), lambda i: (0, 0))],
        out_specs=out_specs,
        out_shape=out_shape,
        compiler_params=_cparams(("parallel",)),
        name="ple",
    )(u3, w_gate, p, w_proj, h2, next_gain.reshape(1, d))


def _tiles(t, seq, ff):
    return dict(
        tm_norm=min(512, t),
        tm_in=min(1024, t), tn_in=512,
        tq=min(256, seq),
        ts=min(512, seq),
        tm_out=min(512, t),
        tm_ffn=min(512, t), tf=min(512, ff),
        tm_moe=min(512, t),
        tm_ple=min(512, t),
    )


def _rotary_tables(positions):
    inv_freq = ROPE_THETA ** (-jnp.arange(0, ROPE_DIM, 2, dtype=F32) / ROPE_DIM)
    ang = positions.astype(F32).reshape(-1, 1) * inv_freq
    cos, sin = jnp.cos(ang), jnp.sin(ang)
    t = ang.shape[0]
    ones = jnp.ones((t, HEAD_DIM - ROPE_DIM), F32)
    zeros8 = jnp.zeros((t, ROPE_HALF), F32)
    zeros = jnp.zeros((t, HEAD_DIM - ROPE_DIM), F32)
    c = jnp.concatenate([cos, cos, ones], axis=1)
    s1 = jnp.concatenate([-sin, zeros8, zeros], axis=1)
    s2 = jnp.concatenate([zeros8, sin, zeros], axis=1)
    rep = LANES // HEAD_DIM
    return tuple(jnp.concatenate([a] * rep, axis=1) for a in (c, s1, s2))


def kernel(x, p, positions, mix_norm, w_in, lambda_q1, lambda_k1, lambda_q2, lambda_k2, subln_gain, w_pool, pool_scale, conv_dw, conv_dw_bias, conv_ln_gain, conv_ln_bias, w_conv_pw, w_out, ffn_norm, w_dense_gate, w_dense_up, w_dense_down, w_router, w_exp_gate, w_exp_up, w_exp_down, ple_norm, w_ple_gate, w_ple_proj, final_norm):
    batch, seq, d = x.shape
    depth = w_in.shape[0]
    t = batch * seq
    attn_w = (w_in.shape[2] - 2 * conv_dw.shape[2] - pool_scale.shape[1]) // 3
    n_heads = attn_w // V_DIM
    qk_cols = attn_w
    pool_col = 3 * attn_w
    ff = w_dense_gate.shape[2]
    ts = _tiles(t, seq, ff)

    rot_c, rot_s1, rot_s2 = _rotary_tables(positions)
    h = x.reshape(t, d)
    u = _rmsnorm(h, mix_norm[0], ts["tm_norm"])
    out = None
    for i in range(depth):
        lam_init = 0.8 - 0.6 * math.exp(-0.3 * i)
        final = i == depth - 1
        z = _inproj(u, w_in[i].astype(BF16), rot_c, rot_s1, rot_s2, qk_cols, ts["tm_in"], ts["tn_in"])
        lam_params = jnp.stack([lambda_q1[i], lambda_k1[i], lambda_q2[i], lambda_k2[i]]).astype(F32)
        o_attn = _attention(z, lam_params, subln_gain[i], batch, seq, n_heads, lam_init, ts["tq"])
        o_local = _local_mixers(z, w_pool[i].astype(BF16), pool_scale[i], conv_dw[i], conv_dw_bias[i],
                                conv_ln_gain[i], conv_ln_bias[i], w_conv_pw[i].astype(BF16),
                                seq, pool_col, ts["ts"])
        h1, u2 = _outproj(o_attn, o_local, w_out[i].astype(BF16), h, ffn_norm[i], ts["tm_out"])
        j = i // 2
        if i % 2 == 0:
            h2, u3 = _dense_ffn(u2, w_dense_gate[j].astype(BF16), w_dense_up[j].astype(BF16),
                                w_dense_down[j].astype(BF16), h1, ple_norm[i], ts["tm_ffn"], ts["tf"])
        else:
            h2, u3 = _moe(u2, h1, ffn_norm[i], w_router[j], w_exp_gate[j].astype(BF16),
                          w_exp_up[j].astype(BF16), w_exp_down[j].astype(BF16), ple_norm[i],
                          ts["tm_out"], ts["tm_moe"], ts["tf"])
        next_gain = final_norm if final else mix_norm[i + 1]
        res = _ple(u3, w_ple_gate[i].astype(BF16), p[i].reshape(t, -1), w_ple_proj[i].astype(BF16),
                   h2, next_gain, final, ts["tm_ple"])
        if final:
            out = res[0]
        else:
            h, u = res
    return out.reshape(batch, seq, d)
```

```python
import functools
import math

import jax
import jax.numpy as jnp
from jax import lax
from jax.experimental import pallas as pl
from jax.experimental.pallas import tpu as pltpu

F32 = jnp.float32
BF16 = jnp.bfloat16

EPS = 1e-6
SUBLN_EPS = 1e-5
LN_EPS = 1e-5
HEAD_DIM = 64
V_DIM = 2 * HEAD_DIM
ROPE_DIM = HEAD_DIM // 4
ROPE_HALF = ROPE_DIM // 2
ROPE_THETA = 500000.0
POOL_WINDOWS = (2, 4, 8, 16)
POOL_GROUP_DIM = 128
CONV_KERNEL = 31
N_EXPERTS = 8
TOP_K = 2

LANES = 128
V7X_VMEM_BYTES = 64 * 1024 * 1024
VMEM_LIMIT = 56 * 1024 * 1024
POOL_HALO = 16
CONV_HALO = 32


def _cparams(sem):
    return pltpu.CompilerParams(dimension_semantics=sem, vmem_limit_bytes=VMEM_LIMIT)


def _resident(shape, index_map):
    return pl.BlockSpec(shape, index_map, pipeline_mode=pl.Buffered(1))


def _rms(x, gain, eps):
    return x * lax.rsqrt(jnp.mean(x * x, axis=-1, keepdims=True) + eps) * gain


def _rmsnorm_kernel(x_ref, g_ref, o_ref):
    o_ref[...] = _rms(x_ref[...], g_ref[...], EPS).astype(o_ref.dtype)


def _rmsnorm(x, gain, tm):
    t, d = x.shape
    return pl.pallas_call(
        _rmsnorm_kernel,
        grid=(t // tm,),
        in_specs=[pl.BlockSpec((tm, d), lambda i: (i, 0)), pl.BlockSpec((1, d), lambda i: (0, 0))],
        out_specs=pl.BlockSpec((tm, d), lambda i: (i, 0)),
        out_shape=jax.ShapeDtypeStruct((t, d), BF16),
        compiler_params=_cparams(("parallel",)),
        name="rmsnorm",
    )(x, gain.reshape(1, d))


def _inproj_kernel(u_ref, w_ref, c_ref, s1_ref, s2_ref, z_ref, *, n_rot_blocks, n_q_blocks, q_scale):
    j = pl.program_id(1)
    acc = jnp.dot(u_ref[...], w_ref[...], preferred_element_type=F32)
    tn = acc.shape[1]

    @pl.when(j < n_rot_blocks)
    def _():
        reps = tn // LANES
        c = jnp.concatenate([c_ref[...]] * reps, axis=1)
        s1 = jnp.concatenate([s1_ref[...]] * reps, axis=1)
        s2 = jnp.concatenate([s2_ref[...]] * reps, axis=1)
        nxt = pltpu.roll(acc, tn - ROPE_HALF, 1)
        prv = pltpu.roll(acc, ROPE_HALF, 1)
        r = acc * c + nxt * s1 + prv * s2
        scale = jnp.where(j < n_q_blocks, q_scale, 1.0).astype(F32)
        z_ref[...] = (r * scale).astype(z_ref.dtype)

    @pl.when(j >= n_rot_blocks)
    def _():
        z_ref[...] = acc.astype(z_ref.dtype)


def _inproj(u, w, rot_c, rot_s1, rot_s2, qk_cols, tm, tn):
    t, d = u.shape
    n = w.shape[1]
    kern = functools.partial(_inproj_kernel, n_rot_blocks=2 * qk_cols // tn, n_q_blocks=qk_cols // tn,
                             q_scale=HEAD_DIM ** -0.5)
    tab = pl.BlockSpec((tm, LANES), lambda i, j: (i, 0))
    return pl.pallas_call(
        kern,
        grid=(t // tm, n // tn),
        in_specs=[pl.BlockSpec((tm, d), lambda i, j: (i, 0)),
                  pl.BlockSpec((d, tn), lambda i, j: (0, j)),
                  tab, tab, tab],
        out_specs=pl.BlockSpec((tm, tn), lambda i, j: (i, j)),
        out_shape=jax.ShapeDtypeStruct((t, n), BF16),
        compiler_params=_cparams(("parallel", "arbitrary")),
        name="inproj",
    )(u, w, rot_c, rot_s1, rot_s2)


def _attn_kernel(lam_ref, gain_ref, q_ref, k_ref, v_ref, o_ref, qs_ref, vx_ref, m_ref, acc_ref,
                 *, tq, tk, hp, lam_init):
    qi = pl.program_id(2)
    per = tk // tq

    @pl.when(qi == 0)
    def _():
        for a in range(hp):
            vx_ref[a, :, 0:V_DIM] = v_ref[:, a * V_DIM:(a + 1) * V_DIM]
            vx_ref[a, :, V_DIM:2 * V_DIM] = jnp.ones((v_ref.shape[0], V_DIM), v_ref.dtype)

    lane = lax.broadcasted_iota(jnp.int32, (tq, LANES), 1)
    for a in range(hp):
        q = q_ref[:, a * LANES:(a + 1) * LANES]
        zero = jnp.zeros_like(q)
        qs_ref[a, 0:tq, :] = jnp.where(lane < HEAD_DIM, q, zero)
        qs_ref[a, tq:2 * tq, :] = jnp.where(lane >= HEAD_DIM, q, zero)
    m_ref[...] = jnp.full(m_ref.shape, -jnp.inf, F32)
    acc_ref[...] = jnp.zeros(acc_ref.shape, F32)

    def step(j, masked):
        start = pl.multiple_of(j * tk, tk)
        for a in range(hp):
            k = k_ref[pl.ds(start, tk), a * LANES:(a + 1) * LANES]
            vx = vx_ref[a, pl.ds(start, tk), :]
            s = lax.dot_general(qs_ref[a], k, (((1,), (1,)), ((), ())), preferred_element_type=F32)
            if masked:
                row = lax.broadcasted_iota(jnp.int32, s.shape, 0)
                col = lax.broadcasted_iota(jnp.int32, s.shape, 1)
                s = jnp.where(col <= (row & (tq - 1)) + (qi % per) * tq, s, -jnp.inf)
            m_old = m_ref[a]
            m_new = jnp.maximum(m_old, jnp.max(s, axis=-1, keepdims=True))
            alpha = jnp.exp(m_old - m_new)
            p = jnp.exp(s - jnp.concatenate([m_new] * (tk // LANES), axis=1))
            pv = jnp.dot(p.astype(vx.dtype), vx, preferred_element_type=F32)
            acc_ref[a] = jnp.concatenate([alpha, alpha], axis=1) * acc_ref[a] + pv
            m_ref[a] = m_new

    def body(j, carry):
        step(j, False)
        return carry

    lax.fori_loop(0, qi // per, body, 0)
    step(qi // per, True)

    lp = lam_ref[...]
    lam = (jnp.exp(jnp.sum(lp[0:1] * lp[1:2], axis=-1, keepdims=True))
           - jnp.exp(jnp.sum(lp[2:3] * lp[3:4], axis=-1, keepdims=True)) + lam_init)
    for a in range(hp):
        o_all = acc_ref[a, :, 0:V_DIM] / acc_ref[a, :, V_DIM:2 * V_DIM]
        o = o_all[0:tq] - lam * o_all[tq:2 * tq]
        y = _rms(o, gain_ref[...], SUBLN_EPS) * (1.0 - lam_init)
        o_ref[:, a * V_DIM:(a + 1) * V_DIM] = y.astype(o_ref.dtype)


def _attention(z, lam_params, subln_gain, batch, seq, n_heads, lam_init, tq, tk, hp):
    t = z.shape[0]
    nq = seq // tq
    assert tq & (tq - 1) == 0 and tk % tq == 0 and seq % tk == 0 and n_heads % hp == 0
    kern = functools.partial(_attn_kernel, tq=tq, tk=tk, hp=hp, lam_init=lam_init)
    ng = n_heads // hp
    w = hp * LANES
    return pl.pallas_call(
        kern,
        grid=(batch, ng, nq),
        in_specs=[pl.BlockSpec(lam_params.shape, lambda b, h, i: (0, 0)),
                  pl.BlockSpec((1, V_DIM), lambda b, h, i: (0, 0)),
                  pl.BlockSpec((tq, w), lambda b, h, i: (b * nq + i, h)),
                  pl.BlockSpec((seq, w), lambda b, h, i: (b, ng + h)),
                  pl.BlockSpec((seq, w), lambda b, h, i: (b, 2 * ng + h))],
        out_specs=pl.BlockSpec((tq, w), lambda b, h, i: (b * nq + i, h)),
        out_shape=jax.ShapeDtypeStruct((t, n_heads * V_DIM), BF16),
        scratch_shapes=[pltpu.VMEM((hp, 2 * tq, LANES), BF16),
                        pltpu.VMEM((hp, seq, 2 * V_DIM), BF16),
                        pltpu.VMEM((hp, 2 * tq, LANES), F32),
                        pltpu.VMEM((hp, 2 * tq, 2 * V_DIM), F32)],
        compiler_params=_cparams(("parallel", "parallel", "arbitrary")),
        name="diff_attention",
    )(lam_params, subln_gain.reshape(1, V_DIM), z, z, z)


def _local_kernel(zp_ref, zph_ref, za_ref, zah_ref, zb_ref, zbh_ref, wpool_ref, pscale_ref,
                  dw_ref, dwb_ref, lng_ref, lnb_ref, wpw_ref, o_ref, pbuf_ref, cbuf_ref,
                  *, ts, tiles_per_seq):
    i = pl.program_id(0)
    tile_in_seq = i % tiles_per_seq
    first = tile_in_seq == 0
    pool_w = zp_ref.shape[1]

    zp = zp_ref[...].astype(F32)
    halo = zph_ref[...].astype(F32)
    pbuf_ref[0:POOL_HALO, :] = jnp.where(first, jnp.zeros_like(halo), halo)
    pbuf_ref[POOL_HALO:POOL_HALO + ts, :] = zp
    pos = tile_in_seq * ts + lax.broadcasted_iota(jnp.int32, (ts, 1), 0)
    for g, w in enumerate(POOL_WINDOWS):
        cols = slice(g * POOL_GROUP_DIM, (g + 1) * POOL_GROUP_DIM)
        zg = zp[:, cols]
        wsum = zg
        for k in range(1, w):
            wsum = wsum + pbuf_ref[POOL_HALO - k:POOL_HALO - k + ts, cols]
        count = jnp.minimum(pos + 1, w).astype(F32)
        d = (wsum / count - zg).astype(BF16)
        y = jnp.dot(d, wpool_ref[g], preferred_element_type=F32) * pscale_ref[:, cols]
        o_ref[:, cols] = y.astype(o_ref.dtype)

    c = za_ref[...].astype(F32) * jax.nn.sigmoid(zb_ref[...].astype(F32))
    ch = zah_ref[...].astype(F32) * jax.nn.sigmoid(zbh_ref[...].astype(F32))
    cbuf_ref[0:CONV_HALO, :] = jnp.where(first, jnp.zeros_like(ch), ch)
    cbuf_ref[CONV_HALO:CONV_HALO + ts, :] = c
    acc = jnp.zeros_like(c) + dwb_ref[...]
    base = CONV_HALO - (CONV_KERNEL - 1)
    for k in range(CONV_KERNEL):
        acc = acc + dw_ref[k:k + 1, :] * cbuf_ref[base + k:base + k + ts, :]
    mu = jnp.mean(acc, axis=-1, keepdims=True)
    xc = acc - mu
    yn = xc * lax.rsqrt(jnp.mean(xc * xc, axis=-1, keepdims=True) + LN_EPS) * lng_ref[...] + lnb_ref[...]
    sw = yn * jax.nn.sigmoid(yn)
    out = jnp.dot(sw.astype(BF16), wpw_ref[...], preferred_element_type=F32)
    o_ref[:, pool_w:] = out.astype(o_ref.dtype)


def _local_mixers(z, w_pool, pool_scale, dw, dw_b, ln_g, ln_b, w_pw, seq, pool_col, ts):
    t = z.shape[0]
    pool_w = pool_scale.shape[0]
    conv_w = dw.shape[1]
    assert pool_w == conv_w and pool_col % pool_w == 0
    pc = pool_col // pool_w
    hp = ts // POOL_HALO
    hc = ts // CONV_HALO
    kern = functools.partial(_local_kernel, ts=ts, tiles_per_seq=seq // ts)
    row = lambda shape: _resident(shape, lambda i: (0,) * len(shape))
    return pl.pallas_call(
        kern,
        grid=(t // ts,),
        in_specs=[pl.BlockSpec((ts, pool_w), lambda i: (i, pc)),
                  pl.BlockSpec((POOL_HALO, pool_w), lambda i: (jnp.maximum(i * hp - 1, 0), pc)),
                  pl.BlockSpec((ts, conv_w), lambda i: (i, pc + 1)),
                  pl.BlockSpec((CONV_HALO, conv_w), lambda i: (jnp.maximum(i * hc - 1, 0), pc + 1)),
                  pl.BlockSpec((ts, conv_w), lambda i: (i, pc + 2)),
                  pl.BlockSpec((CONV_HALO, conv_w), lambda i: (jnp.maximum(i * hc - 1, 0), pc + 2)),
                  row(w_pool.shape), row((1, pool_w)), row(dw.shape), row((1, conv_w)),
                  row((1, conv_w)), row((1, conv_w)), row(w_pw.shape)],
        out_specs=pl.BlockSpec((ts, pool_w + conv_w), lambda i: (i, 0)),
        out_shape=jax.ShapeDtypeStruct((t, pool_w + conv_w), BF16),
        scratch_shapes=[pltpu.VMEM((POOL_HALO + ts, pool_w), F32),
                        pltpu.VMEM((CONV_HALO + ts, conv_w), F32)],
        compiler_params=_cparams(("parallel",)),
        name="local_mixers",
    )(z, z, z, z, z, z, w_pool, pool_scale.reshape(1, pool_w), dw, dw_b.reshape(1, conv_w),
      ln_g.reshape(1, conv_w), ln_b.reshape(1, conv_w), w_pw)


def _outproj_kernel(oa_ref, ol_ref, w_ref, h_ref, g_ref, h1_ref, u_ref):
    ka = oa_ref.shape[1]
    mix = jnp.dot(oa_ref[...], w_ref[0:ka, :], preferred_element_type=F32)
    mix = mix + jnp.dot(ol_ref[...], w_ref[ka:, :], preferred_element_type=F32)
    h1 = h_ref[...] + mix
    h1_ref[...] = h1
    u_ref[...] = _rms(h1, g_ref[...], EPS).astype(u_ref.dtype)


def _outproj(o_attn, o_local, w_out, h, gain, tm):
    t, d = h.shape
    ka, kl = o_attn.shape[1], o_local.shape[1]
    return pl.pallas_call(
        _outproj_kernel,
        grid=(t // tm,),
        in_specs=[pl.BlockSpec((tm, ka), lambda i: (i, 0)),
                  pl.BlockSpec((tm, kl), lambda i: (i, 0)),
                  _resident((ka + kl, d), lambda i: (0, 0)),
                  pl.BlockSpec((tm, d), lambda i: (i, 0)),
                  _resident((1, d), lambda i: (0, 0))],
        out_specs=[pl.BlockSpec((tm, d), lambda i: (i, 0)), pl.BlockSpec((tm, d), lambda i: (i, 0))],
        out_shape=[jax.ShapeDtypeStruct((t, d), F32), jax.ShapeDtypeStruct((t, d), BF16)],
        compiler_params=_cparams(("parallel",)),
        name="outproj",
    )(o_attn, o_local, w_out, h, gain.reshape(1, d))


def _swiglu_partial(x, wg, wu, wd):
    g = jnp.dot(x, wg, preferred_element_type=F32)
    u = jnp.dot(x, wu, preferred_element_type=F32)
    hidden = (g * jax.nn.sigmoid(g)) * u
    return jnp.dot(hidden.astype(wd.dtype), wd, preferred_element_type=F32)


def _dense_ffn_kernel(u_ref, wg_ref, wu_ref, wd_ref, y_ref, acc_ref):
    f = pl.program_id(1)

    @pl.when(f == 0)
    def _():
        acc_ref[...] = jnp.zeros(acc_ref.shape, F32)

    acc_ref[...] += _swiglu_partial(u_ref[...], wg_ref[...], wu_ref[...], wd_ref[...])

    @pl.when(f == pl.num_programs(1) - 1)
    def _():
        y_ref[...] = acc_ref[...].astype(y_ref.dtype)


def _dense_ffn(u, wg, wu, wd, tm, tf):
    t, d = u.shape
    ff = wg.shape[1]
    return pl.pallas_call(
        _dense_ffn_kernel,
        grid=(t // tm, ff // tf),
        in_specs=[pl.BlockSpec((tm, d), lambda i, f: (i, 0)),
                  pl.BlockSpec((d, tf), lambda i, f: (0, f)),
                  pl.BlockSpec((d, tf), lambda i, f: (0, f)),
                  pl.BlockSpec((tf, d), lambda i, f: (f, 0))],
        out_specs=pl.BlockSpec((tm, d), lambda i, f: (i, 0)),
        out_shape=jax.ShapeDtypeStruct((t, d), BF16),
        scratch_shapes=[pltpu.VMEM((tm, d), F32)],
        compiler_params=_cparams(("parallel", "arbitrary")),
        name="dense_ffn",
    )(u, wg, wu, wd)


def _router_kernel(h_ref, g_ref, w_ref, o_ref):
    u = _rms(h_ref[...], g_ref[...], EPS)
    logits = jnp.dot(u, w_ref[...], preferred_element_type=F32, precision=lax.Precision.HIGHEST)
    lane = lax.broadcasted_iota(jnp.int32, logits.shape, 1)
    neg = jnp.full(logits.shape, -jnp.inf, F32)
    lg = jnp.where(lane < N_EXPERTS, logits, neg)
    m1 = jnp.max(lg, axis=-1, keepdims=True)
    i1 = jnp.min(jnp.where(lg == m1, lane, LANES), axis=-1, keepdims=True)
    lg2 = jnp.where(lane == i1, neg, lg)
    m2 = jnp.max(lg2, axis=-1, keepdims=True)
    i2 = jnp.min(jnp.where(lg2 == m2, lane, LANES), axis=-1, keepdims=True)
    e2 = jnp.exp(m2 - m1)
    g1 = 1.0 / (1.0 + e2)
    g2 = e2 / (1.0 + e2)
    out = jnp.where(lane == 0, g1, jnp.where(lane == 1, g2, jnp.where(
        lane == 2, i1.astype(F32), jnp.where(lane == 3, i2.astype(F32), 0.0))))
    o_ref[...] = out


def _router(h1, gain, w_router, tm):
    t, d = h1.shape
    w = jnp.zeros((d, LANES), F32).at[:, :N_EXPERTS].set(w_router)
    return pl.pallas_call(
        _router_kernel,
        grid=(t // tm,),
        in_specs=[pl.BlockSpec((tm, d), lambda i: (i, 0)),
                  _resident((1, d), lambda i: (0, 0)),
                  _resident((d, LANES), lambda i: (0, 0))],
        out_specs=pl.BlockSpec((tm, LANES), lambda i: (i, 0)),
        out_shape=jax.ShapeDtypeStruct((t, LANES), F32),
        compiler_params=_cparams(("parallel",)),
        name="router",
    )(h1, gain.reshape(1, d), w)


def _moe_ffn_kernel(te_ref, nu_ref, x_ref, wg_ref, wu_ref, wd_ref, y_ref, acc_ref):
    i = pl.program_id(0)
    f = pl.program_id(1)

    @pl.when(i < nu_ref[0])
    def _():
        @pl.when(f == 0)
        def _():
            acc_ref[...] = jnp.zeros(acc_ref.shape, F32)

        acc_ref[...] += _swiglu_partial(x_ref[...], wg_ref[...], wu_ref[...], wd_ref[...])

        @pl.when(f == pl.num_programs(1) - 1)
        def _():
            y_ref[...] = acc_ref[...].astype(y_ref.dtype)

    @pl.when(i >= nu_ref[0])
    def _():
        y_ref[...] = jnp.zeros(y_ref.shape, y_ref.dtype)


def _moe_ffn(x_sorted, tile_expert, n_used, wg, wu, wd, tm, tf):
    p, d = x_sorted.shape
    ff = wg.shape[2]
    nf = ff // tf

    def row(i, f, te, nu):
        return (jnp.minimum(i, nu[0] - 1), 0)

    def fcol(i, f, nu):
        return jnp.where(i < nu[0], f, nf - 1)

    grid_spec = pltpu.PrefetchScalarGridSpec(
        num_scalar_prefetch=2,
        grid=(p // tm, nf),
        in_specs=[pl.BlockSpec((tm, d), row),
                  pl.BlockSpec((None, d, tf), lambda i, f, te, nu: (te[i], 0, fcol(i, f, nu))),
                  pl.BlockSpec((None, d, tf), lambda i, f, te, nu: (te[i], 0, fcol(i, f, nu))),
                  pl.BlockSpec((None, tf, d), lambda i, f, te, nu: (te[i], fcol(i, f, nu), 0))],
        out_specs=pl.BlockSpec((tm, d), lambda i, f, te, nu: (i, 0)),
        scratch_shapes=[pltpu.VMEM((tm, d), F32)],
    )
    return pl.pallas_call(
        _moe_ffn_kernel,
        grid_spec=grid_spec,
        out_shape=jax.ShapeDtypeStruct((p, d), BF16),
        compiler_params=_cparams(("arbitrary", "arbitrary")),
        name="moe_ffn",
    )(tile_expert, n_used, x_sorted, wg, wu, wd)


def _moe(u2, h1, ffn_gain, w_router, wg, wu, wd, tm_route, tm, tf):
    t, d = u2.shape
    route = _router(h1, ffn_gain, w_router, tm_route)
    expert = route[:, 2:2 + TOP_K].astype(jnp.int32).reshape(-1)
    onehot = (expert[:, None] == jnp.arange(N_EXPERTS, dtype=jnp.int32)[None, :]).astype(jnp.int32)
    csum = jnp.cumsum(onehot, axis=0)
    rank = jnp.sum(csum * onehot, axis=1) - 1
    counts = csum[-1]
    padded = ((counts + tm - 1) // tm) * tm
    pend = jnp.cumsum(padded)
    pstart = pend - padded
    dest = pstart[expert] + rank
    n_tiles = (TOP_K * t) // tm + N_EXPERTS
    p = n_tiles * tm
    token_of_row = jnp.zeros((p,), jnp.int32).at[dest].set(jnp.arange(TOP_K * t, dtype=jnp.int32) // TOP_K)
    tile_start = jnp.arange(n_tiles, dtype=jnp.int32) * tm
    tile_expert = jnp.minimum(jnp.sum((tile_start[:, None] >= pend[None, :]).astype(jnp.int32), axis=1),
                              N_EXPERTS - 1).astype(jnp.int32)
    n_used = (pend[-1] // tm).astype(jnp.int32).reshape(1)
    x_sorted = jnp.take(u2, token_of_row, axis=0)
    y_sorted = _moe_ffn(x_sorted, tile_expert, n_used, wg, wu, wd, tm, tf)
    dest2 = dest.reshape(t, TOP_K)
    y0 = jnp.take(y_sorted, dest2[:, 0], axis=0)
    y1 = jnp.take(y_sorted, dest2[:, 1], axis=0)
    return [y0, y1], route


def _ple_kernel(*refs, n_y, gated, final):
    h_ref, y_refs = refs[0], refs[1:1 + n_y]
    rest = refs[1 + n_y:]
    if gated:
        r = rest[0][...]
        rest = rest[1:]
    gp_ref, wg_ref, p_ref, wp_ref, gn_ref = rest[:5]
    out_refs = rest[5:]
    h2 = h_ref[...]
    for k, y_ref in enumerate(y_refs):
        y = y_ref[...].astype(F32)
        h2 = h2 + (r[:, k:k + 1] * y if gated else y)
    u3 = _rms(h2, gp_ref[...], EPS).astype(BF16)
    gate = jax.nn.sigmoid(jnp.dot(u3, wg_ref[...], preferred_element_type=F32))
    emb = jnp.dot(p_ref[...].astype(BF16), wp_ref[...], preferred_element_type=F32)
    h3 = h2 + emb * gate
    if final:
        out_refs[0][...] = _rms(h3, gn_ref[...], EPS)
    else:
        out_refs[0][...] = h3
        out_refs[1][...] = _rms(h3, gn_ref[...], EPS).astype(out_refs[1].dtype)


def _ple(h1, ys, route, ple_gain, w_gate, p, w_proj, next_gain, final, tm):
    t, d = h1.shape
    pd = p.shape[1]
    blk = pl.BlockSpec((tm, d), lambda i: (i, 0))
    vec = _resident((1, d), lambda i: (0, 0))
    gated = route is not None
    in_specs = [blk] + [blk] * len(ys)
    args = [h1] + list(ys)
    if gated:
        in_specs.append(pl.BlockSpec((tm, LANES), lambda i: (i, 0)))
        args.append(route)
    in_specs += [vec, _resident((d, d), lambda i: (0, 0)), pl.BlockSpec((tm, pd), lambda i: (i, 0)),
                 _resident((pd, d), lambda i: (0, 0)), vec]
    args += [ple_gain.reshape(1, d), w_gate, p, w_proj, next_gain.reshape(1, d)]
    if final:
        out_specs = [blk]
        out_shape = [jax.ShapeDtypeStruct((t, d), F32)]
    else:
        out_specs = [blk, blk]
        out_shape = [jax.ShapeDtypeStruct((t, d), F32), jax.ShapeDtypeStruct((t, d), BF16)]
    return pl.pallas_call(
        functools.partial(_ple_kernel, n_y=len(ys), gated=gated, final=final),
        grid=(t // tm,),
        in_specs=in_specs,
        out_specs=out_specs,
        out_shape=out_shape,
        compiler_params=_cparams(("parallel",)),
        name="ple",
    )(*args)


def _tiles(t, seq, ff):
    return dict(
        tm_norm=min(512, t),
        tm_in=min(1024, t), tn_in=512,
        tq=min(256, seq), tk=min(512, seq), heads_per_step=4,
        ts=min(512, seq),
        tm_out=min(512, t),
        tm_ffn=min(1024, t), tf_ffn=min(512, ff),
        tm_moe=min(512, t), tf_moe=min(1024, ff),
        tm_ple=min(512, t),
    )


def _rotary_tables(positions):
    inv_freq = ROPE_THETA ** (-jnp.arange(0, ROPE_DIM, 2, dtype=F32) / ROPE_DIM)
    ang = positions.astype(F32).reshape(-1, 1) * inv_freq
    cos, sin = jnp.cos(ang), jnp.sin(ang)
    t = ang.shape[0]
    ones = jnp.ones((t, HEAD_DIM - ROPE_DIM), F32)
    zeros8 = jnp.zeros((t, ROPE_HALF), F32)
    zeros = jnp.zeros((t, HEAD_DIM - ROPE_DIM), F32)
    c = jnp.concatenate([cos, cos, ones], axis=1)
    s1 = jnp.concatenate([-sin, zeros8, zeros], axis=1)
    s2 = jnp.concatenate([zeros8, sin, zeros], axis=1)
    rep = LANES // HEAD_DIM
    return tuple(jnp.concatenate([a] * rep, axis=1) for a in (c, s1, s2))


def kernel(x, p, positions, mix_norm, w_in, lambda_q1, lambda_k1, lambda_q2, lambda_k2, subln_gain, w_pool, pool_scale, conv_dw, conv_dw_bias, conv_ln_gain, conv_ln_bias, w_conv_pw, w_out, ffn_norm, w_dense_gate, w_dense_up, w_dense_down, w_router, w_exp_gate, w_exp_up, w_exp_down, ple_norm, w_ple_gate, w_ple_proj, final_norm):
    batch, seq, d = x.shape
    depth = w_in.shape[0]
    t = batch * seq
    attn_w = (w_in.shape[2] - 2 * conv_dw.shape[2] - pool_scale.shape[1]) // 3
    n_heads = attn_w // V_DIM
    qk_cols = attn_w
    pool_col = 3 * attn_w
    ff = w_dense_gate.shape[2]
    ts = _tiles(t, seq, ff)

    rot_c, rot_s1, rot_s2 = _rotary_tables(positions)
    h = x.reshape(t, d)
    u = _rmsnorm(h, mix_norm[0], ts["tm_norm"])
    out = None
    for i in range(depth):
        lam_init = 0.8 - 0.6 * math.exp(-0.3 * i)
        final = i == depth - 1
        z = _inproj(u, w_in[i].astype(BF16), rot_c, rot_s1, rot_s2, qk_cols, ts["tm_in"], ts["tn_in"])
        lam_params = jnp.stack([lambda_q1[i], lambda_k1[i], lambda_q2[i], lambda_k2[i]]).astype(F32)
        o_attn = _attention(z, lam_params, subln_gain[i], batch, seq, n_heads, lam_init, ts["tq"],
                            ts["tk"], ts["heads_per_step"])
        o_local = _local_mixers(z, w_pool[i].astype(BF16), pool_scale[i], conv_dw[i], conv_dw_bias[i],
                                conv_ln_gain[i], conv_ln_bias[i], w_conv_pw[i].astype(BF16),
                                seq, pool_col, ts["ts"])
        h1, u2 = _outproj(o_attn, o_local, w_out[i].astype(BF16), h, ffn_norm[i], ts["tm_out"])
        j = i // 2
        if i % 2 == 0:
            ys = [_dense_ffn(u2, w_dense_gate[j].astype(BF16), w_dense_up[j].astype(BF16),
                             w_dense_down[j].astype(BF16), ts["tm_ffn"], ts["tf_ffn"])]
            route = None
        else:
            ys, route = _moe(u2, h1, ffn_norm[i], w_router[j], w_exp_gate[j].astype(BF16),
                             w_exp_up[j].astype(BF16), w_exp_down[j].astype(BF16),
                             ts["tm_out"], ts["tm_moe"], ts["tf_moe"])
        next_gain = final_norm if final else mix_norm[i + 1]
        res = _ple(h1, ys, route, ple_norm[i], w_ple_gate[i].astype(BF16), p[i].reshape(t, -1),
                   w_ple_proj[i].astype(BF16), next_gain, final, ts["tm_ple"])
        if final:
            out = res[0]
        else:
            h, u = res
    return out.reshape(batch, seq, d)
```

```python
import functools
import math

import jax
import jax.numpy as jnp
from jax import lax
from jax.experimental import pallas as pl
from jax.experimental.pallas import tpu as pltpu

F32 = jnp.float32
BF16 = jnp.bfloat16

EPS = 1e-6
SUBLN_EPS = 1e-5
LN_EPS = 1e-5
HEAD_DIM = 64
V_DIM = 2 * HEAD_DIM
ROPE_DIM = HEAD_DIM // 4
ROPE_HALF = ROPE_DIM // 2
ROPE_THETA = 500000.0
POOL_WINDOWS = (2, 4, 8, 16)
POOL_GROUP_DIM = 128
CONV_KERNEL = 31
N_EXPERTS = 8
TOP_K = 2

LANES = 128
V7X_VMEM_BYTES = 64 * 1024 * 1024
VMEM_LIMIT = 56 * 1024 * 1024
POOL_HALO = 16
CONV_HALO = 32


def _cparams(sem):
    return pltpu.CompilerParams(dimension_semantics=sem, vmem_limit_bytes=VMEM_LIMIT)


def _resident(shape, index_map):
    return pl.BlockSpec(shape, index_map, pipeline_mode=pl.Buffered(1))


def _rms(x, gain, eps):
    return x * lax.rsqrt(jnp.mean(x * x, axis=-1, keepdims=True) + eps) * gain


def _rmsnorm_kernel(x_ref, g_ref, o_ref):
    o_ref[...] = _rms(x_ref[...], g_ref[...], EPS).astype(o_ref.dtype)


def _rmsnorm(x, gain, tm):
    t, d = x.shape
    return pl.pallas_call(
        _rmsnorm_kernel,
        grid=(t // tm,),
        in_specs=[pl.BlockSpec((tm, d), lambda i: (i, 0)), pl.BlockSpec((1, d), lambda i: (0, 0))],
        out_specs=pl.BlockSpec((tm, d), lambda i: (i, 0)),
        out_shape=jax.ShapeDtypeStruct((t, d), BF16),
        compiler_params=_cparams(("parallel",)),
        name="rmsnorm",
    )(x, gain.reshape(1, d))


def _inproj_kernel(u_ref, w_ref, c_ref, s_ref, swap_ref, z_ref, *, n_rot_blocks, n_q_blocks, q_scale):
    j = pl.program_id(1)
    acc = jnp.dot(u_ref[...], w_ref[...], preferred_element_type=F32)
    tn = acc.shape[1]

    @pl.when(j < n_rot_blocks)
    def _():
        reps = tn // LANES
        c = jnp.concatenate([c_ref[...]] * reps, axis=1)
        s = jnp.concatenate([s_ref[...]] * reps, axis=1)
        pw = swap_ref.shape[0]
        ab = acc.astype(swap_ref.dtype)
        partner = jnp.concatenate(
            [jnp.dot(ab[:, b * pw:(b + 1) * pw], swap_ref[...], preferred_element_type=F32)
             for b in range(tn // pw)], axis=1)
        r = acc * c + partner * s
        scale = jnp.where(j < n_q_blocks, q_scale, 1.0).astype(F32)
        z_ref[...] = (r * scale).astype(z_ref.dtype)

    @pl.when(j >= n_rot_blocks)
    def _():
        z_ref[...] = acc.astype(z_ref.dtype)


def _inproj(u, w, rot_c, rot_s, swap, qk_cols, tm, tn):
    t, d = u.shape
    n = w.shape[1]
    kern = functools.partial(_inproj_kernel, n_rot_blocks=2 * qk_cols // tn, n_q_blocks=qk_cols // tn,
                             q_scale=HEAD_DIM ** -0.5)
    tab = pl.BlockSpec((tm, LANES), lambda i, j: (i, 0))
    return pl.pallas_call(
        kern,
        grid=(t // tm, n // tn),
        in_specs=[pl.BlockSpec((tm, d), lambda i, j: (i, 0)),
                  pl.BlockSpec((d, tn), lambda i, j: (0, j)),
                  tab, tab, _resident(swap.shape, lambda i, j: (0, 0))],
        out_specs=pl.BlockSpec((tm, tn), lambda i, j: (i, j)),
        out_shape=jax.ShapeDtypeStruct((t, n), BF16),
        compiler_params=_cparams(("parallel", "arbitrary")),
        name="inproj",
    )(u, w, rot_c, rot_s, swap)


def _attn_kernel(lam_ref, gain_ref, q_ref, k_ref, v_ref, o_ref, qs_ref, vx_ref, m_ref, acc_ref,
                 *, tq, tk, hp, lam_init):
    qi = pl.program_id(2)
    per = tk // tq

    @pl.when(qi == 0)
    def _():
        for a in range(hp):
            vx_ref[a, :, 0:V_DIM] = v_ref[:, a * V_DIM:(a + 1) * V_DIM]
            vx_ref[a, :, V_DIM:2 * V_DIM] = jnp.ones((v_ref.shape[0], V_DIM), v_ref.dtype)

    lane = lax.broadcasted_iota(jnp.int32, (tq, LANES), 1)
    for a in range(hp):
        q = q_ref[:, a * LANES:(a + 1) * LANES]
        zero = jnp.zeros_like(q)
        qs_ref[a, 0:tq, :] = jnp.where(lane < HEAD_DIM, q, zero)
        qs_ref[a, tq:2 * tq, :] = jnp.where(lane >= HEAD_DIM, q, zero)
    m_ref[...] = jnp.full(m_ref.shape, -jnp.inf, F32)
    acc_ref[...] = jnp.zeros(acc_ref.shape, F32)

    def step(j, masked):
        start = pl.multiple_of(j * tk, tk)
        for a in range(hp):
            k = k_ref[pl.ds(start, tk), a * LANES:(a + 1) * LANES]
            vx = vx_ref[a, pl.ds(start, tk), :]
            s = lax.dot_general(qs_ref[a], k, (((1,), (1,)), ((), ())), preferred_element_type=F32)
            if masked:
                row = lax.broadcasted_iota(jnp.int32, s.shape, 0)
                col = lax.broadcasted_iota(jnp.int32, s.shape, 1)
                s = jnp.where(col <= (row & (tq - 1)) + (qi % per) * tq, s, -jnp.inf)
            m_old = m_ref[a]
            m_new = jnp.maximum(m_old, jnp.max(s, axis=-1, keepdims=True))
            alpha = jnp.exp(m_old - m_new)
            p = jnp.exp(s - jnp.concatenate([m_new] * (tk // LANES), axis=1))
            pv = jnp.dot(p.astype(vx.dtype), vx, preferred_element_type=F32)
            acc_ref[a] = jnp.concatenate([alpha, alpha], axis=1) * acc_ref[a] + pv
            m_ref[a] = m_new

    def body(j, carry):
        step(j, False)
        return carry

    lax.fori_loop(0, qi // per, body, 0)
    step(qi // per, True)

    lp = lam_ref[...]
    lam = (jnp.exp(jnp.sum(lp[0:1] * lp[1:2], axis=-1, keepdims=True))
           - jnp.exp(jnp.sum(lp[2:3] * lp[3:4], axis=-1, keepdims=True)) + lam_init)
    for a in range(hp):
        o_all = acc_ref[a, :, 0:V_DIM] / acc_ref[a, :, V_DIM:2 * V_DIM]
        o = o_all[0:tq] - lam * o_all[tq:2 * tq]
        y = _rms(o, gain_ref[...], SUBLN_EPS) * (1.0 - lam_init)
        o_ref[:, a * V_DIM:(a + 1) * V_DIM] = y.astype(o_ref.dtype)


def _attention(z, lam_params, subln_gain, batch, seq, n_heads, lam_init, tq, tk, hp):
    t = z.shape[0]
    nq = seq // tq
    assert tq & (tq - 1) == 0 and tk % tq == 0 and seq % tk == 0 and n_heads % hp == 0
    kern = functools.partial(_attn_kernel, tq=tq, tk=tk, hp=hp, lam_init=lam_init)
    ng = n_heads // hp
    w = hp * LANES
    return pl.pallas_call(
        kern,
        grid=(batch, ng, nq),
        in_specs=[pl.BlockSpec(lam_params.shape, lambda b, h, i: (0, 0)),
                  pl.BlockSpec((1, V_DIM), lambda b, h, i: (0, 0)),
                  pl.BlockSpec((tq, w), lambda b, h, i: (b * nq + i, h)),
                  pl.BlockSpec((seq, w), lambda b, h, i: (b, ng + h)),
                  pl.BlockSpec((seq, w), lambda b, h, i: (b, 2 * ng + h))],
        out_specs=pl.BlockSpec((tq, w), lambda b, h, i: (b * nq + i, h)),
        out_shape=jax.ShapeDtypeStruct((t, n_heads * V_DIM), BF16),
        scratch_shapes=[pltpu.VMEM((hp, 2 * tq, LANES), BF16),
                        pltpu.VMEM((hp, seq, 2 * V_DIM), BF16),
                        pltpu.VMEM((hp, 2 * tq, LANES), F32),
                        pltpu.VMEM((hp, 2 * tq, 2 * V_DIM), F32)],
        compiler_params=_cparams(("parallel", "parallel", "arbitrary")),
        name="diff_attention",
    )(lam_params, subln_gain.reshape(1, V_DIM), z, z, z)


def _local_kernel(zp_ref, zph_ref, za_ref, zah_ref, zb_ref, zbh_ref, wpool_ref, pscale_ref,
                  dw_ref, dwb_ref, lng_ref, lnb_ref, wpw_ref, o_ref, pbuf_ref, cbuf_ref, sbuf_ref,
                  *, ts, tiles_per_seq):
    i = pl.program_id(0)
    tile_in_seq = i % tiles_per_seq
    first = tile_in_seq == 0
    pool_w = zp_ref.shape[1]

    zp = zp_ref[...].astype(F32)
    halo = zph_ref[...].astype(F32)
    pbuf_ref[0:POOL_HALO, :] = jnp.where(first, jnp.zeros_like(halo), halo)
    pbuf_ref[POOL_HALO:POOL_HALO + ts, :] = zp
    pos = tile_in_seq * ts + lax.broadcasted_iota(jnp.int32, (ts, 1), 0)
    for g, w in enumerate(POOL_WINDOWS):
        cols = slice(g * POOL_GROUP_DIM, (g + 1) * POOL_GROUP_DIM)
        zg = zp[:, cols]
        wsum = zg
        for k in range(1, w):
            wsum = wsum + pbuf_ref[POOL_HALO - k:POOL_HALO - k + ts, cols]
        count = jnp.minimum(pos + 1, w).astype(F32)
        d = (wsum / count - zg).astype(BF16)
        y = jnp.dot(d, wpool_ref[g], preferred_element_type=F32) * pscale_ref[:, cols]
        o_ref[:, cols] = y.astype(o_ref.dtype)

    c = za_ref[...].astype(F32) * jax.nn.sigmoid(zb_ref[...].astype(F32))
    ch = zah_ref[...].astype(F32) * jax.nn.sigmoid(zbh_ref[...].astype(F32))
    cbuf_ref[0:CONV_HALO, :] = jnp.where(first, jnp.zeros_like(ch), ch)
    cbuf_ref[CONV_HALO:CONV_HALO + ts, :] = c
    acc = jnp.zeros_like(c) + dwb_ref[...]
    base = CONV_HALO - (CONV_KERNEL - 1)
    sub = 8
    for b in range(sub):
        offs = [o for o in range(base, base + CONV_KERNEL) if o % sub == b]
        if not offs:
            continue
        span = offs[-1] - offs[0] + ts
        sbuf_ref[0:span, :] = cbuf_ref[offs[0]:offs[0] + span, :]
        for o in offs:
            k = o - base
            acc = acc + dw_ref[k:k + 1, :] * sbuf_ref[o - offs[0]:o - offs[0] + ts, :]
    mu = jnp.mean(acc, axis=-1, keepdims=True)
    xc = acc - mu
    yn = xc * lax.rsqrt(jnp.mean(xc * xc, axis=-1, keepdims=True) + LN_EPS) * lng_ref[...] + lnb_ref[...]
    sw = yn * jax.nn.sigmoid(yn)
    out = jnp.dot(sw.astype(BF16), wpw_ref[...], preferred_element_type=F32)
    o_ref[:, pool_w:] = out.astype(o_ref.dtype)


def _local_mixers(z, w_pool, pool_scale, dw, dw_b, ln_g, ln_b, w_pw, seq, pool_col, ts):
    t = z.shape[0]
    pool_w = pool_scale.shape[0]
    conv_w = dw.shape[1]
    assert pool_w == conv_w and pool_col % pool_w == 0
    pc = pool_col // pool_w
    hp = ts // POOL_HALO
    hc = ts // CONV_HALO
    kern = functools.partial(_local_kernel, ts=ts, tiles_per_seq=seq // ts)
    row = lambda shape: _resident(shape, lambda i: (0,) * len(shape))
    return pl.pallas_call(
        kern,
        grid=(t // ts,),
        in_specs=[pl.BlockSpec((ts, pool_w), lambda i: (i, pc)),
                  pl.BlockSpec((POOL_HALO, pool_w), lambda i: (jnp.maximum(i * hp - 1, 0), pc)),
                  pl.BlockSpec((ts, conv_w), lambda i: (i, pc + 1)),
                  pl.BlockSpec((CONV_HALO, conv_w), lambda i: (jnp.maximum(i * hc - 1, 0), pc + 1)),
                  pl.BlockSpec((ts, conv_w), lambda i: (i, pc + 2)),
                  pl.BlockSpec((CONV_HALO, conv_w), lambda i: (jnp.maximum(i * hc - 1, 0), pc + 2)),
                  row(w_pool.shape), row((1, pool_w)), row(dw.shape), row((1, conv_w)),
                  row((1, conv_w)), row((1, conv_w)), row(w_pw.shape)],
        out_specs=pl.BlockSpec((ts, pool_w + conv_w), lambda i: (i, 0)),
        out_shape=jax.ShapeDtypeStruct((t, pool_w + conv_w), BF16),
        scratch_shapes=[pltpu.VMEM((POOL_HALO + ts, pool_w), F32),
                        pltpu.VMEM((CONV_HALO + ts, conv_w), F32),
                        pltpu.VMEM((CONV_HALO + ts, conv_w), F32)],
        compiler_params=_cparams(("parallel",)),
        name="local_mixers",
    )(z, z, z, z, z, z, w_pool, pool_scale.reshape(1, pool_w), dw, dw_b.reshape(1, conv_w),
      ln_g.reshape(1, conv_w), ln_b.reshape(1, conv_w), w_pw)


def _outproj_kernel(oa_ref, ol_ref, w_ref, h_ref, g_ref, h1_ref, u_ref):
    ka = oa_ref.shape[1]
    mix = jnp.dot(oa_ref[...], w_ref[0:ka, :], preferred_element_type=F32)
    mix = mix + jnp.dot(ol_ref[...], w_ref[ka:, :], preferred_element_type=F32)
    h1 = h_ref[...] + mix
    h1_ref[...] = h1
    u_ref[...] = _rms(h1, g_ref[...], EPS).astype(u_ref.dtype)


def _outproj(o_attn, o_local, w_out, h, gain, tm):
    t, d = h.shape
    ka, kl = o_attn.shape[1], o_local.shape[1]
    return pl.pallas_call(
        _outproj_kernel,
        grid=(t // tm,),
        in_specs=[pl.BlockSpec((tm, ka), lambda i: (i, 0)),
                  pl.BlockSpec((tm, kl), lambda i: (i, 0)),
                  _resident((ka + kl, d), lambda i: (0, 0)),
                  pl.BlockSpec((tm, d), lambda i: (i, 0)),
                  _resident((1, d), lambda i: (0, 0))],
        out_specs=[pl.BlockSpec((tm, d), lambda i: (i, 0)), pl.BlockSpec((tm, d), lambda i: (i, 0))],
        out_shape=[jax.ShapeDtypeStruct((t, d), F32), jax.ShapeDtypeStruct((t, d), BF16)],
        compiler_params=_cparams(("parallel",)),
        name="outproj",
    )(o_attn, o_local, w_out, h, gain.reshape(1, d))


def _swiglu_partial(x, wg, wu, wd):
    g = jnp.dot(x, wg, preferred_element_type=F32)
    u = jnp.dot(x, wu, preferred_element_type=F32)
    hidden = (g * jax.nn.sigmoid(g)) * u
    return jnp.dot(hidden.astype(wd.dtype), wd, preferred_element_type=F32)


def _dense_ffn_kernel(*refs, n_cast):
    u_ref, wg_ref, wu_ref, wd_ref = refs[:4]
    cast_in = refs[4:4 + n_cast]
    y_ref = refs[4 + n_cast]
    cast_out = refs[5 + n_cast:5 + 2 * n_cast]
    acc_ref = refs[5 + 2 * n_cast]
    f = pl.program_id(1)

    @pl.when(f == 0)
    def _():
        acc_ref[...] = jnp.zeros(acc_ref.shape, F32)

    acc_ref[...] += _swiglu_partial(u_ref[...], wg_ref[...], wu_ref[...], wd_ref[...])

    for src, dst in zip(cast_in, cast_out):
        dst[...] = src[...].astype(dst.dtype)

    @pl.when(f == pl.num_programs(1) - 1)
    def _():
        y_ref[...] = acc_ref[...].astype(y_ref.dtype)


def _cast_slices(arrays_axes, n_steps, nf):
    specs = []
    for arr, axis in arrays_axes:
        e = arr.shape[0]
        if n_steps % e:
            return None
        per = n_steps // e
        if arr.shape[axis] % per:
            return None
        width = arr.shape[axis] // per
        align = LANES if axis == 2 else 16
        if width % align:
            return None
        block = tuple(None if a == 0 else (width if a == axis else arr.shape[a]) for a in range(3))

        def imap(i, f, per=per, axis=axis):
            s = i * nf + f
            return (s // per, s % per, 0) if axis == 1 else (s // per, 0, s % per)

        specs.append(pl.BlockSpec(block, imap))
    return specs


def _dense_ffn(u, wg, wu, wd, tm, tf, casts=()):
    t, d = u.shape
    ff = wg.shape[1]
    grid = (t // tm, ff // tf)
    cast_specs = _cast_slices(casts, grid[0] * grid[1], grid[1]) if casts else []
    if cast_specs is None:
        cast_specs, casts, unsupported = [], (), True
    else:
        unsupported = False
    n_cast = len(cast_specs)
    outs = pl.pallas_call(
        functools.partial(_dense_ffn_kernel, n_cast=n_cast),
        grid=grid,
        in_specs=[pl.BlockSpec((tm, d), lambda i, f: (i, 0)),
                  pl.BlockSpec((d, tf), lambda i, f: (0, f)),
                  pl.BlockSpec((d, tf), lambda i, f: (0, f)),
                  pl.BlockSpec((tf, d), lambda i, f: (f, 0))] + cast_specs,
        out_specs=[pl.BlockSpec((tm, d), lambda i, f: (i, 0))] + cast_specs,
        out_shape=[jax.ShapeDtypeStruct((t, d), BF16)]
        + [jax.ShapeDtypeStruct(a.shape, BF16) for a, _ in casts],
        scratch_shapes=[pltpu.VMEM((tm, d), F32)],
        compiler_params=_cparams(("parallel", "arbitrary")),
        name="dense_ffn",
    )(u, wg, wu, wd, *[a for a, _ in casts])
    return outs[0], (None if unsupported else list(outs[1:]))


def _router_kernel(h_ref, g_ref, w_ref, o_ref):
    u = _rms(h_ref[...], g_ref[...], EPS)
    logits = jnp.dot(u, w_ref[...], preferred_element_type=F32, precision=lax.Precision.HIGHEST)
    lane = lax.broadcasted_iota(jnp.int32, logits.shape, 1)
    neg = jnp.full(logits.shape, -jnp.inf, F32)
    lg = jnp.where(lane < N_EXPERTS, logits, neg)
    m1 = jnp.max(lg, axis=-1, keepdims=True)
    i1 = jnp.min(jnp.where(lg == m1, lane, LANES), axis=-1, keepdims=True)
    lg2 = jnp.where(lane == i1, neg, lg)
    m2 = jnp.max(lg2, axis=-1, keepdims=True)
    i2 = jnp.min(jnp.where(lg2 == m2, lane, LANES), axis=-1, keepdims=True)
    e2 = jnp.exp(m2 - m1)
    g1 = 1.0 / (1.0 + e2)
    g2 = e2 / (1.0 + e2)
    out = jnp.where(lane == 0, g1, jnp.where(lane == 1, g2, jnp.where(
        lane == 2, i1.astype(F32), jnp.where(lane == 3, i2.astype(F32), 0.0))))
    o_ref[...] = out


def _router(h1, gain, w_router, tm):
    t, d = h1.shape
    w = jnp.zeros((d, LANES), F32).at[:, :N_EXPERTS].set(w_router)
    return pl.pallas_call(
        _router_kernel,
        grid=(t // tm,),
        in_specs=[pl.BlockSpec((tm, d), lambda i: (i, 0)),
                  _resident((1, d), lambda i: (0, 0)),
                  _resident((d, LANES), lambda i: (0, 0))],
        out_specs=pl.BlockSpec((tm, LANES), lambda i: (i, 0)),
        out_shape=jax.ShapeDtypeStruct((t, LANES), F32),
        compiler_params=_cparams(("parallel",)),
        name="router",
    )(h1, gain.reshape(1, d), w)


def _moe_ffn_kernel(te_ref, nu_ref, x_ref, wg_ref, wu_ref, wd_ref, y_ref, acc_ref):
    i = pl.program_id(0)
    f = pl.program_id(1)

    @pl.when(i < nu_ref[0])
    def _():
        @pl.when(f == 0)
        def _():
            acc_ref[...] = jnp.zeros(acc_ref.shape, F32)

        acc_ref[...] += _swiglu_partial(x_ref[...], wg_ref[...], wu_ref[...], wd_ref[...])

        @pl.when(f == pl.num_programs(1) - 1)
        def _():
            y_ref[...] = acc_ref[...].astype(y_ref.dtype)

    @pl.when(i >= nu_ref[0])
    def _():
        y_ref[...] = jnp.zeros(y_ref.shape, y_ref.dtype)


def _moe_ffn(x_sorted, tile_expert, n_used, wg, wu, wd, tm, tf):
    p, d = x_sorted.shape
    ff = wg.shape[2]
    nf = ff // tf

    def row(i, f, te, nu):
        return (jnp.minimum(i, nu[0] - 1), 0)

    def fcol(i, f, nu):
        return jnp.where(i < nu[0], f, nf - 1)

    grid_spec = pltpu.PrefetchScalarGridSpec(
        num_scalar_prefetch=2,
        grid=(p // tm, nf),
        in_specs=[pl.BlockSpec((tm, d), row),
                  pl.BlockSpec((None, d, tf), lambda i, f, te, nu: (te[i], 0, fcol(i, f, nu))),
                  pl.BlockSpec((None, d, tf), lambda i, f, te, nu: (te[i], 0, fcol(i, f, nu))),
                  pl.BlockSpec((None, tf, d), lambda i, f, te, nu: (te[i], fcol(i, f, nu), 0))],
        out_specs=pl.BlockSpec((tm, d), lambda i, f, te, nu: (i, 0)),
        scratch_shapes=[pltpu.VMEM((tm, d), F32)],
    )
    return pl.pallas_call(
        _moe_ffn_kernel,
        grid_spec=grid_spec,
        out_shape=jax.ShapeDtypeStruct((p, d), BF16),
        compiler_params=_cparams(("arbitrary", "arbitrary")),
        name="moe_ffn",
    )(tile_expert, n_used, x_sorted, wg, wu, wd)


def _moe(u2, h1, ffn_gain, w_router, wg, wu, wd, tm_route, tm, tf):
    t, d = u2.shape
    route = _router(h1, ffn_gain, w_router, tm_route)
    expert = route[:, 2:2 + TOP_K].astype(jnp.int32).reshape(-1)
    onehot = (expert[:, None] == jnp.arange(N_EXPERTS, dtype=jnp.int32)[None, :]).astype(jnp.int32)
    csum = jnp.cumsum(onehot, axis=0)
    rank = jnp.sum(csum * onehot, axis=1) - 1
    counts = csum[-1]
    padded = ((counts + tm - 1) // tm) * tm
    pend = jnp.cumsum(padded)
    pstart = pend - padded
    dest = pstart[expert] + rank
    n_tiles = (TOP_K * t) // tm + N_EXPERTS
    p = n_tiles * tm
    token_of_row = jnp.zeros((p,), jnp.int32).at[dest].set(jnp.arange(TOP_K * t, dtype=jnp.int32) // TOP_K)
    tile_start = jnp.arange(n_tiles, dtype=jnp.int32) * tm
    tile_expert = jnp.minimum(jnp.sum((tile_start[:, None] >= pend[None, :]).astype(jnp.int32), axis=1),
                              N_EXPERTS - 1).astype(jnp.int32)
    n_used = (pend[-1] // tm).astype(jnp.int32).reshape(1)
    x_sorted = jnp.take(u2, token_of_row, axis=0)
    y_sorted = _moe_ffn(x_sorted, tile_expert, n_used, wg, wu, wd, tm, tf)
    dest2 = dest.reshape(t, TOP_K)
    y0 = jnp.take(y_sorted, dest2[:, 0], axis=0)
    y1 = jnp.take(y_sorted, dest2[:, 1], axis=0)
    return [y0, y1], route


def _ple_kernel(*refs, n_y, gated, final):
    h_ref, y_refs = refs[0], refs[1:1 + n_y]
    rest = refs[1 + n_y:]
    if gated:
        r = rest[0][...]
        rest = rest[1:]
    gp_ref, wg_ref, p_ref, wp_ref, gn_ref = rest[:5]
    out_refs = rest[5:]
    h2 = h_ref[...]
    for k, y_ref in enumerate(y_refs):
        y = y_ref[...].astype(F32)
        h2 = h2 + (r[:, k:k + 1] * y if gated else y)
    u3 = _rms(h2, gp_ref[...], EPS).astype(BF16)
    gate = jax.nn.sigmoid(jnp.dot(u3, wg_ref[...], preferred_element_type=F32))
    emb = jnp.dot(p_ref[...].astype(BF16), wp_ref[...], preferred_element_type=F32)
    h3 = h2 + emb * gate
    if final:
        out_refs[0][...] = _rms(h3, gn_ref[...], EPS)
    else:
        out_refs[0][...] = h3
        out_refs[1][...] = _rms(h3, gn_ref[...], EPS).astype(out_refs[1].dtype)


def _ple(h1, ys, route, ple_gain, w_gate, p, w_proj, next_gain, final, tm):
    t, d = h1.shape
    pd = p.shape[1]
    blk = pl.BlockSpec((tm, d), lambda i: (i, 0))
    vec = _resident((1, d), lambda i: (0, 0))
    gated = route is not None
    in_specs = [blk] + [blk] * len(ys)
    args = [h1] + list(ys)
    if gated:
        in_specs.append(pl.BlockSpec((tm, LANES), lambda i: (i, 0)))
        args.append(route)
    in_specs += [vec, _resident((d, d), lambda i: (0, 0)), pl.BlockSpec((tm, pd), lambda i: (i, 0)),
                 _resident((pd, d), lambda i: (0, 0)), vec]
    args += [ple_gain.reshape(1, d), w_gate, p, w_proj, next_gain.reshape(1, d)]
    if final:
        out_specs = [blk]
        out_shape = [jax.ShapeDtypeStruct((t, d), F32)]
    else:
        out_specs = [blk, blk]
        out_shape = [jax.ShapeDtypeStruct((t, d), F32), jax.ShapeDtypeStruct((t, d), BF16)]
    return pl.pallas_call(
        functools.partial(_ple_kernel, n_y=len(ys), gated=gated, final=final),
        grid=(t // tm,),
        in_specs=in_specs,
        out_specs=out_specs,
        out_shape=out_shape,
        compiler_params=_cparams(("parallel",)),
        name="ple",
    )(*args)


def _tiles(t, seq, ff):
    return dict(
        tm_norm=min(512, t),
        tm_in=min(1024, t), tn_in=512,
        tq=min(256, seq), tk=min(512, seq), heads_per_step=4,
        ts=min(512, seq),
        tm_out=min(512, t),
        tm_ffn=min(1024, t), tf_ffn=min(256, ff),
        tm_moe=min(512, t), tf_moe=min(1024, ff),
        tm_ple=min(512, t),
    )


def _rotary_tables(positions):
    inv_freq = ROPE_THETA ** (-jnp.arange(0, ROPE_DIM, 2, dtype=F32) / ROPE_DIM)
    ang = positions.astype(F32).reshape(-1, 1) * inv_freq
    cos, sin = lax.optimization_barrier((jnp.cos(ang), jnp.sin(ang)))
    t = ang.shape[0]
    ones = jnp.ones((t, HEAD_DIM - ROPE_DIM), F32)
    zeros = jnp.zeros((t, HEAD_DIM - ROPE_DIM), F32)
    c = jnp.concatenate([cos, cos, ones], axis=1)
    s = jnp.concatenate([-sin, sin, zeros], axis=1)
    rep = LANES // HEAD_DIM
    c, s = (jnp.concatenate([a] * rep, axis=1) for a in (c, s))
    lane = jnp.arange(2 * LANES, dtype=jnp.int32)
    dim = lane % HEAD_DIM
    src = jnp.where(dim < ROPE_HALF, lane + ROPE_HALF, jnp.where(dim < ROPE_DIM, lane - ROPE_HALF, lane))
    swap = (lane[:, None] == src[None, :]).astype(BF16)
    return c, s, swap


def kernel(x, p, positions, mix_norm, w_in, lambda_q1, lambda_k1, lambda_q2, lambda_k2, subln_gain, w_pool, pool_scale, conv_dw, conv_dw_bias, conv_ln_gain, conv_ln_bias, w_conv_pw, w_out, ffn_norm, w_dense_gate, w_dense_up, w_dense_down, w_router, w_exp_gate, w_exp_up, w_exp_down, ple_norm, w_ple_gate, w_ple_proj, final_norm):
    batch, seq, d = x.shape
    depth = w_in.shape[0]
    t = batch * seq
    attn_w = (w_in.shape[2] - 2 * conv_dw.shape[2] - pool_scale.shape[1]) // 3
    n_heads = attn_w // V_DIM
    qk_cols = attn_w
    pool_col = 3 * attn_w
    ff = w_dense_gate.shape[2]
    ts = _tiles(t, seq, ff)

    rot_c, rot_s, rot_swap = _rotary_tables(positions)
    h = x.reshape(t, d)
    u = _rmsnorm(h, mix_norm[0], ts["tm_norm"])
    out = None
    experts_bf16 = None
    for i in range(depth):
        lam_init = 0.8 - 0.6 * math.exp(-0.3 * i)
        final = i == depth - 1
        z = _inproj(u, w_in[i].astype(BF16), rot_c, rot_s, rot_swap, qk_cols, ts["tm_in"], ts["tn_in"])
        lam_params = jnp.stack([lambda_q1[i], lambda_k1[i], lambda_q2[i], lambda_k2[i]]).astype(F32)
        o_attn = _attention(z, lam_params, subln_gain[i], batch, seq, n_heads, lam_init, ts["tq"],
                            ts["tk"], ts["heads_per_step"])
        o_local = _local_mixers(z, w_pool[i].astype(BF16), pool_scale[i], conv_dw[i], conv_dw_bias[i],
                                conv_ln_gain[i], conv_ln_bias[i], w_conv_pw[i].astype(BF16),
                                seq, pool_col, ts["ts"])
        h1, u2 = _outproj(o_attn, o_local, w_out[i].astype(BF16), h, ffn_norm[i], ts["tm_out"])
        j = i // 2
        if i % 2 == 0:
            casts = ()
            if i + 1 < depth:
                jn = (i + 1) // 2
                casts = ((w_exp_gate[jn], 2), (w_exp_up[jn], 2), (w_exp_down[jn], 1))
            y, experts_bf16 = _dense_ffn(u2, w_dense_gate[j].astype(BF16), w_dense_up[j].astype(BF16),
                                         w_dense_down[j].astype(BF16), ts["tm_ffn"], ts["tf_ffn"], casts)
            ys, route = [y], None
        else:
            if experts_bf16 is None:
                experts_bf16 = [w.astype(BF16) for w in (w_exp_gate[j], w_exp_up[j], w_exp_down[j])]
            ys, route = _moe(u2, h1, ffn_norm[i], w_router[j], *experts_bf16,
                             ts["tm_out"], ts["tm_moe"], ts["tf_moe"])
            experts_bf16 = None
        next_gain = final_norm if final else mix_norm[i + 1]
        res = _ple(h1, ys, route, ple_norm[i], w_ple_gate[i].astype(BF16), p[i].reshape(t, -1),
                   w_ple_proj[i].astype(BF16), next_gain, final, ts["tm_ple"])
        if final:
            out = res[0]
        else:
            h, u = res
    return out.reshape(batch, seq, d)
```

```python
import functools
import math

import jax
import jax.numpy as jnp
from jax import lax
from jax.experimental import pallas as pl
from jax.experimental.pallas import tpu as pltpu

F32 = jnp.float32
BF16 = jnp.bfloat16

EPS = 1e-6
SUBLN_EPS = 1e-5
LN_EPS = 1e-5
HEAD_DIM = 64
V_DIM = 2 * HEAD_DIM
ROPE_DIM = HEAD_DIM // 4
ROPE_HALF = ROPE_DIM // 2
ROPE_THETA = 500000.0
LOG2_E = math.log2(math.e)
POOL_WINDOWS = (2, 4, 8, 16)
POOL_GROUP_DIM = 128
CONV_KERNEL = 31
N_EXPERTS = 8
TOP_K = 2

LANES = 128
V7X_VMEM_BYTES = 64 * 1024 * 1024
VMEM_LIMIT = 56 * 1024 * 1024
POOL_HALO = 16
CONV_HALO = 32


def _cparams(sem):
    return pltpu.CompilerParams(dimension_semantics=sem, vmem_limit_bytes=VMEM_LIMIT)


def _resident(shape, index_map):
    return pl.BlockSpec(shape, index_map, pipeline_mode=pl.Buffered(1))


def _rms(x, gain, eps):
    return x * lax.rsqrt(jnp.mean(x * x, axis=-1, keepdims=True) + eps) * gain


def _rmsnorm_kernel(x_ref, g_ref, o_ref):
    o_ref[...] = _rms(x_ref[...], g_ref[...], EPS).astype(o_ref.dtype)


def _rmsnorm(x, gain, tm):
    t, d = x.shape
    return pl.pallas_call(
        _rmsnorm_kernel,
        grid=(t // tm,),
        in_specs=[pl.BlockSpec((tm, d), lambda i: (i, 0)), pl.BlockSpec((1, d), lambda i: (0, 0))],
        out_specs=pl.BlockSpec((tm, d), lambda i: (i, 0)),
        out_shape=jax.ShapeDtypeStruct((t, d), BF16),
        compiler_params=_cparams(("parallel",)),
        name="rmsnorm",
    )(x, gain.reshape(1, d))


def _inproj_kernel(u_ref, w_ref, c_ref, s_ref, swap_ref, z_ref, *, n_rot_blocks, n_q_blocks, q_scale):
    j = pl.program_id(1)
    acc = jnp.dot(u_ref[...], w_ref[...], preferred_element_type=F32)
    tn = acc.shape[1]

    @pl.when(j < n_rot_blocks)
    def _():
        reps = tn // LANES
        c = jnp.concatenate([c_ref[...]] * reps, axis=1)
        s = jnp.concatenate([s_ref[...]] * reps, axis=1)
        pw = swap_ref.shape[0]
        ab = acc.astype(swap_ref.dtype)
        partner = jnp.concatenate(
            [jnp.dot(ab[:, b * pw:(b + 1) * pw], swap_ref[...], preferred_element_type=F32)
             for b in range(tn // pw)], axis=1)
        r = acc * c + partner * s
        scale = jnp.where(j < n_q_blocks, q_scale, 1.0).astype(F32)
        z_ref[...] = (r * scale).astype(z_ref.dtype)

    @pl.when(j >= n_rot_blocks)
    def _():
        z_ref[...] = acc.astype(z_ref.dtype)


def _inproj(u, w, rot_c, rot_s, swap, qk_cols, tm, tn):
    t, d = u.shape
    n = w.shape[1]
    kern = functools.partial(_inproj_kernel, n_rot_blocks=2 * qk_cols // tn, n_q_blocks=qk_cols // tn,
                             q_scale=HEAD_DIM ** -0.5 * LOG2_E)
    tab = pl.BlockSpec((tm, LANES), lambda i, j: (i, 0))
    return pl.pallas_call(
        kern,
        grid=(t // tm, n // tn),
        in_specs=[pl.BlockSpec((tm, d), lambda i, j: (i, 0)),
                  pl.BlockSpec((d, tn), lambda i, j: (0, j)),
                  tab, tab, _resident(swap.shape, lambda i, j: (0, 0))],
        out_specs=pl.BlockSpec((tm, tn), lambda i, j: (i, j)),
        out_shape=jax.ShapeDtypeStruct((t, n), BF16),
        compiler_params=_cparams(("parallel", "arbitrary")),
        name="inproj",
    )(u, w, rot_c, rot_s, swap)


def _attn_kernel(lam_ref, gain_ref, q_ref, k_ref, v_ref, o_ref, qs_ref, vx_ref, m_ref, acc_ref, sa_ref,
                 sb_ref, *, tq, tk, hp, lam_init):
    qi = pl.program_id(2)
    per = tk // tq

    @pl.when(qi == 0)
    def _():
        for a in range(hp):
            vx_ref[a, :, 0:V_DIM] = v_ref[:, a * V_DIM:(a + 1) * V_DIM]
            vx_ref[a, :, V_DIM:2 * V_DIM] = jnp.ones((v_ref.shape[0], V_DIM), v_ref.dtype)

    lane = lax.broadcasted_iota(jnp.int32, (tq, LANES), 1)
    for a in range(hp):
        q = q_ref[:, a * LANES:(a + 1) * LANES]
        zero = jnp.zeros_like(q)
        qs_ref[a, 0:tq, :] = jnp.where(lane < HEAD_DIM, q, zero)
        qs_ref[a, tq:2 * tq, :] = jnp.where(lane >= HEAD_DIM, q, zero)
    m_ref[...] = jnp.full(m_ref.shape, -jnp.inf, F32)
    acc_ref[...] = jnp.zeros(acc_ref.shape, F32)

    def scores(j, s_ref):
        start = pl.multiple_of(j * tk, tk)
        for a in range(hp):
            k = k_ref[pl.ds(start, tk), a * LANES:(a + 1) * LANES]
            s_ref[a] = lax.dot_general(qs_ref[a], k, (((1,), (1,)), ((), ())),
                                       preferred_element_type=F32)

    def update(j, s_ref, masked):
        start = pl.multiple_of(j * tk, tk)
        for a in range(hp):
            vx = vx_ref[a, pl.ds(start, tk), :]
            s = s_ref[a]
            if masked:
                row = lax.broadcasted_iota(jnp.int32, s.shape, 0)
                col = lax.broadcasted_iota(jnp.int32, s.shape, 1)
                s = jnp.where(col <= (row & (tq - 1)) + (qi % per) * tq, s, -jnp.inf)
            m_old = m_ref[a]
            m_new = jnp.maximum(m_old, jnp.max(s, axis=-1, keepdims=True))
            alpha = jnp.exp2(m_old - m_new)
            p = jnp.exp2(s - jnp.concatenate([m_new] * (tk // LANES), axis=1))
            pv = jnp.dot(p.astype(vx.dtype), vx, preferred_element_type=F32)
            acc_ref[a] = jnp.concatenate([alpha, alpha], axis=1) * acc_ref[a] + pv
            m_ref[a] = m_new

    n_full = qi // per
    scores(0, sa_ref)

    def body(j, carry):
        def even():
            scores(j + 1, sb_ref)
            update(j, sa_ref, False)

        def odd():
            scores(j + 1, sa_ref)
            update(j, sb_ref, False)

        lax.cond((j & 1) == 0, even, odd)
        return carry

    lax.fori_loop(0, n_full, body, 0)
    lax.cond((n_full & 1) == 0, lambda: update(n_full, sa_ref, True), lambda: update(n_full, sb_ref, True))

    lp = lam_ref[...]
    lam = (jnp.exp(jnp.sum(lp[0:1] * lp[1:2], axis=-1, keepdims=True))
           - jnp.exp(jnp.sum(lp[2:3] * lp[3:4], axis=-1, keepdims=True)) + lam_init)
    for a in range(hp):
        o_all = acc_ref[a, :, 0:V_DIM] / acc_ref[a, :, V_DIM:2 * V_DIM]
        o = o_all[0:tq] - lam * o_all[tq:2 * tq]
        y = _rms(o, gain_ref[...], SUBLN_EPS) * (1.0 - lam_init)
        o_ref[:, a * V_DIM:(a + 1) * V_DIM] = y.astype(o_ref.dtype)


def _attention(z, lam_params, subln_gain, batch, seq, n_heads, lam_init, tq, tk, hp):
    t = z.shape[0]
    nq = seq // tq
    assert tq & (tq - 1) == 0 and tk % tq == 0 and seq % tk == 0 and n_heads % hp == 0
    kern = functools.partial(_attn_kernel, tq=tq, tk=tk, hp=hp, lam_init=lam_init)
    ng = n_heads // hp
    w = hp * LANES
    return pl.pallas_call(
        kern,
        grid=(batch, ng, nq),
        in_specs=[pl.BlockSpec(lam_params.shape, lambda b, h, i: (0, 0)),
                  pl.BlockSpec((1, V_DIM), lambda b, h, i: (0, 0)),
                  pl.BlockSpec((tq, w), lambda b, h, i: (b * nq + i, h)),
                  pl.BlockSpec((seq, w), lambda b, h, i: (b, ng + h)),
                  pl.BlockSpec((seq, w), lambda b, h, i: (b, 2 * ng + h))],
        out_specs=pl.BlockSpec((tq, w), lambda b, h, i: (b * nq + i, h)),
        out_shape=jax.ShapeDtypeStruct((t, n_heads * V_DIM), BF16),
        scratch_shapes=[pltpu.VMEM((hp, 2 * tq, LANES), BF16),
                        pltpu.VMEM((hp, seq, 2 * V_DIM), BF16),
                        pltpu.VMEM((hp, 2 * tq, LANES), F32),
                        pltpu.VMEM((hp, 2 * tq, 2 * V_DIM), F32),
                        pltpu.VMEM((hp, 2 * tq, tk), F32),
                        pltpu.VMEM((hp, 2 * tq, tk), F32)],
        compiler_params=_cparams(("parallel", "parallel", "arbitrary")),
        name="diff_attention",
    )(lam_params, subln_gain.reshape(1, V_DIM), z, z, z)


def _local_kernel(zp_ref, zph_ref, za_ref, zah_ref, zb_ref, zbh_ref, wpool_ref, pscale_ref,
                  dw_ref, dwb_ref, lng_ref, lnb_ref, wpw_ref, o_ref, pbuf_ref, cbuf_ref, sbuf_ref,
                  *, ts, tiles_per_seq):
    i = pl.program_id(0)
    tile_in_seq = i % tiles_per_seq
    first = tile_in_seq == 0
    pool_w = zp_ref.shape[1]

    zp = zp_ref[...].astype(F32)
    halo = zph_ref[...].astype(F32)
    pbuf_ref[0:POOL_HALO, :] = jnp.where(first, jnp.zeros_like(halo), halo)
    pbuf_ref[POOL_HALO:POOL_HALO + ts, :] = zp
    pos = tile_in_seq * ts + lax.broadcasted_iota(jnp.int32, (ts, 1), 0)
    for g, w in enumerate(POOL_WINDOWS):
        cols = slice(g * POOL_GROUP_DIM, (g + 1) * POOL_GROUP_DIM)
        zg = zp[:, cols]
        wsum = zg
        for k in range(1, w):
            wsum = wsum + pbuf_ref[POOL_HALO - k:POOL_HALO - k + ts, cols]
        count = jnp.minimum(pos + 1, w).astype(F32)
        d = (wsum / count - zg).astype(BF16)
        y = jnp.dot(d, wpool_ref[g], preferred_element_type=F32) * pscale_ref[:, cols]
        o_ref[:, cols] = y.astype(o_ref.dtype)

    c = za_ref[...].astype(F32) * jax.nn.sigmoid(zb_ref[...].astype(F32))
    ch = zah_ref[...].astype(F32) * jax.nn.sigmoid(zbh_ref[...].astype(F32))
    cbuf_ref[0:CONV_HALO, :] = jnp.where(first, jnp.zeros_like(ch), ch)
    cbuf_ref[CONV_HALO:CONV_HALO + ts, :] = c
    acc = jnp.zeros_like(c) + dwb_ref[...]
    base = CONV_HALO - (CONV_KERNEL - 1)
    sub = 8
    for b in range(sub):
        offs = [o for o in range(base, base + CONV_KERNEL) if o % sub == b]
        if not offs:
            continue
        span = offs[-1] - offs[0] + ts
        sbuf_ref[0:span, :] = cbuf_ref[offs[0]:offs[0] + span, :]
        for o in offs:
            k = o - base
            acc = acc + dw_ref[k:k + 1, :] * sbuf_ref[o - offs[0]:o - offs[0] + ts, :]
    mu = jnp.mean(acc, axis=-1, keepdims=True)
    xc = acc - mu
    yn = xc * lax.rsqrt(jnp.mean(xc * xc, axis=-1, keepdims=True) + LN_EPS) * lng_ref[...] + lnb_ref[...]
    sw = yn * jax.nn.sigmoid(yn)
    out = jnp.dot(sw.astype(BF16), wpw_ref[...], preferred_element_type=F32)
    o_ref[:, pool_w:] = out.astype(o_ref.dtype)


def _local_mixers(z, w_pool, pool_scale, dw, dw_b, ln_g, ln_b, w_pw, seq, pool_col, ts):
    t = z.shape[0]
    pool_w = pool_scale.shape[0]
    conv_w = dw.shape[1]
    assert pool_w == conv_w and pool_col % pool_w == 0
    pc = pool_col // pool_w
    hp = ts // POOL_HALO
    hc = ts // CONV_HALO
    kern = functools.partial(_local_kernel, ts=ts, tiles_per_seq=seq // ts)
    row = lambda shape: _resident(shape, lambda i: (0,) * len(shape))
    return pl.pallas_call(
        kern,
        grid=(t // ts,),
        in_specs=[pl.BlockSpec((ts, pool_w), lambda i: (i, pc)),
                  pl.BlockSpec((POOL_HALO, pool_w), lambda i: (jnp.maximum(i * hp - 1, 0), pc)),
                  pl.BlockSpec((ts, conv_w), lambda i: (i, pc + 1)),
                  pl.BlockSpec((CONV_HALO, conv_w), lambda i: (jnp.maximum(i * hc - 1, 0), pc + 1)),
                  pl.BlockSpec((ts, conv_w), lambda i: (i, pc + 2)),
                  pl.BlockSpec((CONV_HALO, conv_w), lambda i: (jnp.maximum(i * hc - 1, 0), pc + 2)),
                  row(w_pool.shape), row((1, pool_w)), row(dw.shape), row((1, conv_w)),
                  row((1, conv_w)), row((1, conv_w)), row(w_pw.shape)],
        out_specs=pl.BlockSpec((ts, pool_w + conv_w), lambda i: (i, 0)),
        out_shape=jax.ShapeDtypeStruct((t, pool_w + conv_w), BF16),
        scratch_shapes=[pltpu.VMEM((POOL_HALO + ts, pool_w), F32),
                        pltpu.VMEM((CONV_HALO + ts, conv_w), F32),
                        pltpu.VMEM((CONV_HALO + ts, conv_w), F32)],
        compiler_params=_cparams(("parallel",)),
        name="local_mixers",
    )(z, z, z, z, z, z, w_pool, pool_scale.reshape(1, pool_w), dw, dw_b.reshape(1, conv_w),
      ln_g.reshape(1, conv_w), ln_b.reshape(1, conv_w), w_pw)


def _outproj_kernel(oa_ref, ol_ref, w_ref, h_ref, g_ref, h1_ref, u_ref):
    ka = oa_ref.shape[1]
    mix = jnp.dot(oa_ref[...], w_ref[0:ka, :], preferred_element_type=F32)
    mix = mix + jnp.dot(ol_ref[...], w_ref[ka:, :], preferred_element_type=F32)
    h1 = h_ref[...] + mix
    h1_ref[...] = h1
    u_ref[...] = _rms(h1, g_ref[...], EPS).astype(u_ref.dtype)


def _outproj(o_attn, o_local, w_out, h, gain, tm):
    t, d = h.shape
    ka, kl = o_attn.shape[1], o_local.shape[1]
    return pl.pallas_call(
        _outproj_kernel,
        grid=(t // tm,),
        in_specs=[pl.BlockSpec((tm, ka), lambda i: (i, 0)),
                  pl.BlockSpec((tm, kl), lambda i: (i, 0)),
                  _resident((ka + kl, d), lambda i: (0, 0)),
                  pl.BlockSpec((tm, d), lambda i: (i, 0)),
                  _resident((1, d), lambda i: (0, 0))],
        out_specs=[pl.BlockSpec((tm, d), lambda i: (i, 0)), pl.BlockSpec((tm, d), lambda i: (i, 0))],
        out_shape=[jax.ShapeDtypeStruct((t, d), F32), jax.ShapeDtypeStruct((t, d), BF16)],
        compiler_params=_cparams(("parallel",)),
        name="outproj",
    )(o_attn, o_local, w_out, h, gain.reshape(1, d))


def _swiglu_partial(x, wg, wu, wd):
    g = jnp.dot(x, wg, preferred_element_type=F32)
    u = jnp.dot(x, wu, preferred_element_type=F32)
    hidden = (g * jax.nn.sigmoid(g)) * u
    return jnp.dot(hidden.astype(wd.dtype), wd, preferred_element_type=F32)


def _dense_ffn_kernel(*refs, n_cast):
    u_ref, wg_ref, wu_ref, wd_ref = refs[:4]
    cast_in = refs[4:4 + n_cast]
    y_ref = refs[4 + n_cast]
    cast_out = refs[5 + n_cast:5 + 2 * n_cast]
    acc_ref = refs[5 + 2 * n_cast]
    f = pl.program_id(1)

    @pl.when(f == 0)
    def _():
        acc_ref[...] = jnp.zeros(acc_ref.shape, F32)

    acc_ref[...] += _swiglu_partial(u_ref[...], wg_ref[...], wu_ref[...], wd_ref[...])

    for src, dst in zip(cast_in, cast_out):
        dst[...] = src[...].astype(dst.dtype)

    @pl.when(f == pl.num_programs(1) - 1)
    def _():
        y_ref[...] = acc_ref[...].astype(y_ref.dtype)


def _cast_slices(arrays_axes, n_steps, nf):
    specs = []
    for arr, axis in arrays_axes:
        e = arr.shape[0]
        if n_steps % e:
            return None
        per = n_steps // e
        if arr.shape[axis] % per:
            return None
        width = arr.shape[axis] // per
        align = LANES if axis == 2 else 16
        if width % align:
            return None
        block = tuple(None if a == 0 else (width if a == axis else arr.shape[a]) for a in range(3))

        def imap(i, f, per=per, axis=axis):
            s = i * nf + f
            return (s // per, s % per, 0) if axis == 1 else (s // per, 0, s % per)

        specs.append(pl.BlockSpec(block, imap))
    return specs


def _dense_ffn(u, wg, wu, wd, tm, tf, casts=()):
    t, d = u.shape
    ff = wg.shape[1]
    grid = (t // tm, ff // tf)
    cast_specs = _cast_slices(casts, grid[0] * grid[1], grid[1]) if casts else []
    if cast_specs is None:
        cast_specs, casts, unsupported = [], (), True
    else:
        unsupported = False
    n_cast = len(cast_specs)
    outs = pl.pallas_call(
        functools.partial(_dense_ffn_kernel, n_cast=n_cast),
        grid=grid,
        in_specs=[pl.BlockSpec((tm, d), lambda i, f: (i, 0)),
                  pl.BlockSpec((d, tf), lambda i, f: (0, f)),
                  pl.BlockSpec((d, tf), lambda i, f: (0, f)),
                  pl.BlockSpec((tf, d), lambda i, f: (f, 0))] + cast_specs,
        out_specs=[pl.BlockSpec((tm, d), lambda i, f: (i, 0))] + cast_specs,
        out_shape=[jax.ShapeDtypeStruct((t, d), BF16)]
        + [jax.ShapeDtypeStruct(a.shape, BF16) for a, _ in casts],
        scratch_shapes=[pltpu.VMEM((tm, d), F32)],
        compiler_params=_cparams(("parallel", "arbitrary")),
        name="dense_ffn",
    )(u, wg, wu, wd, *[a for a, _ in casts])
    return outs[0], (None if unsupported else list(outs[1:]))


def _router_kernel(h_ref, g_ref, w_ref, o_ref):
    u = _rms(h_ref[...], g_ref[...], EPS)
    logits = jnp.dot(u, w_ref[...], preferred_element_type=F32, precision=lax.Precision.HIGHEST)
    lane = lax.broadcasted_iota(jnp.int32, logits.shape, 1)
    neg = jnp.full(logits.shape, -jnp.inf, F32)
    lg = jnp.where(lane < N_EXPERTS, logits, neg)
    m1 = jnp.max(lg, axis=-1, keepdims=True)
    i1 = jnp.min(jnp.where(lg == m1, lane, LANES), axis=-1, keepdims=True)
    lg2 = jnp.where(lane == i1, neg, lg)
    m2 = jnp.max(lg2, axis=-1, keepdims=True)
    i2 = jnp.min(jnp.where(lg2 == m2, lane, LANES), axis=-1, keepdims=True)
    e2 = jnp.exp(m2 - m1)
    g1 = 1.0 / (1.0 + e2)
    g2 = e2 / (1.0 + e2)
    out = jnp.where(lane == 0, g1, jnp.where(lane == 1, g2, jnp.where(
        lane == 2, i1.astype(F32), jnp.where(lane == 3, i2.astype(F32), 0.0))))
    o_ref[...] = out


def _router(h1, gain, w_router, tm):
    t, d = h1.shape
    w = jnp.zeros((d, LANES), F32).at[:, :N_EXPERTS].set(w_router)
    return pl.pallas_call(
        _router_kernel,
        grid=(t // tm,),
        in_specs=[pl.BlockSpec((tm, d), lambda i: (i, 0)),
                  _resident((1, d), lambda i: (0, 0)),
                  _resident((d, LANES), lambda i: (0, 0))],
        out_specs=pl.BlockSpec((tm, LANES), lambda i: (i, 0)),
        out_shape=jax.ShapeDtypeStruct((t, LANES), F32),
        compiler_params=_cparams(("parallel",)),
        name="router",
    )(h1, gain.reshape(1, d), w)


def _moe_ffn_kernel(te_ref, nu_ref, x_ref, wg_ref, wu_ref, wd_ref, y_ref, acc_ref):
    i = pl.program_id(0)
    f = pl.program_id(1)

    @pl.when(i < nu_ref[0])
    def _():
        @pl.when(f == 0)
        def _():
            acc_ref[...] = jnp.zeros(acc_ref.shape, F32)

        acc_ref[...] += _swiglu_partial(x_ref[...], wg_ref[...], wu_ref[...], wd_ref[...])

        @pl.when(f == pl.num_programs(1) - 1)
        def _():
            y_ref[...] = acc_ref[...].astype(y_ref.dtype)

    @pl.when(i >= nu_ref[0])
    def _():
        y_ref[...] = jnp.zeros(y_ref.shape, y_ref.dtype)


def _moe_ffn(x_sorted, tile_expert, n_used, wg, wu, wd, tm, tf):
    p, d = x_sorted.shape
    ff = wg.shape[2]
    nf = ff // tf

    def row(i, f, te, nu):
        return (jnp.minimum(i, nu[0] - 1), 0)

    def fcol(i, f, nu):
        return jnp.where(i < nu[0], f, nf - 1)

    grid_spec = pltpu.PrefetchScalarGridSpec(
        num_scalar_prefetch=2,
        grid=(p // tm, nf),
        in_specs=[pl.BlockSpec((tm, d), row),
                  pl.BlockSpec((None, d, tf), lambda i, f, te, nu: (te[i], 0, fcol(i, f, nu))),
                  pl.BlockSpec((None, d, tf), lambda i, f, te, nu: (te[i], 0, fcol(i, f, nu))),
                  pl.BlockSpec((None, tf, d), lambda i, f, te, nu: (te[i], fcol(i, f, nu), 0))],
        out_specs=pl.BlockSpec((tm, d), lambda i, f, te, nu: (i, 0)),
        scratch_shapes=[pltpu.VMEM((tm, d), F32)],
    )
    return pl.pallas_call(
        _moe_ffn_kernel,
        grid_spec=grid_spec,
        out_shape=jax.ShapeDtypeStruct((p, d), BF16),
        compiler_params=_cparams(("arbitrary", "arbitrary")),
        name="moe_ffn",
    )(tile_expert, n_used, x_sorted, wg, wu, wd)


def _moe(u2, h1, ffn_gain, w_router, wg, wu, wd, tm_route, tm, tf):
    t, d = u2.shape
    route = _router(h1, ffn_gain, w_router, tm_route)
    expert = route[:, 2:2 + TOP_K].astype(jnp.int32).reshape(-1)
    onehot = (expert[:, None] == jnp.arange(N_EXPERTS, dtype=jnp.int32)[None, :]).astype(jnp.int32)
    csum = jnp.cumsum(onehot, axis=0)
    rank = jnp.sum(csum * onehot, axis=1) - 1
    counts = csum[-1]
    padded = ((counts + tm - 1) // tm) * tm
    pend = jnp.cumsum(padded)
    pstart = pend - padded
    dest = pstart[expert] + rank
    n_tiles = (TOP_K * t) // tm + N_EXPERTS
    p = n_tiles * tm
    token_of_row = jnp.zeros((p,), jnp.int32).at[dest].set(jnp.arange(TOP_K * t, dtype=jnp.int32) // TOP_K)
    tile_start = jnp.arange(n_tiles, dtype=jnp.int32) * tm
    tile_expert = jnp.minimum(jnp.sum((tile_start[:, None] >= pend[None, :]).astype(jnp.int32), axis=1),
                              N_EXPERTS - 1).astype(jnp.int32)
    n_used = (pend[-1] // tm).astype(jnp.int32).reshape(1)
    x_sorted = u2.at[token_of_row].get(mode="promise_in_bounds")
    y_sorted = _moe_ffn(x_sorted, tile_expert, n_used, wg, wu, wd, tm, tf)
    dest2 = dest.reshape(t, TOP_K)
    y0 = y_sorted.at[dest2[:, 0]].get(mode="promise_in_bounds")
    y1 = y_sorted.at[dest2[:, 1]].get(mode="promise_in_bounds")
    return [y0, y1], route


def _ple_kernel(*refs, n_y, gated, final):
    h_ref, y_refs = refs[0], refs[1:1 + n_y]
    rest = refs[1 + n_y:]
    if gated:
        r = rest[0][...]
        rest = rest[1:]
    gp_ref, wg_ref, p_ref, wp_ref, gn_ref = rest[:5]
    out_refs = rest[5:]
    h2 = h_ref[...]
    for k, y_ref in enumerate(y_refs):
        y = y_ref[...].astype(F32)
        h2 = h2 + (r[:, k:k + 1] * y if gated else y)
    u3 = _rms(h2, gp_ref[...], EPS).astype(BF16)
    gate = jax.nn.sigmoid(jnp.dot(u3, wg_ref[...], preferred_element_type=F32))
    emb = jnp.dot(p_ref[...].astype(BF16), wp_ref[...], preferred_element_type=F32)
    h3 = h2 + emb * gate
    if final:
        out_refs[0][...] = _rms(h3, gn_ref[...], EPS)
    else:
        out_refs[0][...] = h3
        out_refs[1][...] = _rms(h3, gn_ref[...], EPS).astype(out_refs[1].dtype)


def _ple(h1, ys, route, ple_gain, w_gate, p, w_proj, next_gain, final, tm):
    t, d = h1.shape
    pd = p.shape[1]
    blk = pl.BlockSpec((tm, d), lambda i: (i, 0))
    vec = _resident((1, d), lambda i: (0, 0))
    gated = route is not None
    in_specs = [blk] + [blk] * len(ys)
    args = [h1] + list(ys)
    if gated:
        in_specs.append(pl.BlockSpec((tm, LANES), lambda i: (i, 0)))
        args.append(route)
    in_specs += [vec, _resident((d, d), lambda i: (0, 0)), pl.BlockSpec((tm, pd), lambda i: (i, 0)),
                 _resident((pd, d), lambda i: (0, 0)), vec]
    args += [ple_gain.reshape(1, d), w_gate, p, w_proj, next_gain.reshape(1, d)]
    if final:
        out_specs = [blk]
        out_shape = [jax.ShapeDtypeStruct((t, d), F32)]
    else:
        out_specs = [blk, blk]
        out_shape = [jax.ShapeDtypeStruct((t, d), F32), jax.ShapeDtypeStruct((t, d), BF16)]
    return pl.pallas_call(
        functools.partial(_ple_kernel, n_y=len(ys), gated=gated, final=final),
        grid=(t // tm,),
        in_specs=in_specs,
        out_specs=out_specs,
        out_shape=out_shape,
        compiler_params=_cparams(("parallel",)),
        name="ple",
    )(*args)


def _tiles(t, seq, ff):
    return dict(
        tm_norm=min(512, t),
        tm_in=min(1024, t), tn_in=512,
        tq=min(256, seq), tk=min(512, seq), heads_per_step=4,
        ts=min(512, seq),
        tm_out=min(512, t),
        tm_ffn=min(1024, t), tf_ffn=min(256, ff),
        tm_moe=min(512, t), tf_moe=min(1024, ff),
        tm_ple=min(512, t),
    )


def _rotary_tables(positions):
    inv_freq = ROPE_THETA ** (-jnp.arange(0, ROPE_DIM, 2, dtype=F32) / ROPE_DIM)
    ang = positions.astype(F32).reshape(-1, 1) * inv_freq
    cos, sin = lax.optimization_barrier((jnp.cos(ang), jnp.sin(ang)))
    t = ang.shape[0]
    ones = jnp.ones((t, HEAD_DIM - ROPE_DIM), F32)
    zeros = jnp.zeros((t, HEAD_DIM - ROPE_DIM), F32)
    c = jnp.concatenate([cos, cos, ones], axis=1)
    s = jnp.concatenate([-sin, sin, zeros], axis=1)
    rep = LANES // HEAD_DIM
    c, s = (jnp.concatenate([a] * rep, axis=1) for a in (c, s))
    lane = jnp.arange(2 * LANES, dtype=jnp.int32)
    dim = lane % HEAD_DIM
    src = jnp.where(dim < ROPE_HALF, lane + ROPE_HALF, jnp.where(dim < ROPE_DIM, lane - ROPE_HALF, lane))
    swap = (lane[:, None] == src[None, :]).astype(BF16)
    return c, s, swap


def kernel(x, p, positions, mix_norm, w_in, lambda_q1, lambda_k1, lambda_q2, lambda_k2, subln_gain, w_pool, pool_scale, conv_dw, conv_dw_bias, conv_ln_gain, conv_ln_bias, w_conv_pw, w_out, ffn_norm, w_dense_gate, w_dense_up, w_dense_down, w_router, w_exp_gate, w_exp_up, w_exp_down, ple_norm, w_ple_gate, w_ple_proj, final_norm):
    batch, seq, d = x.shape
    depth = w_in.shape[0]
    t = batch * seq
    attn_w = (w_in.shape[2] - 2 * conv_dw.shape[2] - pool_scale.shape[1]) // 3
    n_heads = attn_w // V_DIM
    qk_cols = attn_w
    pool_col = 3 * attn_w
    ff = w_dense_gate.shape[2]
    ts = _tiles(t, seq, ff)

    rot_c, rot_s, rot_swap = _rotary_tables(positions)
    h = x.reshape(t, d)
    u = _rmsnorm(h, mix_norm[0], ts["tm_norm"])
    out = None
    experts_bf16 = None
    for i in range(depth):
        lam_init = 0.8 - 0.6 * math.exp(-0.3 * i)
        final = i == depth - 1
        z = _inproj(u, w_in[i].astype(BF16), rot_c, rot_s, rot_swap, qk_cols, ts["tm_in"], ts["tn_in"])
        lam_params = jnp.stack([lambda_q1[i], lambda_k1[i], lambda_q2[i], lambda_k2[i]]).astype(F32)
        o_attn = _attention(z, lam_params, subln_gain[i], batch, seq, n_heads, lam_init, ts["tq"],
                            ts["tk"], ts["heads_per_step"])
        o_local = _local_mixers(z, w_pool[i].astype(BF16), pool_scale[i], conv_dw[i], conv_dw_bias[i],
                                conv_ln_gain[i], conv_ln_bias[i], w_conv_pw[i].astype(BF16),
                                seq, pool_col, ts["ts"])
        h1, u2 = _outproj(o_attn, o_local, w_out[i].astype(BF16), h, ffn_norm[i], ts["tm_out"])
        j = i // 2
        if i % 2 == 0:
            casts = ()
            if i + 1 < depth:
                jn = (i + 1) // 2
                casts = ((w_exp_gate[jn], 2), (w_exp_up[jn], 2), (w_exp_down[jn], 1))
            y, experts_bf16 = _dense_ffn(u2, w_dense_gate[j].astype(BF16), w_dense_up[j].astype(BF16),
                                         w_dense_down[j].astype(BF16), ts["tm_ffn"], ts["tf_ffn"], casts)
            ys, route = [y], None
        else:
            if experts_bf16 is None:
                experts_bf16 = [w.astype(BF16) for w in (w_exp_gate[j], w_exp_up[j], w_exp_down[j])]
            ys, route = _moe(u2, h1, ffn_norm[i], w_router[j], *experts_bf16,
                             ts["tm_out"], ts["tm_moe"], ts["tf_moe"])
            experts_bf16 = None
        next_gain = final_norm if final else mix_norm[i + 1]
        res = _ple(h1, ys, route, ple_norm[i], w_ple_gate[i].astype(BF16), p[i].reshape(t, -1),
                   w_ple_proj[i].astype(BF16), next_gain, final, ts["tm_ple"])
        if final:
            out = res[0]
        else:
            h, u = res
    return out.reshape(batch, seq, d)
```

```python
import functools
import math

import jax
import jax.numpy as jnp
from jax import lax
from jax.experimental import pallas as pl
from jax.experimental.pallas import tpu as pltpu

F32 = jnp.float32
BF16 = jnp.bfloat16

EPS = 1e-6
SUBLN_EPS = 1e-5
LN_EPS = 1e-5
HEAD_DIM = 64
V_DIM = 2 * HEAD_DIM
ROPE_DIM = HEAD_DIM // 4
ROPE_HALF = ROPE_DIM // 2
ROPE_THETA = 500000.0
LOG2_E = math.log2(math.e)
POOL_WINDOWS = (2, 4, 8, 16)
POOL_GROUP_DIM = 128
CONV_KERNEL = 31
N_EXPERTS = 8
TOP_K = 2

LANES = 128
V7X_VMEM_BYTES = 64 * 1024 * 1024
VMEM_LIMIT = 56 * 1024 * 1024
POOL_HALO = 16
CONV_HALO = 32


def _cparams(sem):
    return pltpu.CompilerParams(dimension_semantics=sem, vmem_limit_bytes=VMEM_LIMIT)


def _resident(shape, index_map):
    return pl.BlockSpec(shape, index_map, pipeline_mode=pl.Buffered(1))


def _rms(x, gain, eps):
    return x * lax.rsqrt(jnp.mean(x * x, axis=-1, keepdims=True) + eps) * gain


def _rmsnorm_kernel(x_ref, g_ref, o_ref):
    o_ref[...] = _rms(x_ref[...], g_ref[...], EPS).astype(o_ref.dtype)


def _rmsnorm(x, gain, tm):
    t, d = x.shape
    return pl.pallas_call(
        _rmsnorm_kernel,
        grid=(t // tm,),
        in_specs=[pl.BlockSpec((tm, d), lambda i: (i, 0)), pl.BlockSpec((1, d), lambda i: (0, 0))],
        out_specs=pl.BlockSpec((tm, d), lambda i: (i, 0)),
        out_shape=jax.ShapeDtypeStruct((t, d), BF16),
        compiler_params=_cparams(("parallel",)),
        name="rmsnorm",
    )(x, gain.reshape(1, d))


def _inproj_kernel(u_ref, w_ref, c_ref, s_ref, swap_ref, z_ref, *, n_rot_blocks, n_q_blocks, q_scale):
    j = pl.program_id(1)
    acc = jnp.dot(u_ref[...], w_ref[...], preferred_element_type=F32)
    tn = acc.shape[1]

    @pl.when(j < n_rot_blocks)
    def _():
        reps = tn // LANES
        c = jnp.concatenate([c_ref[...]] * reps, axis=1)
        s = jnp.concatenate([s_ref[...]] * reps, axis=1)
        pw = swap_ref.shape[0]
        ab = acc.astype(swap_ref.dtype)
        partner = jnp.concatenate(
            [jnp.dot(ab[:, b * pw:(b + 1) * pw], swap_ref[...], preferred_element_type=F32)
             for b in range(tn // pw)], axis=1)
        r = acc * c + partner * s
        scale = jnp.where(j < n_q_blocks, q_scale, 1.0).astype(F32)
        z_ref[...] = (r * scale).astype(z_ref.dtype)

    @pl.when(j >= n_rot_blocks)
    def _():
        z_ref[...] = acc.astype(z_ref.dtype)


def _inproj(u, w, rot_c, rot_s, swap, qk_cols, tm, tn):
    t, d = u.shape
    n = w.shape[1]
    kern = functools.partial(_inproj_kernel, n_rot_blocks=2 * qk_cols // tn, n_q_blocks=qk_cols // tn,
                             q_scale=HEAD_DIM ** -0.5 * LOG2_E)
    tab = pl.BlockSpec((tm, LANES), lambda i, j: (i, 0))
    return pl.pallas_call(
        kern,
        grid=(t // tm, n // tn),
        in_specs=[pl.BlockSpec((tm, d), lambda i, j: (i, 0)),
                  pl.BlockSpec((d, tn), lambda i, j: (0, j)),
                  tab, tab, _resident(swap.shape, lambda i, j: (0, 0))],
        out_specs=pl.BlockSpec((tm, tn), lambda i, j: (i, j)),
        out_shape=jax.ShapeDtypeStruct((t, n), BF16),
        compiler_params=_cparams(("parallel", "arbitrary")),
        name="inproj",
    )(u, w, rot_c, rot_s, swap)


def _attn_kernel(lam_ref, gain_ref, q_ref, k_ref, v_ref, o_ref, qs_ref, vx_ref, m_ref, acc_ref, sa_ref,
                 sb_ref, *, tq, tk, hp, lam_init):
    qi = pl.program_id(2)
    per = tk // tq

    @pl.when(qi == 0)
    def _():
        for a in range(hp):
            vx_ref[a, :, 0:V_DIM] = v_ref[:, a * V_DIM:(a + 1) * V_DIM]
            vx_ref[a, :, V_DIM:2 * V_DIM] = jnp.ones((v_ref.shape[0], V_DIM), v_ref.dtype)

    lane = lax.broadcasted_iota(jnp.int32, (tq, LANES), 1)
    for a in range(hp):
        q = q_ref[:, a * LANES:(a + 1) * LANES]
        zero = jnp.zeros_like(q)
        qs_ref[a, 0:tq, :] = jnp.where(lane < HEAD_DIM, q, zero)
        qs_ref[a, tq:2 * tq, :] = jnp.where(lane >= HEAD_DIM, q, zero)
    m_ref[...] = jnp.full(m_ref.shape, -jnp.inf, F32)
    acc_ref[...] = jnp.zeros(acc_ref.shape, F32)

    def scores(j, s_ref):
        start = pl.multiple_of(j * tk, tk)
        for a in range(hp):
            k = k_ref[pl.ds(start, tk), a * LANES:(a + 1) * LANES]
            s_ref[a] = lax.dot_general(qs_ref[a], k, (((1,), (1,)), ((), ())),
                                       preferred_element_type=F32)

    def update(j, s_ref, masked):
        start = pl.multiple_of(j * tk, tk)
        for a in range(hp):
            vx = vx_ref[a, pl.ds(start, tk), :]
            s = s_ref[a]
            if masked:
                row = lax.broadcasted_iota(jnp.int32, s.shape, 0)
                col = lax.broadcasted_iota(jnp.int32, s.shape, 1)
                s = jnp.where(col <= (row & (tq - 1)) + (qi % per) * tq, s, -jnp.inf)
            m_old = m_ref[a]
            m_new = jnp.maximum(m_old, jnp.max(s, axis=-1, keepdims=True))
            alpha = jnp.exp2(m_old - m_new)
            p = jnp.exp2(s - jnp.concatenate([m_new] * (tk // LANES), axis=1))
            pv = jnp.dot(p.astype(vx.dtype), vx, preferred_element_type=F32)
            acc_ref[a] = jnp.concatenate([alpha, alpha], axis=1) * acc_ref[a] + pv
            m_ref[a] = m_new

    n_full = qi // per
    scores(0, sa_ref)

    def body(j, carry):
        def even():
            scores(j + 1, sb_ref)
            update(j, sa_ref, False)

        def odd():
            scores(j + 1, sa_ref)
            update(j, sb_ref, False)

        lax.cond((j & 1) == 0, even, odd)
        return carry

    lax.fori_loop(0, n_full, body, 0)
    lax.cond((n_full & 1) == 0, lambda: update(n_full, sa_ref, True), lambda: update(n_full, sb_ref, True))

    lp = lam_ref[...]
    lam = (jnp.exp(jnp.sum(lp[0:1] * lp[1:2], axis=-1, keepdims=True))
           - jnp.exp(jnp.sum(lp[2:3] * lp[3:4], axis=-1, keepdims=True)) + lam_init)
    for a in range(hp):
        o_all = acc_ref[a, :, 0:V_DIM] / acc_ref[a, :, V_DIM:2 * V_DIM]
        o = o_all[0:tq] - lam * o_all[tq:2 * tq]
        y = _rms(o, gain_ref[...], SUBLN_EPS) * (1.0 - lam_init)
        o_ref[:, a * V_DIM:(a + 1) * V_DIM] = y.astype(o_ref.dtype)


def _attention(z, lam_params, subln_gain, batch, seq, n_heads, lam_init, tq, tk, hp):
    t = z.shape[0]
    nq = seq // tq
    assert tq & (tq - 1) == 0 and tk % tq == 0 and seq % tk == 0 and n_heads % hp == 0
    kern = functools.partial(_attn_kernel, tq=tq, tk=tk, hp=hp, lam_init=lam_init)
    ng = n_heads // hp
    w = hp * LANES
    return pl.pallas_call(
        kern,
        grid=(batch, ng, nq),
        in_specs=[pl.BlockSpec(lam_params.shape, lambda b, h, i: (0, 0)),
                  pl.BlockSpec((1, V_DIM), lambda b, h, i: (0, 0)),
                  pl.BlockSpec((tq, w), lambda b, h, i: (b * nq + i, h)),
                  pl.BlockSpec((seq, w), lambda b, h, i: (b, ng + h)),
                  pl.BlockSpec((seq, w), lambda b, h, i: (b, 2 * ng + h))],
        out_specs=pl.BlockSpec((tq, w), lambda b, h, i: (b * nq + i, h)),
        out_shape=jax.ShapeDtypeStruct((t, n_heads * V_DIM), BF16),
        scratch_shapes=[pltpu.VMEM((hp, 2 * tq, LANES), BF16),
                        pltpu.VMEM((hp, seq, 2 * V_DIM), BF16),
                        pltpu.VMEM((hp, 2 * tq, LANES), F32),
                        pltpu.VMEM((hp, 2 * tq, 2 * V_DIM), F32),
                        pltpu.VMEM((hp, 2 * tq, tk), F32),
                        pltpu.VMEM((hp, 2 * tq, tk), F32)],
        compiler_params=_cparams(("parallel", "parallel", "arbitrary")),
        name="diff_attention",
    )(lam_params, subln_gain.reshape(1, V_DIM), z, z, z)


def _local_kernel(zp_ref, zph_ref, za_ref, zah_ref, zb_ref, zbh_ref, wpool_ref, pscale_ref,
                  dw_ref, dwb_ref, lng_ref, lnb_ref, wpw_ref, o_ref, pbuf_ref, cbuf_ref, sbuf_ref,
                  *, ts, tiles_per_seq):
    i = pl.program_id(0)
    tile_in_seq = i % tiles_per_seq
    first = tile_in_seq == 0
    pool_w = zp_ref.shape[1]

    zp = zp_ref[...].astype(F32)
    halo = zph_ref[...].astype(F32)
    pbuf_ref[0:POOL_HALO, :] = jnp.where(first, jnp.zeros_like(halo), halo)
    pbuf_ref[POOL_HALO:POOL_HALO + ts, :] = zp
    pos = tile_in_seq * ts + lax.broadcasted_iota(jnp.int32, (ts, 1), 0)
    for g, w in enumerate(POOL_WINDOWS):
        cols = slice(g * POOL_GROUP_DIM, (g + 1) * POOL_GROUP_DIM)
        zg = zp[:, cols]
        wsum = zg
        for k in range(1, w):
            wsum = wsum + pbuf_ref[POOL_HALO - k:POOL_HALO - k + ts, cols]
        count = jnp.minimum(pos + 1, w).astype(F32)
        d = (wsum / count - zg).astype(BF16)
        y = jnp.dot(d, wpool_ref[g], preferred_element_type=F32) * pscale_ref[:, cols]
        o_ref[:, cols] = y.astype(o_ref.dtype)

    c = za_ref[...].astype(F32) * jax.nn.sigmoid(zb_ref[...].astype(F32))
    ch = zah_ref[...].astype(F32) * jax.nn.sigmoid(zbh_ref[...].astype(F32))
    cbuf_ref[0:CONV_HALO, :] = jnp.where(first, jnp.zeros_like(ch), ch)
    cbuf_ref[CONV_HALO:CONV_HALO + ts, :] = c
    acc = jnp.zeros_like(c) + dwb_ref[...]
    base = CONV_HALO - (CONV_KERNEL - 1)
    sub = 8
    for b in range(sub):
        offs = [o for o in range(base, base + CONV_KERNEL) if o % sub == b]
        if not offs:
            continue
        span = offs[-1] - offs[0] + ts
        sbuf_ref[0:span, :] = cbuf_ref[offs[0]:offs[0] + span, :]
        for o in offs:
            k = o - base
            acc = acc + dw_ref[k:k + 1, :] * sbuf_ref[o - offs[0]:o - offs[0] + ts, :]
    mu = jnp.mean(acc, axis=-1, keepdims=True)
    xc = acc - mu
    yn = xc * lax.rsqrt(jnp.mean(xc * xc, axis=-1, keepdims=True) + LN_EPS) * lng_ref[...] + lnb_ref[...]
    sw = yn * jax.nn.sigmoid(yn)
    out = jnp.dot(sw.astype(BF16), wpw_ref[...], preferred_element_type=F32)
    o_ref[:, pool_w:] = out.astype(o_ref.dtype)


def _local_mixers(z, w_pool, pool_scale, dw, dw_b, ln_g, ln_b, w_pw, seq, pool_col, ts):
    t = z.shape[0]
    pool_w = pool_scale.shape[0]
    conv_w = dw.shape[1]
    assert pool_w == conv_w and pool_col % pool_w == 0
    pc = pool_col // pool_w
    hp = ts // POOL_HALO
    hc = ts // CONV_HALO
    kern = functools.partial(_local_kernel, ts=ts, tiles_per_seq=seq // ts)
    row = lambda shape: _resident(shape, lambda i: (0,) * len(shape))
    return pl.pallas_call(
        kern,
        grid=(t // ts,),
        in_specs=[pl.BlockSpec((ts, pool_w), lambda i: (i, pc)),
                  pl.BlockSpec((POOL_HALO, pool_w), lambda i: (jnp.maximum(i * hp - 1, 0), pc)),
                  pl.BlockSpec((ts, conv_w), lambda i: (i, pc + 1)),
                  pl.BlockSpec((CONV_HALO, conv_w), lambda i: (jnp.maximum(i * hc - 1, 0), pc + 1)),
                  pl.BlockSpec((ts, conv_w), lambda i: (i, pc + 2)),
                  pl.BlockSpec((CONV_HALO, conv_w), lambda i: (jnp.maximum(i * hc - 1, 0), pc + 2)),
                  row(w_pool.shape), row((1, pool_w)), row(dw.shape), row((1, conv_w)),
                  row((1, conv_w)), row((1, conv_w)), row(w_pw.shape)],
        out_specs=pl.BlockSpec((ts, pool_w + conv_w), lambda i: (i, 0)),
        out_shape=jax.ShapeDtypeStruct((t, pool_w + conv_w), BF16),
        scratch_shapes=[pltpu.VMEM((POOL_HALO + ts, pool_w), F32),
                        pltpu.VMEM((CONV_HALO + ts, conv_w), F32),
                        pltpu.VMEM((CONV_HALO + ts, conv_w), F32)],
        compiler_params=_cparams(("parallel",)),
        name="local_mixers",
    )(z, z, z, z, z, z, w_pool, pool_scale.reshape(1, pool_w), dw, dw_b.reshape(1, conv_w),
      ln_g.reshape(1, conv_w), ln_b.reshape(1, conv_w), w_pw)


def _outproj_kernel(oa_ref, ol_ref, w_ref, h_ref, g_ref, h1_ref, u_ref):
    ka = oa_ref.shape[1]
    mix = jnp.dot(oa_ref[...], w_ref[0:ka, :], preferred_element_type=F32)
    mix = mix + jnp.dot(ol_ref[...], w_ref[ka:, :], preferred_element_type=F32)
    h1 = h_ref[...] + mix
    h1_ref[...] = h1
    u_ref[...] = _rms(h1, g_ref[...], EPS).astype(u_ref.dtype)


def _outproj(o_attn, o_local, w_out, h, gain, tm, u_dtype):
    t, d = h.shape
    ka, kl = o_attn.shape[1], o_local.shape[1]
    return pl.pallas_call(
        _outproj_kernel,
        grid=(t // tm,),
        in_specs=[pl.BlockSpec((tm, ka), lambda i: (i, 0)),
                  pl.BlockSpec((tm, kl), lambda i: (i, 0)),
                  _resident((ka + kl, d), lambda i: (0, 0)),
                  pl.BlockSpec((tm, d), lambda i: (i, 0)),
                  _resident((1, d), lambda i: (0, 0))],
        out_specs=[pl.BlockSpec((tm, d), lambda i: (i, 0)), pl.BlockSpec((tm, d), lambda i: (i, 0))],
        out_shape=[jax.ShapeDtypeStruct((t, d), F32), jax.ShapeDtypeStruct((t, d), u_dtype)],
        compiler_params=_cparams(("parallel",)),
        name="outproj",
    )(o_attn, o_local, w_out, h, gain.reshape(1, d))


def _swiglu_partial(x, wg, wu, wd):
    g = jnp.dot(x, wg, preferred_element_type=F32)
    u = jnp.dot(x, wu, preferred_element_type=F32)
    hidden = (g * jax.nn.sigmoid(g)) * u
    return jnp.dot(hidden.astype(wd.dtype), wd, preferred_element_type=F32)


def _dense_ffn_kernel(*refs, n_cast):
    u_ref, wg_ref, wu_ref, wd_ref = refs[:4]
    cast_in = refs[4:4 + n_cast]
    y_ref = refs[4 + n_cast]
    cast_out = refs[5 + n_cast:5 + 2 * n_cast]
    acc_ref = refs[5 + 2 * n_cast]
    f = pl.program_id(1)

    @pl.when(f == 0)
    def _():
        acc_ref[...] = jnp.zeros(acc_ref.shape, F32)

    acc_ref[...] += _swiglu_partial(u_ref[...], wg_ref[...], wu_ref[...], wd_ref[...])

    for src, dst in zip(cast_in, cast_out):
        dst[...] = src[...].astype(dst.dtype)

    @pl.when(f == pl.num_programs(1) - 1)
    def _():
        y_ref[...] = acc_ref[...].astype(y_ref.dtype)


def _cast_slices(arrays_axes, n_steps, nf):
    specs = []
    for arr, axis in arrays_axes:
        e = arr.shape[0]
        if n_steps % e:
            return None
        per = n_steps // e
        if arr.shape[axis] % per:
            return None
        width = arr.shape[axis] // per
        align = LANES if axis == 2 else 16
        if width % align:
            return None
        block = tuple(None if a == 0 else (width if a == axis else arr.shape[a]) for a in range(3))

        def imap(i, f, per=per, axis=axis):
            s = i * nf + f
            return (s // per, s % per, 0) if axis == 1 else (s // per, 0, s % per)

        specs.append(pl.BlockSpec(block, imap))
    return specs


def _dense_ffn(u, wg, wu, wd, tm, tf, casts=()):
    t, d = u.shape
    ff = wg.shape[1]
    grid = (t // tm, ff // tf)
    cast_specs = _cast_slices(casts, grid[0] * grid[1], grid[1]) if casts else []
    if cast_specs is None:
        cast_specs, casts, unsupported = [], (), True
    else:
        unsupported = False
    n_cast = len(cast_specs)
    outs = pl.pallas_call(
        functools.partial(_dense_ffn_kernel, n_cast=n_cast),
        grid=grid,
        in_specs=[pl.BlockSpec((tm, d), lambda i, f: (i, 0)),
                  pl.BlockSpec((d, tf), lambda i, f: (0, f)),
                  pl.BlockSpec((d, tf), lambda i, f: (0, f)),
                  pl.BlockSpec((tf, d), lambda i, f: (f, 0))] + cast_specs,
        out_specs=[pl.BlockSpec((tm, d), lambda i, f: (i, 0))] + cast_specs,
        out_shape=[jax.ShapeDtypeStruct((t, d), BF16)]
        + [jax.ShapeDtypeStruct(a.shape, BF16) for a, _ in casts],
        scratch_shapes=[pltpu.VMEM((tm, d), F32)],
        compiler_params=_cparams(("parallel", "arbitrary")),
        name="dense_ffn",
    )(u, wg, wu, wd, *[a for a, _ in casts])
    return outs[0], (None if unsupported else list(outs[1:]))


def _router_kernel(u_ref, w_ref, o_ref):
    logits = jnp.dot(u_ref[...], w_ref[...], preferred_element_type=F32, precision=lax.Precision.HIGHEST)
    lane = lax.broadcasted_iota(jnp.int32, logits.shape, 1)
    neg = jnp.full(logits.shape, -jnp.inf, F32)
    lg = jnp.where(lane < N_EXPERTS, logits, neg)
    m1 = jnp.max(lg, axis=-1, keepdims=True)
    i1 = jnp.min(jnp.where(lg == m1, lane, LANES), axis=-1, keepdims=True)
    lg2 = jnp.where(lane == i1, neg, lg)
    m2 = jnp.max(lg2, axis=-1, keepdims=True)
    i2 = jnp.min(jnp.where(lg2 == m2, lane, LANES), axis=-1, keepdims=True)
    e2 = jnp.exp(m2 - m1)
    g1 = 1.0 / (1.0 + e2)
    g2 = e2 / (1.0 + e2)
    out = jnp.where(lane == 0, g1, jnp.where(lane == 1, g2, jnp.where(
        lane == 2, i1.astype(F32), jnp.where(lane == 3, i2.astype(F32), 0.0))))
    o_ref[...] = out


def _router(u2, w_router, tm):
    t, d = u2.shape
    w = jnp.zeros((d, LANES), F32).at[:, :N_EXPERTS].set(w_router)
    return pl.pallas_call(
        _router_kernel,
        grid=(t // tm,),
        in_specs=[pl.BlockSpec((tm, d), lambda i: (i, 0)),
                  _resident((d, LANES), lambda i: (0, 0))],
        out_specs=pl.BlockSpec((tm, LANES), lambda i: (i, 0)),
        out_shape=jax.ShapeDtypeStruct((t, LANES), F32),
        compiler_params=_cparams(("parallel",)),
        name="router",
    )(u2, w)


def _dispatch_kernel(pend_ref, padded_ref, dest_ref, x_ref, o_ref, zero_ref, row_sem, zero_sem,
                     *, tile, n_tiles):
    rows = x_ref.shape[0]

    @pl.when(pl.program_id(0) == 0)
    def _():
        zero_ref[...] = jnp.zeros(zero_ref.shape, zero_ref.dtype)

        def zero_tile(first_row):
            return pltpu.make_async_copy(zero_ref, o_ref.at[pl.ds(first_row, tile)], zero_sem)

        def last_tile_of(e):
            return zero_tile(pl.multiple_of(pend_ref[e] - tile, tile))

        trailing = range(n_tiles - N_EXPERTS, n_tiles)
        for phase in ("start", "wait"):
            for e in range(N_EXPERTS):
                @pl.when(padded_ref[e] > 0)
                def _():
                    getattr(last_tile_of(e), phase)()
            for tl in trailing:
                @pl.when(tl * tile >= pend_ref[N_EXPERTS - 1])
                def _():
                    getattr(zero_tile(tl * tile), phase)()

    def row_copy(r, k):
        row = dest_ref[0, 0, r * TOP_K + k]
        return pltpu.make_async_copy(x_ref.at[pl.ds(r, 1)], o_ref.at[pl.ds(row, 1)], row_sem)

    def issue(r, carry):
        for k in range(TOP_K):
            row_copy(r, k).start()
        return carry

    def drain(r, carry):
        for k in range(TOP_K):
            row_copy(r, k).wait()
        return carry

    lax.fori_loop(0, rows, issue, 0, unroll=8)
    lax.fori_loop(0, rows, drain, 0, unroll=8)


def _dispatch(u2, dest, pend, padded, tile, n_tiles, tm):
    t, d = u2.shape
    dest3 = dest.reshape(t // tm, 1, TOP_K * tm)
    grid_spec = pltpu.PrefetchScalarGridSpec(
        num_scalar_prefetch=2,
        grid=(t // tm,),
        in_specs=[pl.BlockSpec((1, 1, TOP_K * tm), lambda i, pe, pa: (i, 0, 0), memory_space=pltpu.SMEM),
                  pl.BlockSpec((tm, d), lambda i, pe, pa: (i, 0))],
        out_specs=pl.BlockSpec(memory_space=pl.ANY),
        scratch_shapes=[pltpu.VMEM((tile, d), u2.dtype),
                        pltpu.SemaphoreType.DMA(()),
                        pltpu.SemaphoreType.DMA(())],
    )
    return pl.pallas_call(
        functools.partial(_dispatch_kernel, tile=tile, n_tiles=n_tiles),
        grid_spec=grid_spec,
        out_shape=jax.ShapeDtypeStruct((n_tiles * tile, d), u2.dtype),
        compiler_params=_cparams(("arbitrary",)),
        name="moe_dispatch",
    )(pend, padded, dest3, u2)


def _moe_ffn_kernel(te_ref, nu_ref, x_ref, wg_ref, wu_ref, wd_ref, y_ref, acc_ref, xb_ref):
    i = pl.program_id(0)
    f = pl.program_id(1)

    @pl.when(i < nu_ref[0])
    def _():
        @pl.when(f == 0)
        def _():
            acc_ref[...] = jnp.zeros(acc_ref.shape, F32)
            xb_ref[...] = x_ref[...].astype(xb_ref.dtype)

        acc_ref[...] += _swiglu_partial(xb_ref[...], wg_ref[...], wu_ref[...], wd_ref[...])

        @pl.when(f == pl.num_programs(1) - 1)
        def _():
            y_ref[...] = acc_ref[...].astype(y_ref.dtype)

    @pl.when(i >= nu_ref[0])
    def _():
        y_ref[...] = jnp.zeros(y_ref.shape, y_ref.dtype)


def _moe_ffn(x_sorted, tile_expert, n_used, wg, wu, wd, tm, tf):
    p, d = x_sorted.shape
    ff = wg.shape[2]
    nf = ff // tf

    def row(i, f, te, nu):
        return (jnp.minimum(i, nu[0] - 1), 0)

    def fcol(i, f, nu):
        return jnp.where(i < nu[0], f, nf - 1)

    grid_spec = pltpu.PrefetchScalarGridSpec(
        num_scalar_prefetch=2,
        grid=(p // tm, nf),
        in_specs=[pl.BlockSpec((tm, d), row),
                  pl.BlockSpec((None, d, tf), lambda i, f, te, nu: (te[i], 0, fcol(i, f, nu))),
                  pl.BlockSpec((None, d, tf), lambda i, f, te, nu: (te[i], 0, fcol(i, f, nu))),
                  pl.BlockSpec((None, tf, d), lambda i, f, te, nu: (te[i], fcol(i, f, nu), 0))],
        out_specs=pl.BlockSpec((tm, d), lambda i, f, te, nu: (i, 0)),
        scratch_shapes=[pltpu.VMEM((tm, d), F32), pltpu.VMEM((tm, d), wg.dtype)],
    )
    return pl.pallas_call(
        _moe_ffn_kernel,
        grid_spec=grid_spec,
        out_shape=jax.ShapeDtypeStruct((p, d), BF16),
        compiler_params=_cparams(("arbitrary", "arbitrary")),
        name="moe_ffn",
    )(tile_expert, n_used, x_sorted, wg, wu, wd)


def _moe(u2, w_router, wg, wu, wd, tm_route, tm, tf):
    t, d = u2.shape
    route = _router(u2, w_router, tm_route)
    expert = route[:, 2:2 + TOP_K].astype(jnp.int32).reshape(-1)
    onehot = (expert[:, None] == jnp.arange(N_EXPERTS, dtype=jnp.int32)[None, :]).astype(jnp.int32)
    csum = jnp.cumsum(onehot, axis=0)
    rank = jnp.sum(csum * onehot, axis=1) - 1
    counts = csum[-1]
    padded = ((counts + tm - 1) // tm) * tm
    pend = jnp.cumsum(padded)
    pstart = pend - padded
    dest = pstart[expert] + rank
    n_tiles = (TOP_K * t) // tm + N_EXPERTS
    tile_start = jnp.arange(n_tiles, dtype=jnp.int32) * tm
    tile_expert = jnp.minimum(jnp.sum((tile_start[:, None] >= pend[None, :]).astype(jnp.int32), axis=1),
                              N_EXPERTS - 1).astype(jnp.int32)
    n_used = (pend[-1] // tm).astype(jnp.int32).reshape(1)
    x_sorted = _dispatch(u2, dest.astype(jnp.int32), pend.astype(jnp.int32), padded.astype(jnp.int32),
                         tm, n_tiles, tm_route)
    y_sorted = _moe_ffn(x_sorted, tile_expert, n_used, wg, wu, wd, tm, tf)
    dest2 = dest.reshape(t, TOP_K)
    y0 = y_sorted.at[dest2[:, 0]].get(mode="promise_in_bounds")
    y1 = y_sorted.at[dest2[:, 1]].get(mode="promise_in_bounds")
    return [y0, y1], route


def _ple_kernel(*refs, n_y, gated, final):
    h_ref, y_refs = refs[0], refs[1:1 + n_y]
    rest = refs[1 + n_y:]
    if gated:
        r = rest[0][...]
        rest = rest[1:]
    gp_ref, wg_ref, p_ref, wp_ref, gn_ref = rest[:5]
    out_refs = rest[5:]
    h2 = h_ref[...]
    for k, y_ref in enumerate(y_refs):
        y = y_ref[...].astype(F32)
        h2 = h2 + (r[:, k:k + 1] * y if gated else y)
    u3 = _rms(h2, gp_ref[...], EPS).astype(BF16)
    gate = jax.nn.sigmoid(jnp.dot(u3, wg_ref[...], preferred_element_type=F32))
    emb = jnp.dot(p_ref[...].astype(BF16), wp_ref[...], preferred_element_type=F32)
    h3 = h2 + emb * gate
    if final:
        out_refs[0][...] = _rms(h3, gn_ref[...], EPS)
    else:
        out_refs[0][...] = h3
        out_refs[1][...] = _rms(h3, gn_ref[...], EPS).astype(out_refs[1].dtype)


def _ple(h1, ys, route, ple_gain, w_gate, p, w_proj, next_gain, final, tm):
    t, d = h1.shape
    pd = p.shape[1]
    blk = pl.BlockSpec((tm, d), lambda i: (i, 0))
    vec = _resident((1, d), lambda i: (0, 0))
    gated = route is not None
    in_specs = [blk] + [blk] * len(ys)
    args = [h1] + list(ys)
    if gated:
        in_specs.append(pl.BlockSpec((tm, LANES), lambda i: (i, 0)))
        args.append(route)
    in_specs += [vec, _resident((d, d), lambda i: (0, 0)), pl.BlockSpec((tm, pd), lambda i: (i, 0)),
                 _resident((pd, d), lambda i: (0, 0)), vec]
    args += [ple_gain.reshape(1, d), w_gate, p, w_proj, next_gain.reshape(1, d)]
    if final:
        out_specs = [blk]
        out_shape = [jax.ShapeDtypeStruct((t, d), F32)]
    else:
        out_specs = [blk, blk]
        out_shape = [jax.ShapeDtypeStruct((t, d), F32), jax.ShapeDtypeStruct((t, d), BF16)]
    return pl.pallas_call(
        functools.partial(_ple_kernel, n_y=len(ys), gated=gated, final=final),
        grid=(t // tm,),
        in_specs=in_specs,
        out_specs=out_specs,
        out_shape=out_shape,
        compiler_params=_cparams(("parallel",)),
        name="ple",
    )(*args)


def _tiles(t, seq, ff):
    return dict(
        tm_norm=min(512, t),
        tm_in=min(1024, t), tn_in=512,
        tq=min(256, seq), tk=min(512, seq), heads_per_step=4,
        ts=min(512, seq),
        tm_out=min(512, t),
        tm_ffn=min(1024, t), tf_ffn=min(256, ff),
        tm_moe=min(512, t), tf_moe=min(1024, ff),
        tm_ple=min(512, t),
    )


def _rotary_tables(positions):
    inv_freq = ROPE_THETA ** (-jnp.arange(0, ROPE_DIM, 2, dtype=F32) / ROPE_DIM)
    ang = positions.astype(F32).reshape(-1, 1) * inv_freq
    cos, sin = lax.optimization_barrier((jnp.cos(ang), jnp.sin(ang)))
    t = ang.shape[0]
    ones = jnp.ones((t, HEAD_DIM - ROPE_DIM), F32)
    zeros = jnp.zeros((t, HEAD_DIM - ROPE_DIM), F32)
    c = jnp.concatenate([cos, cos, ones], axis=1)
    s = jnp.concatenate([-sin, sin, zeros], axis=1)
    rep = LANES // HEAD_DIM
    c, s = (jnp.concatenate([a] * rep, axis=1) for a in (c, s))
    lane = jnp.arange(2 * LANES, dtype=jnp.int32)
    dim = lane % HEAD_DIM
    src = jnp.where(dim < ROPE_HALF, lane + ROPE_HALF, jnp.where(dim < ROPE_DIM, lane - ROPE_HALF, lane))
    swap = (lane[:, None] == src[None, :]).astype(BF16)
    return c, s, swap


def kernel(x, p, positions, mix_norm, w_in, lambda_q1, lambda_k1, lambda_q2, lambda_k2, subln_gain, w_pool, pool_scale, conv_dw, conv_dw_bias, conv_ln_gain, conv_ln_bias, w_conv_pw, w_out, ffn_norm, w_dense_gate, w_dense_up, w_dense_down, w_router, w_exp_gate, w_exp_up, w_exp_down, ple_norm, w_ple_gate, w_ple_proj, final_norm):
    batch, seq, d = x.shape
    depth = w_in.shape[0]
    t = batch * seq
    attn_w = (w_in.shape[2] - 2 * conv_dw.shape[2] - pool_scale.shape[1]) // 3
    n_heads = attn_w // V_DIM
    qk_cols = attn_w
    pool_col = 3 * attn_w
    ff = w_dense_gate.shape[2]
    ts = _tiles(t, seq, ff)

    rot_c, rot_s, rot_swap = _rotary_tables(positions)
    h = x.reshape(t, d)
    u = _rmsnorm(h, mix_norm[0], ts["tm_norm"])
    out = None
    experts_bf16 = None
    for i in range(depth):
        lam_init = 0.8 - 0.6 * math.exp(-0.3 * i)
        final = i == depth - 1
        z = _inproj(u, w_in[i].astype(BF16), rot_c, rot_s, rot_swap, qk_cols, ts["tm_in"], ts["tn_in"])
        lam_params = jnp.stack([lambda_q1[i], lambda_k1[i], lambda_q2[i], lambda_k2[i]]).astype(F32)
        o_attn = _attention(z, lam_params, subln_gain[i], batch, seq, n_heads, lam_init, ts["tq"],
                            ts["tk"], ts["heads_per_step"])
        o_local = _local_mixers(z, w_pool[i].astype(BF16), pool_scale[i], conv_dw[i], conv_dw_bias[i],
                                conv_ln_gain[i], conv_ln_bias[i], w_conv_pw[i].astype(BF16),
                                seq, pool_col, ts["ts"])
        h1, u2 = _outproj(o_attn, o_local, w_out[i].astype(BF16), h, ffn_norm[i], ts["tm_out"],
                          BF16 if i % 2 == 0 else F32)
        j = i // 2
        if i % 2 == 0:
            casts = ()
            if i + 1 < depth:
                jn = (i + 1) // 2
                casts = ((w_exp_gate[jn], 2), (w_exp_up[jn], 2), (w_exp_down[jn], 1))
            y, experts_bf16 = _dense_ffn(u2, w_dense_gate[j].astype(BF16), w_dense_up[j].astype(BF16),
                                         w_dense_down[j].astype(BF16), ts["tm_ffn"], ts["tf_ffn"], casts)
            ys, route = [y], None
        else:
            if experts_bf16 is None:
                experts_bf16 = [w.astype(BF16) for w in (w_exp_gate[j], w_exp_up[j], w_exp_down[j])]
            ys, route = _moe(u2, w_router[j], *experts_bf16, ts["tm_out"], ts["tm_moe"], ts["tf_moe"])
            experts_bf16 = None
        next_gain = final_norm if final else mix_norm[i + 1]
        res = _ple(h1, ys, route, ple_norm[i], w_ple_gate[i].astype(BF16), p[i].reshape(t, -1),
                   w_ple_proj[i].astype(BF16), next_gain, final, ts["tm_ple"])
        if final:
            out = res[0]
        else:
            h, u = res
    return out.reshape(batch, seq, d)
```

```python
import functools
import math

import jax
import jax.numpy as jnp
from jax import lax
from jax.experimental import pallas as pl
from jax.experimental.pallas import tpu as pltpu

F32 = jnp.float32
BF16 = jnp.bfloat16

EPS = 1e-6
SUBLN_EPS = 1e-5
LN_EPS = 1e-5
HEAD_DIM = 64
V_DIM = 2 * HEAD_DIM
ROPE_DIM = HEAD_DIM // 4
ROPE_HALF = ROPE_DIM // 2
ROPE_THETA = 500000.0
LOG2_E = math.log2(math.e)
POOL_WINDOWS = (2, 4, 8, 16)
POOL_GROUP_DIM = 128
CONV_KERNEL = 31
N_EXPERTS = 8
TOP_K = 2

LANES = 128
V7X_VMEM_BYTES = 64 * 1024 * 1024
VMEM_LIMIT = 62 * 1024 * 1024
POOL_HALO = 16
CONV_HALO = 32


def _cparams(sem):
    return pltpu.CompilerParams(dimension_semantics=sem, vmem_limit_bytes=VMEM_LIMIT)


def _resident(shape, index_map):
    return pl.BlockSpec(shape, index_map, pipeline_mode=pl.Buffered(1))


def _rms(x, gain, eps):
    return x * lax.rsqrt(jnp.mean(x * x, axis=-1, keepdims=True) + eps) * gain


def _rmsnorm_kernel(x_ref, g_ref, o_ref):
    o_ref[...] = _rms(x_ref[...], g_ref[...], EPS).astype(o_ref.dtype)


def _rmsnorm(x, gain, tm):
    t, d = x.shape
    return pl.pallas_call(
        _rmsnorm_kernel,
        grid=(t // tm,),
        in_specs=[pl.BlockSpec((tm, d), lambda i: (i, 0)), pl.BlockSpec((1, d), lambda i: (0, 0))],
        out_specs=pl.BlockSpec((tm, d), lambda i: (i, 0)),
        out_shape=jax.ShapeDtypeStruct((t, d), BF16),
        compiler_params=_cparams(("parallel",)),
        name="rmsnorm",
    )(x, gain.reshape(1, d))


def _inproj_kernel(u_ref, w_ref, c_ref, s_ref, swap_ref, z_ref, *, n_rot_blocks, n_q_blocks, q_scale):
    j = pl.program_id(1)
    acc = jnp.dot(u_ref[...], w_ref[...], preferred_element_type=F32)
    tn = acc.shape[1]

    @pl.when(j < n_rot_blocks)
    def _():
        reps = tn // LANES
        c = jnp.concatenate([c_ref[...]] * reps, axis=1)
        s = jnp.concatenate([s_ref[...]] * reps, axis=1)
        pw = swap_ref.shape[0]
        ab = acc.astype(swap_ref.dtype)
        partner = jnp.concatenate(
            [jnp.dot(ab[:, b * pw:(b + 1) * pw], swap_ref[...], preferred_element_type=F32)
             for b in range(tn // pw)], axis=1)
        r = acc * c + partner * s
        scale = jnp.where(j < n_q_blocks, q_scale, 1.0).astype(F32)
        z_ref[...] = (r * scale).astype(z_ref.dtype)

    @pl.when(j >= n_rot_blocks)
    def _():
        z_ref[...] = acc.astype(z_ref.dtype)


def _inproj(u, w, rot_c, rot_s, swap, qk_cols, tm, tn):
    t, d = u.shape
    n = w.shape[1]
    kern = functools.partial(_inproj_kernel, n_rot_blocks=2 * qk_cols // tn, n_q_blocks=qk_cols // tn,
                             q_scale=HEAD_DIM ** -0.5 * LOG2_E)
    tab = pl.BlockSpec((tm, LANES), lambda i, j: (i, 0))
    return pl.pallas_call(
        kern,
        grid=(t // tm, n // tn),
        in_specs=[pl.BlockSpec((tm, d), lambda i, j: (i, 0)),
                  pl.BlockSpec((d, tn), lambda i, j: (0, j)),
                  tab, tab, _resident(swap.shape, lambda i, j: (0, 0))],
        out_specs=pl.BlockSpec((tm, tn), lambda i, j: (i, j)),
        out_shape=jax.ShapeDtypeStruct((t, n), BF16),
        compiler_params=_cparams(("parallel", "arbitrary")),
        name="inproj",
    )(u, w, rot_c, rot_s, swap)


def _attn_kernel(lam_ref, gain_ref, q_ref, k_ref, v_ref, o_ref, qs_ref, vx_ref, m_ref, acc_ref, sa_ref,
                 sb_ref, *, tq, tk, hp, lam_init):
    qi = pl.program_id(2)
    per = tk // tq

    @pl.when(qi == 0)
    def _():
        for a in range(hp):
            vx_ref[a, :, 0:V_DIM] = v_ref[:, a * V_DIM:(a + 1) * V_DIM]
            vx_ref[a, :, V_DIM:2 * V_DIM] = jnp.ones((v_ref.shape[0], V_DIM), v_ref.dtype)

    lane = lax.broadcasted_iota(jnp.int32, (tq, LANES), 1)
    for a in range(hp):
        q = q_ref[:, a * LANES:(a + 1) * LANES]
        zero = jnp.zeros_like(q)
        qs_ref[a, 0:tq, :] = jnp.where(lane < HEAD_DIM, q, zero)
        qs_ref[a, tq:2 * tq, :] = jnp.where(lane >= HEAD_DIM, q, zero)
    m_ref[...] = jnp.full(m_ref.shape, -jnp.inf, F32)
    acc_ref[...] = jnp.zeros(acc_ref.shape, F32)

    def scores(j, s_ref):
        start = pl.multiple_of(j * tk, tk)
        for a in range(hp):
            k = k_ref[pl.ds(start, tk), a * LANES:(a + 1) * LANES]
            s_ref[a] = lax.dot_general(qs_ref[a], k, (((1,), (1,)), ((), ())),
                                       preferred_element_type=F32)

    def update(j, s_ref, masked):
        start = pl.multiple_of(j * tk, tk)
        for a in range(hp):
            vx = vx_ref[a, pl.ds(start, tk), :]
            s = s_ref[a]
            if masked:
                row = lax.broadcasted_iota(jnp.int32, s.shape, 0)
                col = lax.broadcasted_iota(jnp.int32, s.shape, 1)
                s = jnp.where(col <= (row & (tq - 1)) + (qi % per) * tq, s, -jnp.inf)
            m_old = m_ref[a]
            m_new = jnp.maximum(m_old, jnp.max(s, axis=-1, keepdims=True))
            alpha = jnp.exp2(m_old - m_new)
            p = jnp.exp2(s - jnp.concatenate([m_new] * (tk // LANES), axis=1))
            pv = jnp.dot(p.astype(vx.dtype), vx, preferred_element_type=F32)
            acc_ref[a] = jnp.concatenate([alpha, alpha], axis=1) * acc_ref[a] + pv
            m_ref[a] = m_new

    n_full = qi // per
    scores(0, sa_ref)

    def body(j, carry):
        def even():
            scores(j + 1, sb_ref)
            update(j, sa_ref, False)

        def odd():
            scores(j + 1, sa_ref)
            update(j, sb_ref, False)

        lax.cond((j & 1) == 0, even, odd)
        return carry

    lax.fori_loop(0, n_full, body, 0)
    lax.cond((n_full & 1) == 0, lambda: update(n_full, sa_ref, True), lambda: update(n_full, sb_ref, True))

    lp = lam_ref[...]
    lam = (jnp.exp(jnp.sum(lp[0:1] * lp[1:2], axis=-1, keepdims=True))
           - jnp.exp(jnp.sum(lp[2:3] * lp[3:4], axis=-1, keepdims=True)) + lam_init)
    for a in range(hp):
        o_all = acc_ref[a, :, 0:V_DIM] / acc_ref[a, :, V_DIM:2 * V_DIM]
        o = o_all[0:tq] - lam * o_all[tq:2 * tq]
        y = _rms(o, gain_ref[...], SUBLN_EPS) * (1.0 - lam_init)
        o_ref[:, a * V_DIM:(a + 1) * V_DIM] = y.astype(o_ref.dtype)


def _attention(z, lam_params, subln_gain, batch, seq, n_heads, lam_init, tq, tk, hp):
    t = z.shape[0]
    nq = seq // tq
    assert tq & (tq - 1) == 0 and tk % tq == 0 and seq % tk == 0 and n_heads % hp == 0
    kern = functools.partial(_attn_kernel, tq=tq, tk=tk, hp=hp, lam_init=lam_init)
    ng = n_heads // hp
    w = hp * LANES
    return pl.pallas_call(
        kern,
        grid=(batch, ng, nq),
        in_specs=[pl.BlockSpec(lam_params.shape, lambda b, h, i: (0, 0)),
                  pl.BlockSpec((1, V_DIM), lambda b, h, i: (0, 0)),
                  pl.BlockSpec((tq, w), lambda b, h, i: (b * nq + i, h)),
                  pl.BlockSpec((seq, w), lambda b, h, i: (b, ng + h)),
                  pl.BlockSpec((seq, w), lambda b, h, i: (b, 2 * ng + h))],
        out_specs=pl.BlockSpec((tq, w), lambda b, h, i: (b * nq + i, h)),
        out_shape=jax.ShapeDtypeStruct((t, n_heads * V_DIM), BF16),
        scratch_shapes=[pltpu.VMEM((hp, 2 * tq, LANES), BF16),
                        pltpu.VMEM((hp, seq, 2 * V_DIM), BF16),
                        pltpu.VMEM((hp, 2 * tq, LANES), F32),
                        pltpu.VMEM((hp, 2 * tq, 2 * V_DIM), F32),
                        pltpu.VMEM((hp, 2 * tq, tk), F32),
                        pltpu.VMEM((hp, 2 * tq, tk), F32)],
        compiler_params=_cparams(("parallel", "parallel", "arbitrary")),
        name="diff_attention",
    )(lam_params, subln_gain.reshape(1, V_DIM), z, z, z)


def _local_kernel(zp_ref, zph_ref, za_ref, zah_ref, zb_ref, zbh_ref, wpool_ref, pscale_ref,
                  dw_ref, dwb_ref, lng_ref, lnb_ref, wpw_ref, o_ref, pbuf_ref, cbuf_ref, sbuf_ref,
                  *, ts, tiles_per_seq):
    i = pl.program_id(0)
    tile_in_seq = i % tiles_per_seq
    first = tile_in_seq == 0
    pool_w = zp_ref.shape[1]

    zp = zp_ref[...].astype(F32)
    halo = zph_ref[...].astype(F32)
    pbuf_ref[0:POOL_HALO, :] = jnp.where(first, jnp.zeros_like(halo), halo)
    pbuf_ref[POOL_HALO:POOL_HALO + ts, :] = zp
    pos = tile_in_seq * ts + lax.broadcasted_iota(jnp.int32, (ts, 1), 0)
    for g, w in enumerate(POOL_WINDOWS):
        cols = slice(g * POOL_GROUP_DIM, (g + 1) * POOL_GROUP_DIM)
        zg = zp[:, cols]
        wsum = zg
        for k in range(1, w):
            wsum = wsum + pbuf_ref[POOL_HALO - k:POOL_HALO - k + ts, cols]
        count = jnp.minimum(pos + 1, w).astype(F32)
        d = (wsum / count - zg).astype(BF16)
        y = jnp.dot(d, wpool_ref[g], preferred_element_type=F32) * pscale_ref[:, cols]
        o_ref[:, cols] = y.astype(o_ref.dtype)

    c = za_ref[...].astype(F32) * jax.nn.sigmoid(zb_ref[...].astype(F32))
    ch = zah_ref[...].astype(F32) * jax.nn.sigmoid(zbh_ref[...].astype(F32))
    cbuf_ref[0:CONV_HALO, :] = jnp.where(first, jnp.zeros_like(ch), ch)
    cbuf_ref[CONV_HALO:CONV_HALO + ts, :] = c
    acc = jnp.zeros_like(c) + dwb_ref[...]
    base = CONV_HALO - (CONV_KERNEL - 1)
    sub = 8
    for b in range(sub):
        offs = [o for o in range(base, base + CONV_KERNEL) if o % sub == b]
        if not offs:
            continue
        span = offs[-1] - offs[0] + ts
        sbuf_ref[0:span, :] = cbuf_ref[offs[0]:offs[0] + span, :]
        for o in offs:
            k = o - base
            acc = acc + dw_ref[k:k + 1, :] * sbuf_ref[o - offs[0]:o - offs[0] + ts, :]
    mu = jnp.mean(acc, axis=-1, keepdims=True)
    xc = acc - mu
    yn = xc * lax.rsqrt(jnp.mean(xc * xc, axis=-1, keepdims=True) + LN_EPS) * lng_ref[...] + lnb_ref[...]
    sw = yn * jax.nn.sigmoid(yn)
    out = jnp.dot(sw.astype(BF16), wpw_ref[...], preferred_element_type=F32)
    o_ref[:, pool_w:] = out.astype(o_ref.dtype)


def _local_mixers(z, w_pool, pool_scale, dw, dw_b, ln_g, ln_b, w_pw, seq, pool_col, ts):
    t = z.shape[0]
    pool_w = pool_scale.shape[0]
    conv_w = dw.shape[1]
    assert pool_w == conv_w and pool_col % pool_w == 0
    pc = pool_col // pool_w
    hp = ts // POOL_HALO
    hc = ts // CONV_HALO
    kern = functools.partial(_local_kernel, ts=ts, tiles_per_seq=seq // ts)
    row = lambda shape: _resident(shape, lambda i: (0,) * len(shape))
    return pl.pallas_call(
        kern,
        grid=(t // ts,),
        in_specs=[pl.BlockSpec((ts, pool_w), lambda i: (i, pc)),
                  pl.BlockSpec((POOL_HALO, pool_w), lambda i: (jnp.maximum(i * hp - 1, 0), pc)),
                  pl.BlockSpec((ts, conv_w), lambda i: (i, pc + 1)),
                  pl.BlockSpec((CONV_HALO, conv_w), lambda i: (jnp.maximum(i * hc - 1, 0), pc + 1)),
                  pl.BlockSpec((ts, conv_w), lambda i: (i, pc + 2)),
                  pl.BlockSpec((CONV_HALO, conv_w), lambda i: (jnp.maximum(i * hc - 1, 0), pc + 2)),
                  row(w_pool.shape), row((1, pool_w)), row(dw.shape), row((1, conv_w)),
                  row((1, conv_w)), row((1, conv_w)), row(w_pw.shape)],
        out_specs=pl.BlockSpec((ts, pool_w + conv_w), lambda i: (i, 0)),
        out_shape=jax.ShapeDtypeStruct((t, pool_w + conv_w), BF16),
        scratch_shapes=[pltpu.VMEM((POOL_HALO + ts, pool_w), F32),
                        pltpu.VMEM((CONV_HALO + ts, conv_w), F32),
                        pltpu.VMEM((CONV_HALO + ts, conv_w), F32)],
        compiler_params=_cparams(("parallel",)),
        name="local_mixers",
    )(z, z, z, z, z, z, w_pool, pool_scale.reshape(1, pool_w), dw, dw_b.reshape(1, conv_w),
      ln_g.reshape(1, conv_w), ln_b.reshape(1, conv_w), w_pw)


def _outproj_kernel(oa_ref, ol_ref, w_ref, h_ref, g_ref, h1_ref, u_ref):
    ka = oa_ref.shape[1]
    mix = jnp.dot(oa_ref[...], w_ref[0:ka, :], preferred_element_type=F32)
    mix = mix + jnp.dot(ol_ref[...], w_ref[ka:, :], preferred_element_type=F32)
    h1 = h_ref[...] + mix
    h1_ref[...] = h1
    u_ref[...] = _rms(h1, g_ref[...], EPS).astype(u_ref.dtype)


def _outproj(o_attn, o_local, w_out, h, gain, tm, u_dtype):
    t, d = h.shape
    ka, kl = o_attn.shape[1], o_local.shape[1]
    return pl.pallas_call(
        _outproj_kernel,
        grid=(t // tm,),
        in_specs=[pl.BlockSpec((tm, ka), lambda i: (i, 0)),
                  pl.BlockSpec((tm, kl), lambda i: (i, 0)),
                  _resident((ka + kl, d), lambda i: (0, 0)),
                  pl.BlockSpec((tm, d), lambda i: (i, 0)),
                  _resident((1, d), lambda i: (0, 0))],
        out_specs=[pl.BlockSpec((tm, d), lambda i: (i, 0)), pl.BlockSpec((tm, d), lambda i: (i, 0))],
        out_shape=[jax.ShapeDtypeStruct((t, d), F32), jax.ShapeDtypeStruct((t, d), u_dtype)],
        compiler_params=_cparams(("parallel",)),
        name="outproj",
    )(o_attn, o_local, w_out, h, gain.reshape(1, d))


def _swiglu_partial(x, wg, wu, wd):
    g = jnp.dot(x, wg, preferred_element_type=F32)
    u = jnp.dot(x, wu, preferred_element_type=F32)
    hidden = (g * jax.nn.sigmoid(g)) * u
    return jnp.dot(hidden.astype(wd.dtype), wd, preferred_element_type=F32)


def _dense_ffn_kernel(*refs, n_cast):
    u_ref, wg_ref, wu_ref, wd_ref = refs[:4]
    cast_in = refs[4:4 + n_cast]
    y_ref = refs[4 + n_cast]
    cast_out = refs[5 + n_cast:5 + 2 * n_cast]
    acc_ref = refs[5 + 2 * n_cast]
    f = pl.program_id(1)

    @pl.when(f == 0)
    def _():
        acc_ref[...] = jnp.zeros(acc_ref.shape, F32)

    acc_ref[...] += _swiglu_partial(u_ref[...], wg_ref[...], wu_ref[...], wd_ref[...])

    for src, dst in zip(cast_in, cast_out):
        dst[...] = src[...].astype(dst.dtype)

    @pl.when(f == pl.num_programs(1) - 1)
    def _():
        y_ref[...] = acc_ref[...].astype(y_ref.dtype)


def _cast_slices(arrays_axes, n_steps, nf):
    specs = []
    for arr, axis in arrays_axes:
        e = arr.shape[0]
        if n_steps % e:
            return None
        per = n_steps // e
        if arr.shape[axis] % per:
            return None
        width = arr.shape[axis] // per
        align = LANES if axis == 2 else 16
        if width % align:
            return None
        block = tuple(None if a == 0 else (width if a == axis else arr.shape[a]) for a in range(3))

        def imap(i, f, per=per, axis=axis):
            s = i * nf + f
            return (s // per, s % per, 0) if axis == 1 else (s // per, 0, s % per)

        specs.append(pl.BlockSpec(block, imap))
    return specs


def _dense_ffn(u, wg, wu, wd, tm, tf, casts=()):
    t, d = u.shape
    ff = wg.shape[1]
    grid = (t // tm, ff // tf)
    cast_specs = _cast_slices(casts, grid[0] * grid[1], grid[1]) if casts else []
    if cast_specs is None:
        cast_specs, casts, unsupported = [], (), True
    else:
        unsupported = False
    n_cast = len(cast_specs)
    outs = pl.pallas_call(
        functools.partial(_dense_ffn_kernel, n_cast=n_cast),
        grid=grid,
        in_specs=[pl.BlockSpec((tm, d), lambda i, f: (i, 0)),
                  pl.BlockSpec((d, tf), lambda i, f: (0, f)),
                  pl.BlockSpec((d, tf), lambda i, f: (0, f)),
                  pl.BlockSpec((tf, d), lambda i, f: (f, 0))] + cast_specs,
        out_specs=[pl.BlockSpec((tm, d), lambda i, f: (i, 0))] + cast_specs,
        out_shape=[jax.ShapeDtypeStruct((t, d), BF16)]
        + [jax.ShapeDtypeStruct(a.shape, BF16) for a, _ in casts],
        scratch_shapes=[pltpu.VMEM((tm, d), F32)],
        compiler_params=_cparams(("parallel", "arbitrary")),
        name="dense_ffn",
    )(u, wg, wu, wd, *[a for a, _ in casts])
    return outs[0], (None if unsupported else list(outs[1:]))


def _router_kernel(u_ref, w_ref, o_ref):
    logits = jnp.dot(u_ref[...], w_ref[...], preferred_element_type=F32, precision=lax.Precision.HIGHEST)
    lane = lax.broadcasted_iota(jnp.int32, logits.shape, 1)
    neg = jnp.full(logits.shape, -jnp.inf, F32)
    lg = jnp.where(lane < N_EXPERTS, logits, neg)
    m1 = jnp.max(lg, axis=-1, keepdims=True)
    i1 = jnp.min(jnp.where(lg == m1, lane, LANES), axis=-1, keepdims=True)
    lg2 = jnp.where(lane == i1, neg, lg)
    m2 = jnp.max(lg2, axis=-1, keepdims=True)
    i2 = jnp.min(jnp.where(lg2 == m2, lane, LANES), axis=-1, keepdims=True)
    e2 = jnp.exp(m2 - m1)
    g1 = 1.0 / (1.0 + e2)
    g2 = e2 / (1.0 + e2)
    out = jnp.where(lane == 0, g1, jnp.where(lane == 1, g2, jnp.where(
        lane == 2, i1.astype(F32), jnp.where(lane == 3, i2.astype(F32), 0.0))))
    o_ref[...] = out


def _router(u2, w_router, tm):
    t, d = u2.shape
    w = jnp.zeros((d, LANES), F32).at[:, :N_EXPERTS].set(w_router)
    return pl.pallas_call(
        _router_kernel,
        grid=(t // tm,),
        in_specs=[pl.BlockSpec((tm, d), lambda i: (i, 0)),
                  _resident((d, LANES), lambda i: (0, 0))],
        out_specs=pl.BlockSpec((tm, LANES), lambda i: (i, 0)),
        out_shape=jax.ShapeDtypeStruct((t, LANES), F32),
        compiler_params=_cparams(("parallel",)),
        name="router",
    )(u2, w)


def _dispatch_kernel(pend_ref, padded_ref, dest_ref, x_ref, o_ref, zero_ref, row_sem, zero_sem,
                     *, tile, n_tiles):
    rows = x_ref.shape[0]

    @pl.when(pl.program_id(0) == 0)
    def _():
        zero_ref[...] = jnp.zeros(zero_ref.shape, zero_ref.dtype)

        def zero_tile(first_row):
            return pltpu.make_async_copy(zero_ref, o_ref.at[pl.ds(first_row, tile)], zero_sem)

        def last_tile_of(e):
            return zero_tile(pl.multiple_of(pend_ref[e] - tile, tile))

        trailing = range(n_tiles - N_EXPERTS, n_tiles)
        for phase in ("start", "wait"):
            for e in range(N_EXPERTS):
                @pl.when(padded_ref[e] > 0)
                def _():
                    getattr(last_tile_of(e), phase)()
            for tl in trailing:
                @pl.when(tl * tile >= pend_ref[N_EXPERTS - 1])
                def _():
                    getattr(zero_tile(tl * tile), phase)()

    def row_copy(r, k):
        row = dest_ref[0, 0, r * TOP_K + k]
        return pltpu.make_async_copy(x_ref.at[pl.ds(r, 1)], o_ref.at[pl.ds(row, 1)], row_sem)

    def issue(r, carry):
        for k in range(TOP_K):
            row_copy(r, k).start()
        return carry

    def drain(r, carry):
        for k in range(TOP_K):
            row_copy(r, k).wait()
        return carry

    lax.fori_loop(0, rows, issue, 0, unroll=8)
    lax.fori_loop(0, rows, drain, 0, unroll=8)


def _dispatch(u2, dest, pend, padded, tile, n_tiles, tm):
    t, d = u2.shape
    dest3 = dest.reshape(t // tm, 1, TOP_K * tm)
    grid_spec = pltpu.PrefetchScalarGridSpec(
        num_scalar_prefetch=2,
        grid=(t // tm,),
        in_specs=[pl.BlockSpec((1, 1, TOP_K * tm), lambda i, pe, pa: (i, 0, 0), memory_space=pltpu.SMEM),
                  pl.BlockSpec((tm, d), lambda i, pe, pa: (i, 0))],
        out_specs=pl.BlockSpec(memory_space=pl.ANY),
        scratch_shapes=[pltpu.VMEM((tile, d), u2.dtype),
                        pltpu.SemaphoreType.DMA(()),
                        pltpu.SemaphoreType.DMA(())],
    )
    return pl.pallas_call(
        functools.partial(_dispatch_kernel, tile=tile, n_tiles=n_tiles),
        grid_spec=grid_spec,
        out_shape=jax.ShapeDtypeStruct((n_tiles * tile, d), u2.dtype),
        compiler_params=_cparams(("arbitrary",)),
        name="moe_dispatch",
    )(pend, padded, dest3, u2)


def _moe_ffn_kernel(te_ref, nu_ref, x_ref, wg_ref, wu_ref, wd_ref, y_ref, acc_ref, xb_ref):
    i = pl.program_id(0)
    f = pl.program_id(1)

    @pl.when(i < nu_ref[0])
    def _():
        @pl.when(f == 0)
        def _():
            acc_ref[...] = jnp.zeros(acc_ref.shape, F32)
            xb_ref[...] = x_ref[...].astype(xb_ref.dtype)

        acc_ref[...] += _swiglu_partial(xb_ref[...], wg_ref[...], wu_ref[...], wd_ref[...])

        @pl.when(f == pl.num_programs(1) - 1)
        def _():
            y_ref[...] = acc_ref[...].astype(y_ref.dtype)

    @pl.when(i >= nu_ref[0])
    def _():
        y_ref[...] = jnp.zeros(y_ref.shape, y_ref.dtype)


def _moe_ffn(x_sorted, tile_expert, n_used, wg, wu, wd, tm, tf):
    p, d = x_sorted.shape
    ff = wg.shape[2]
    nf = ff // tf

    def row(i, f, te, nu):
        return (jnp.minimum(i, nu[0] - 1), 0)

    def fcol(i, f, nu):
        return jnp.where(i < nu[0], f, nf - 1)

    grid_spec = pltpu.PrefetchScalarGridSpec(
        num_scalar_prefetch=2,
        grid=(p // tm, nf),
        in_specs=[pl.BlockSpec((tm, d), row),
                  pl.BlockSpec((None, d, tf), lambda i, f, te, nu: (te[i], 0, fcol(i, f, nu))),
                  pl.BlockSpec((None, d, tf), lambda i, f, te, nu: (te[i], 0, fcol(i, f, nu))),
                  pl.BlockSpec((None, tf, d), lambda i, f, te, nu: (te[i], fcol(i, f, nu), 0))],
        out_specs=pl.BlockSpec((tm, d), lambda i, f, te, nu: (i, 0)),
        scratch_shapes=[pltpu.VMEM((tm, d), F32), pltpu.VMEM((tm, d), wg.dtype)],
    )
    return pl.pallas_call(
        _moe_ffn_kernel,
        grid_spec=grid_spec,
        out_shape=jax.ShapeDtypeStruct((p, d), BF16),
        compiler_params=_cparams(("arbitrary", "arbitrary")),
        name="moe_ffn",
    )(tile_expert, n_used, x_sorted, wg, wu, wd)


def _moe(u2, w_router, wg, wu, wd, tm_route, tm, tf):
    t, d = u2.shape
    route = _router(u2, w_router, tm_route)
    expert = route[:, 2:2 + TOP_K].astype(jnp.int32).reshape(-1)
    onehot = (expert[:, None] == jnp.arange(N_EXPERTS, dtype=jnp.int32)[None, :]).astype(jnp.int32)
    csum = jnp.cumsum(onehot, axis=0)
    rank = jnp.sum(csum * onehot, axis=1) - 1
    counts = csum[-1]
    padded = ((counts + tm - 1) // tm) * tm
    pend = jnp.cumsum(padded)
    pstart = pend - padded
    dest = pstart[expert] + rank
    n_tiles = (TOP_K * t) // tm + N_EXPERTS
    tile_start = jnp.arange(n_tiles, dtype=jnp.int32) * tm
    tile_expert = jnp.minimum(jnp.sum((tile_start[:, None] >= pend[None, :]).astype(jnp.int32), axis=1),
                              N_EXPERTS - 1).astype(jnp.int32)
    n_used = (pend[-1] // tm).astype(jnp.int32).reshape(1)
    x_sorted = _dispatch(u2, dest.astype(jnp.int32), pend.astype(jnp.int32), padded.astype(jnp.int32),
                         tm, n_tiles, tm_route)
    y_sorted = _moe_ffn(x_sorted, tile_expert, n_used, wg, wu, wd, tm, tf)
    dest2 = dest.reshape(t, TOP_K)
    y0 = y_sorted.at[dest2[:, 0]].get(mode="promise_in_bounds")
    y1 = y_sorted.at[dest2[:, 1]].get(mode="promise_in_bounds")
    return [y0, y1], route


def _ple_kernel(*refs, n_y, gated, final):
    h_ref, y_refs = refs[0], refs[1:1 + n_y]
    rest = refs[1 + n_y:]
    if gated:
        r = rest[0][...]
        rest = rest[1:]
    gp_ref, wg_ref, p_ref, wp_ref, gn_ref = rest[:5]
    out_refs = rest[5:]
    h2 = h_ref[...]
    for k, y_ref in enumerate(y_refs):
        y = y_ref[...].astype(F32)
        h2 = h2 + (r[:, k:k + 1] * y if gated else y)
    u3 = _rms(h2, gp_ref[...], EPS).astype(BF16)
    gate = jax.nn.sigmoid(jnp.dot(u3, wg_ref[...], preferred_element_type=F32))
    emb = jnp.dot(p_ref[...].astype(BF16), wp_ref[...], preferred_element_type=F32)
    h3 = h2 + emb * gate
    if final:
        out_refs[0][...] = _rms(h3, gn_ref[...], EPS)
    else:
        out_refs[0][...] = h3
        out_refs[1][...] = _rms(h3, gn_ref[...], EPS).astype(out_refs[1].dtype)


def _ple(h1, ys, route, ple_gain, w_gate, p, w_proj, next_gain, final, tm):
    t, d = h1.shape
    pd = p.shape[1]
    blk = pl.BlockSpec((tm, d), lambda i: (i, 0))
    vec = _resident((1, d), lambda i: (0, 0))
    gated = route is not None
    in_specs = [blk] + [blk] * len(ys)
    args = [h1] + list(ys)
    if gated:
        in_specs.append(pl.BlockSpec((tm, LANES), lambda i: (i, 0)))
        args.append(route)
    in_specs += [vec, _resident((d, d), lambda i: (0, 0)), pl.BlockSpec((tm, pd), lambda i: (i, 0)),
                 _resident((pd, d), lambda i: (0, 0)), vec]
    args += [ple_gain.reshape(1, d), w_gate, p, w_proj, next_gain.reshape(1, d)]
    if final:
        out_specs = [blk]
        out_shape = [jax.ShapeDtypeStruct((t, d), F32)]
    else:
        out_specs = [blk, blk]
        out_shape = [jax.ShapeDtypeStruct((t, d), F32), jax.ShapeDtypeStruct((t, d), BF16)]
    return pl.pallas_call(
        functools.partial(_ple_kernel, n_y=len(ys), gated=gated, final=final),
        grid=(t // tm,),
        in_specs=in_specs,
        out_specs=out_specs,
        out_shape=out_shape,
        compiler_params=_cparams(("parallel",)),
        name="ple",
    )(*args)


def _tiles(t, seq, ff):
    return dict(
        tm_norm=min(512, t),
        tm_in=min(2048, t), tn_in=512,
        tq=min(512, seq), tk=min(512, seq), heads_per_step=2,
        ts=min(512, seq),
        tm_out=min(512, t),
        tm_ffn=min(1024, t), tf_ffn=min(512, ff),
        tm_moe=min(512, t), tf_moe=min(1024, ff),
        tm_ple=min(512, t),
    )


def _rotary_tables(positions):
    inv_freq = ROPE_THETA ** (-jnp.arange(0, ROPE_DIM, 2, dtype=F32) / ROPE_DIM)
    ang = positions.astype(F32).reshape(-1, 1) * inv_freq
    cos, sin = lax.optimization_barrier((jnp.cos(ang), jnp.sin(ang)))
    cos_rep = jnp.tile(cos, (1, LANES // ROPE_HALF))
    sin_rep = jnp.tile(sin, (1, LANES // ROPE_HALF))
    dim128 = (jnp.arange(LANES, dtype=jnp.int32) % HEAD_DIM)[None, :]
    c = jnp.where(dim128 < ROPE_DIM, cos_rep, 1.0)
    s = jnp.where(dim128 < ROPE_HALF, -sin_rep, jnp.where(dim128 < ROPE_DIM, sin_rep, 0.0))
    lane = jnp.arange(2 * LANES, dtype=jnp.int32)
    dim = lane % HEAD_DIM
    src = jnp.where(dim < ROPE_HALF, lane + ROPE_HALF, jnp.where(dim < ROPE_DIM, lane - ROPE_HALF, lane))
    swap = (lane[:, None] == src[None, :]).astype(BF16)
    return c, s, swap


def kernel(x, p, positions, mix_norm, w_in, lambda_q1, lambda_k1, lambda_q2, lambda_k2, subln_gain, w_pool, pool_scale, conv_dw, conv_dw_bias, conv_ln_gain, conv_ln_bias, w_conv_pw, w_out, ffn_norm, w_dense_gate, w_dense_up, w_dense_down, w_router, w_exp_gate, w_exp_up, w_exp_down, ple_norm, w_ple_gate, w_ple_proj, final_norm):
    batch, seq, d = x.shape
    depth = w_in.shape[0]
    t = batch * seq
    attn_w = (w_in.shape[2] - 2 * conv_dw.shape[2] - pool_scale.shape[1]) // 3
    n_heads = attn_w // V_DIM
    qk_cols = attn_w
    pool_col = 3 * attn_w
    ff = w_dense_gate.shape[2]
    ts = _tiles(t, seq, ff)

    rot_c, rot_s, rot_swap = _rotary_tables(positions)
    h = x.reshape(t, d)
    u = _rmsnorm(h, mix_norm[0], ts["tm_norm"])
    out = None
    experts_bf16 = None
    for i in range(depth):
        lam_init = 0.8 - 0.6 * math.exp(-0.3 * i)
        final = i == depth - 1
        z = _inproj(u, w_in[i].astype(BF16), rot_c, rot_s, rot_swap, qk_cols, ts["tm_in"], ts["tn_in"])
        lam_params = jnp.stack([lambda_q1[i], lambda_k1[i], lambda_q2[i], lambda_k2[i]]).astype(F32)
        o_attn = _attention(z, lam_params, subln_gain[i], batch, seq, n_heads, lam_init, ts["tq"],
                            ts["tk"], ts["heads_per_step"])
        o_local = _local_mixers(z, w_pool[i].astype(BF16), pool_scale[i], conv_dw[i], conv_dw_bias[i],
                                conv_ln_gain[i], conv_ln_bias[i], w_conv_pw[i].astype(BF16),
                                seq, pool_col, ts["ts"])
        h1, u2 = _outproj(o_attn, o_local, w_out[i].astype(BF16), h, ffn_norm[i], ts["tm_out"],
                          BF16 if i % 2 == 0 else F32)
        j = i // 2
        if i % 2 == 0:
            casts = ()
            if i + 1 < depth:
                jn = (i + 1) // 2
                casts = ((w_exp_gate[jn], 2), (w_exp_up[jn], 2), (w_exp_down[jn], 1))
            y, experts_bf16 = _dense_ffn(u2, w_dense_gate[j].astype(BF16), w_dense_up[j].astype(BF16),
                                         w_dense_down[j].astype(BF16), ts["tm_ffn"], ts["tf_ffn"], casts)
            ys, route = [y], None
        else:
            if experts_bf16 is None:
                experts_bf16 = [w.astype(BF16) for w in (w_exp_gate[j], w_exp_up[j], w_exp_down[j])]
            ys, route = _moe(u2, w_router[j], *experts_bf16, ts["tm_out"], ts["tm_moe"], ts["tf_moe"])
            experts_bf16 = None
        next_gain = final_norm if final else mix_norm[i + 1]
        res = _ple(h1, ys, route, ple_norm[i], w_ple_gate[i].astype(BF16), p[i].reshape(t, -1),
                   w_ple_proj[i].astype(BF16), next_gain, final, ts["tm_ple"])
        if final:
            out = res[0]
        else:
            h, u = res
    return out.reshape(batch, seq, d)
```

```python
import functools
import math

import jax
import jax.numpy as jnp
from jax import lax
from jax.experimental import pallas as pl
from jax.experimental.pallas import tpu as pltpu

F32 = jnp.float32
BF16 = jnp.bfloat16

EPS = 1e-6
SUBLN_EPS = 1e-5
LN_EPS = 1e-5
HEAD_DIM = 64
V_DIM = 2 * HEAD_DIM
ROPE_DIM = HEAD_DIM // 4
ROPE_HALF = ROPE_DIM // 2
ROPE_THETA = 500000.0
LOG2_E = math.log2(math.e)
POOL_WINDOWS = (2, 4, 8, 16)
POOL_GROUP_DIM = 128
CONV_KERNEL = 31
N_EXPERTS = 8
TOP_K = 2

LANES = 128
V7X_VMEM_BYTES = 64 * 1024 * 1024
VMEM_LIMIT = 62 * 1024 * 1024
POOL_HALO = 16
CONV_HALO = 32


def _cparams(sem):
    return pltpu.CompilerParams(dimension_semantics=sem, vmem_limit_bytes=VMEM_LIMIT)


def _resident(shape, index_map):
    return pl.BlockSpec(shape, index_map, pipeline_mode=pl.Buffered(1))


def _rms(x, gain, eps):
    return x * lax.rsqrt(jnp.mean(x * x, axis=-1, keepdims=True) + eps) * gain


def _rmsnorm_kernel(x_ref, g_ref, o_ref):
    o_ref[...] = _rms(x_ref[...], g_ref[...], EPS).astype(o_ref.dtype)


def _rmsnorm(x, gain, tm):
    t, d = x.shape
    return pl.pallas_call(
        _rmsnorm_kernel,
        grid=(t // tm,),
        in_specs=[pl.BlockSpec((tm, d), lambda i: (i, 0)), pl.BlockSpec((1, d), lambda i: (0, 0))],
        out_specs=pl.BlockSpec((tm, d), lambda i: (i, 0)),
        out_shape=jax.ShapeDtypeStruct((t, d), BF16),
        compiler_params=_cparams(("parallel",)),
        name="rmsnorm",
    )(x, gain.reshape(1, d))


def _inproj_kernel(u_ref, w_ref, c_ref, s_ref, swap_ref, z_ref, *, n_rot_blocks, n_q_blocks, q_scale):
    j = pl.program_id(1)
    u = u_ref[...]
    acc = jnp.dot(u, w_ref[...].astype(u.dtype), preferred_element_type=F32)
    tn = acc.shape[1]

    @pl.when(j < n_rot_blocks)
    def _():
        reps = tn // LANES
        c = jnp.concatenate([c_ref[...]] * reps, axis=1)
        s = jnp.concatenate([s_ref[...]] * reps, axis=1)
        pw = swap_ref.shape[0]
        ab = acc.astype(swap_ref.dtype)
        partner = jnp.concatenate(
            [jnp.dot(ab[:, b * pw:(b + 1) * pw], swap_ref[...], preferred_element_type=F32)
             for b in range(tn // pw)], axis=1)
        r = acc * c + partner * s
        scale = jnp.where(j < n_q_blocks, q_scale, 1.0).astype(F32)
        z_ref[...] = (r * scale).astype(z_ref.dtype)

    @pl.when(j >= n_rot_blocks)
    def _():
        z_ref[...] = acc.astype(z_ref.dtype)


def _inproj(u, w_all, layer, rot_c, rot_s, swap, qk_cols, tm, tn):
    t, d = u.shape
    n = w_all.shape[2]
    kern = functools.partial(_inproj_kernel, n_rot_blocks=2 * qk_cols // tn, n_q_blocks=qk_cols // tn,
                             q_scale=HEAD_DIM ** -0.5 * LOG2_E)
    tab = pl.BlockSpec((tm, LANES), lambda i, j: (i, 0))
    return pl.pallas_call(
        kern,
        grid=(t // tm, n // tn),
        in_specs=[pl.BlockSpec((tm, d), lambda i, j: (i, 0)),
                  pl.BlockSpec((None, d, tn), lambda i, j: (layer, 0, j)),
                  tab, tab, _resident(swap.shape, lambda i, j: (0, 0))],
        out_specs=pl.BlockSpec((tm, tn), lambda i, j: (i, j)),
        out_shape=jax.ShapeDtypeStruct((t, n), BF16),
        compiler_params=_cparams(("parallel", "arbitrary")),
        name="inproj",
    )(u, w_all, rot_c, rot_s, swap)


def _attn_kernel(lam_ref, gain_ref, q_ref, k_ref, v_ref, o_ref, qs_ref, vx_ref, m_ref, acc_ref, sa_ref,
                 sb_ref, *, tq, tk, hp, lam_init):
    qi = pl.program_id(2)
    per = tk // tq

    @pl.when(qi == 0)
    def _():
        for a in range(hp):
            vx_ref[a, :, 0:V_DIM] = v_ref[:, a * V_DIM:(a + 1) * V_DIM]
            vx_ref[a, :, V_DIM:2 * V_DIM] = jnp.ones((v_ref.shape[0], V_DIM), v_ref.dtype)

    lane = lax.broadcasted_iota(jnp.int32, (tq, LANES), 1)
    for a in range(hp):
        q = q_ref[:, a * LANES:(a + 1) * LANES]
        zero = jnp.zeros_like(q)
        qs_ref[a, 0:tq, :] = jnp.where(lane < HEAD_DIM, q, zero)
        qs_ref[a, tq:2 * tq, :] = jnp.where(lane >= HEAD_DIM, q, zero)
    m_ref[...] = jnp.full(m_ref.shape, -jnp.inf, F32)
    acc_ref[...] = jnp.zeros(acc_ref.shape, F32)

    def scores(j, s_ref):
        start = pl.multiple_of(j * tk, tk)
        for a in range(hp):
            k = k_ref[pl.ds(start, tk), a * LANES:(a + 1) * LANES]
            s_ref[a] = lax.dot_general(qs_ref[a], k, (((1,), (1,)), ((), ())),
                                       preferred_element_type=F32)

    def update(j, s_ref, masked):
        start = pl.multiple_of(j * tk, tk)
        for a in range(hp):
            vx = vx_ref[a, pl.ds(start, tk), :]
            s = s_ref[a]
            if masked:
                row = lax.broadcasted_iota(jnp.int32, s.shape, 0)
                col = lax.broadcasted_iota(jnp.int32, s.shape, 1)
                s = jnp.where(col <= (row & (tq - 1)) + (qi % per) * tq, s, -jnp.inf)
            m_old = m_ref[a]
            m_new = jnp.maximum(m_old, jnp.max(s, axis=-1, keepdims=True))
            alpha = jnp.exp2(m_old - m_new)
            p = jnp.exp2(s - jnp.concatenate([m_new] * (tk // LANES), axis=1))
            pv = jnp.dot(p.astype(vx.dtype), vx, preferred_element_type=F32)
            acc_ref[a] = jnp.concatenate([alpha, alpha], axis=1) * acc_ref[a] + pv
            m_ref[a] = m_new

    n_full = qi // per
    scores(0, sa_ref)

    def body(j, carry):
        def even():
            scores(j + 1, sb_ref)
            update(j, sa_ref, False)

        def odd():
            scores(j + 1, sa_ref)
            update(j, sb_ref, False)

        lax.cond((j & 1) == 0, even, odd)
        return carry

    lax.fori_loop(0, n_full, body, 0)
    lax.cond((n_full & 1) == 0, lambda: update(n_full, sa_ref, True), lambda: update(n_full, sb_ref, True))

    lp = lam_ref[...]
    lam = (jnp.exp(jnp.sum(lp[0:1] * lp[1:2], axis=-1, keepdims=True))
           - jnp.exp(jnp.sum(lp[2:3] * lp[3:4], axis=-1, keepdims=True)) + lam_init)
    for a in range(hp):
        o_all = acc_ref[a, :, 0:V_DIM] / acc_ref[a, :, V_DIM:2 * V_DIM]
        o = o_all[0:tq] - lam * o_all[tq:2 * tq]
        y = _rms(o, gain_ref[...], SUBLN_EPS) * (1.0 - lam_init)
        o_ref[:, a * V_DIM:(a + 1) * V_DIM] = y.astype(o_ref.dtype)


def _attention(z, lam_params, subln_gain, batch, seq, n_heads, lam_init, tq, tk, hp):
    t = z.shape[0]
    nq = seq // tq
    assert tq & (tq - 1) == 0 and tk % tq == 0 and seq % tk == 0 and n_heads % hp == 0
    kern = functools.partial(_attn_kernel, tq=tq, tk=tk, hp=hp, lam_init=lam_init)
    ng = n_heads // hp
    w = hp * LANES
    return pl.pallas_call(
        kern,
        grid=(batch, ng, nq),
        in_specs=[pl.BlockSpec(lam_params.shape, lambda b, h, i: (0, 0)),
                  pl.BlockSpec((1, V_DIM), lambda b, h, i: (0, 0)),
                  pl.BlockSpec((tq, w), lambda b, h, i: (b * nq + i, h)),
                  pl.BlockSpec((seq, w), lambda b, h, i: (b, ng + h)),
                  pl.BlockSpec((seq, w), lambda b, h, i: (b, 2 * ng + h))],
        out_specs=pl.BlockSpec((tq, w), lambda b, h, i: (b * nq + i, h)),
        out_shape=jax.ShapeDtypeStruct((t, n_heads * V_DIM), BF16),
        scratch_shapes=[pltpu.VMEM((hp, 2 * tq, LANES), BF16),
                        pltpu.VMEM((hp, seq, 2 * V_DIM), BF16),
                        pltpu.VMEM((hp, 2 * tq, LANES), F32),
                        pltpu.VMEM((hp, 2 * tq, 2 * V_DIM), F32),
                        pltpu.VMEM((hp, 2 * tq, tk), F32),
                        pltpu.VMEM((hp, 2 * tq, tk), F32)],
        compiler_params=_cparams(("parallel", "parallel", "arbitrary")),
        name="diff_attention",
    )(lam_params, subln_gain.reshape(1, V_DIM), z, z, z)


def _local_kernel(zp_ref, zph_ref, za_ref, zah_ref, zb_ref, zbh_ref, wpool_ref, pscale_ref,
                  dw_ref, dwb_ref, lng_ref, lnb_ref, wpw_ref, o_ref, pbuf_ref, cbuf_ref, sbuf_ref,
                  *, ts, tiles_per_seq):
    i = pl.program_id(0)
    tile_in_seq = i % tiles_per_seq
    first = tile_in_seq == 0
    pool_w = zp_ref.shape[1]

    zp = zp_ref[...].astype(F32)
    halo = zph_ref[...].astype(F32)
    pbuf_ref[0:POOL_HALO, :] = jnp.where(first, jnp.zeros_like(halo), halo)
    pbuf_ref[POOL_HALO:POOL_HALO + ts, :] = zp
    pos = tile_in_seq * ts + lax.broadcasted_iota(jnp.int32, (ts, 1), 0)
    for g, w in enumerate(POOL_WINDOWS):
        cols = slice(g * POOL_GROUP_DIM, (g + 1) * POOL_GROUP_DIM)
        zg = zp[:, cols]
        wsum = zg
        for k in range(1, w):
            wsum = wsum + pbuf_ref[POOL_HALO - k:POOL_HALO - k + ts, cols]
        count = jnp.minimum(pos + 1, w).astype(F32)
        d = (wsum / count - zg).astype(BF16)
        y = jnp.dot(d, wpool_ref[g], preferred_element_type=F32) * pscale_ref[:, cols]
        o_ref[:, cols] = y.astype(o_ref.dtype)

    c = za_ref[...].astype(F32) * jax.nn.sigmoid(zb_ref[...].astype(F32))
    ch = zah_ref[...].astype(F32) * jax.nn.sigmoid(zbh_ref[...].astype(F32))
    cbuf_ref[0:CONV_HALO, :] = jnp.where(first, jnp.zeros_like(ch), ch)
    cbuf_ref[CONV_HALO:CONV_HALO + ts, :] = c
    acc = jnp.zeros_like(c) + dwb_ref[...]
    base = CONV_HALO - (CONV_KERNEL - 1)
    sub = 8
    for b in range(sub):
        offs = [o for o in range(base, base + CONV_KERNEL) if o % sub == b]
        if not offs:
            continue
        span = offs[-1] - offs[0] + ts
        sbuf_ref[0:span, :] = cbuf_ref[offs[0]:offs[0] + span, :]
        for o in offs:
            k = o - base
            acc = acc + dw_ref[k:k + 1, :] * sbuf_ref[o - offs[0]:o - offs[0] + ts, :]
    mu = jnp.mean(acc, axis=-1, keepdims=True)
    xc = acc - mu
    yn = xc * lax.rsqrt(jnp.mean(xc * xc, axis=-1, keepdims=True) + LN_EPS) * lng_ref[...] + lnb_ref[...]
    sw = yn * jax.nn.sigmoid(yn)
    out = jnp.dot(sw.astype(BF16), wpw_ref[...], preferred_element_type=F32)
    o_ref[:, pool_w:] = out.astype(o_ref.dtype)


def _local_mixers(z, w_pool, pool_scale, dw, dw_b, ln_g, ln_b, w_pw, seq, pool_col, ts):
    t = z.shape[0]
    pool_w = pool_scale.shape[0]
    conv_w = dw.shape[1]
    assert pool_w == conv_w and pool_col % pool_w == 0
    pc = pool_col // pool_w
    hp = ts // POOL_HALO
    hc = ts // CONV_HALO
    kern = functools.partial(_local_kernel, ts=ts, tiles_per_seq=seq // ts)
    row = lambda shape: _resident(shape, lambda i: (0,) * len(shape))
    return pl.pallas_call(
        kern,
        grid=(t // ts,),
        in_specs=[pl.BlockSpec((ts, pool_w), lambda i: (i, pc)),
                  pl.BlockSpec((POOL_HALO, pool_w), lambda i: (jnp.maximum(i * hp - 1, 0), pc)),
                  pl.BlockSpec((ts, conv_w), lambda i: (i, pc + 1)),
                  pl.BlockSpec((CONV_HALO, conv_w), lambda i: (jnp.maximum(i * hc - 1, 0), pc + 1)),
                  pl.BlockSpec((ts, conv_w), lambda i: (i, pc + 2)),
                  pl.BlockSpec((CONV_HALO, conv_w), lambda i: (jnp.maximum(i * hc - 1, 0), pc + 2)),
                  row(w_pool.shape), row((1, pool_w)), row(dw.shape), row((1, conv_w)),
                  row((1, conv_w)), row((1, conv_w)), row(w_pw.shape)],
        out_specs=pl.BlockSpec((ts, pool_w + conv_w), lambda i: (i, 0)),
        out_shape=jax.ShapeDtypeStruct((t, pool_w + conv_w), BF16),
        scratch_shapes=[pltpu.VMEM((POOL_HALO + ts, pool_w), F32),
                        pltpu.VMEM((CONV_HALO + ts, conv_w), F32),
                        pltpu.VMEM((CONV_HALO + ts, conv_w), F32)],
        compiler_params=_cparams(("parallel",)),
        name="local_mixers",
    )(z, z, z, z, z, z, w_pool, pool_scale.reshape(1, pool_w), dw, dw_b.reshape(1, conv_w),
      ln_g.reshape(1, conv_w), ln_b.reshape(1, conv_w), w_pw)


def _top2_route(u, w_router_padded):
    u_hi = u.astype(BF16)
    u_lo = (u - u_hi.astype(F32)).astype(BF16)
    w_hi = w_router_padded.astype(BF16)
    w_lo = (w_router_padded - w_hi.astype(F32)).astype(BF16)
    logits = (jnp.dot(u_hi, w_hi, preferred_element_type=F32)
              + (jnp.dot(u_hi, w_lo, preferred_element_type=F32)
                 + jnp.dot(u_lo, w_hi, preferred_element_type=F32)))
    lane = lax.broadcasted_iota(jnp.int32, logits.shape, 1)
    neg = jnp.full(logits.shape, -jnp.inf, F32)
    lg = jnp.where(lane < N_EXPERTS, logits, neg)
    m1 = jnp.max(lg, axis=-1, keepdims=True)
    i1 = jnp.min(jnp.where(lg == m1, lane, LANES), axis=-1, keepdims=True)
    lg2 = jnp.where(lane == i1, neg, lg)
    m2 = jnp.max(lg2, axis=-1, keepdims=True)
    i2 = jnp.min(jnp.where(lg2 == m2, lane, LANES), axis=-1, keepdims=True)
    e2 = jnp.exp(m2 - m1)
    g1 = 1.0 / (1.0 + e2)
    g2 = e2 / (1.0 + e2)
    return jnp.where(lane == 0, g1, jnp.where(lane == 1, g2, jnp.where(
        lane == 2, i1.astype(F32), jnp.where(lane == 3, i2.astype(F32), 0.0))))


def _router_kernel(u_ref, w_ref, o_ref):
    o_ref[...] = _top2_route(u_ref[...], w_ref[...])


def _router(u2, w_router, tm):
    t, d = u2.shape
    w = jnp.zeros((d, LANES), F32).at[:, :N_EXPERTS].set(w_router)
    return pl.pallas_call(
        _router_kernel,
        grid=(t // tm,),
        in_specs=[pl.BlockSpec((tm, d), lambda i: (i, 0)),
                  _resident((d, LANES), lambda i: (0, 0))],
        out_specs=pl.BlockSpec((tm, LANES), lambda i: (i, 0)),
        out_shape=jax.ShapeDtypeStruct((t, LANES), F32),
        compiler_params=_cparams(("parallel",)),
        name="router",
    )(u2, w)


def _outproj_kernel(oa_ref, ol_ref, w_ref, h_ref, g_ref, h1_ref, u_ref):
    ka = oa_ref.shape[1]
    mix = jnp.dot(oa_ref[...], w_ref[0:ka, :], preferred_element_type=F32)
    mix = mix + jnp.dot(ol_ref[...], w_ref[ka:, :], preferred_element_type=F32)
    h1 = h_ref[...] + mix
    h1_ref[...] = h1
    u_ref[...] = _rms(h1, g_ref[...], EPS).astype(u_ref.dtype)


def _outproj(o_attn, o_local, w_out, h, gain, tm, u_dtype):
    t, d = h.shape
    ka, kl = o_attn.shape[1], o_local.shape[1]
    blk = pl.BlockSpec((tm, d), lambda i: (i, 0))
    return pl.pallas_call(
        _outproj_kernel,
        grid=(t // tm,),
        in_specs=[pl.BlockSpec((tm, ka), lambda i: (i, 0)),
                  pl.BlockSpec((tm, kl), lambda i: (i, 0)),
                  _resident((ka + kl, d), lambda i: (0, 0)),
                  blk,
                  _resident((1, d), lambda i: (0, 0))],
        out_specs=[blk, blk],
        out_shape=[jax.ShapeDtypeStruct((t, d), F32), jax.ShapeDtypeStruct((t, d), u_dtype)],
        compiler_params=_cparams(("parallel",)),
        name="outproj",
    )(o_attn, o_local, w_out, h, gain.reshape(1, d))


def _swiglu_partial(x, wg, wu, wd):
    g = jnp.dot(x, wg, preferred_element_type=F32)
    u = jnp.dot(x, wu, preferred_element_type=F32)
    hidden = (g * jax.nn.sigmoid(g)) * u
    return jnp.dot(hidden.astype(wd.dtype), wd, preferred_element_type=F32)


def _dense_ffn_kernel(*refs, n_cast):
    u_ref, wg_ref, wu_ref, wd_ref = refs[:4]
    cast_in = refs[4:4 + n_cast]
    y_ref = refs[4 + n_cast]
    cast_out = refs[5 + n_cast:5 + 2 * n_cast]
    acc_ref = refs[5 + 2 * n_cast]
    f = pl.program_id(1)

    @pl.when(f == 0)
    def _():
        acc_ref[...] = jnp.zeros(acc_ref.shape, F32)

    acc_ref[...] += _swiglu_partial(u_ref[...], wg_ref[...], wu_ref[...], wd_ref[...])

    for src, dst in zip(cast_in, cast_out):
        dst[...] = src[...].astype(dst.dtype)

    @pl.when(f == pl.num_programs(1) - 1)
    def _():
        y_ref[...] = acc_ref[...].astype(y_ref.dtype)


def _cast_slices(arrays_axes, n_steps, nf):
    specs = []
    for arr, axis in arrays_axes:
        e = arr.shape[0]
        if n_steps % e:
            return None
        per = n_steps // e
        if arr.shape[axis] % per:
            return None
        width = arr.shape[axis] // per
        align = LANES if axis == 2 else 16
        if width % align:
            return None
        block = tuple(None if a == 0 else (width if a == axis else arr.shape[a]) for a in range(3))

        def imap(i, f, per=per, axis=axis):
            s = i * nf + f
            return (s // per, s % per, 0) if axis == 1 else (s // per, 0, s % per)

        specs.append(pl.BlockSpec(block, imap))
    return specs


def _dense_ffn(u, wg, wu, wd, tm, tf, casts=()):
    t, d = u.shape
    ff = wg.shape[1]
    grid = (t // tm, ff // tf)
    cast_specs = _cast_slices(casts, grid[0] * grid[1], grid[1]) if casts else []
    if cast_specs is None:
        cast_specs, casts, unsupported = [], (), True
    else:
        unsupported = False
    n_cast = len(cast_specs)
    outs = pl.pallas_call(
        functools.partial(_dense_ffn_kernel, n_cast=n_cast),
        grid=grid,
        in_specs=[pl.BlockSpec((tm, d), lambda i, f: (i, 0)),
                  pl.BlockSpec((d, tf), lambda i, f: (0, f)),
                  pl.BlockSpec((d, tf), lambda i, f: (0, f)),
                  pl.BlockSpec((tf, d), lambda i, f: (f, 0))] + cast_specs,
        out_specs=[pl.BlockSpec((tm, d), lambda i, f: (i, 0))] + cast_specs,
        out_shape=[jax.ShapeDtypeStruct((t, d), BF16)]
        + [jax.ShapeDtypeStruct(a.shape, BF16) for a, _ in casts],
        scratch_shapes=[pltpu.VMEM((tm, d), F32)],
        compiler_params=_cparams(("parallel", "arbitrary")),
        name="dense_ffn",
    )(u, wg, wu, wd, *[a for a, _ in casts])
    return outs[0], (None if unsupported else list(outs[1:]))


def _dispatch_kernel(pend_ref, padded_ref, dest_ref, x_ref, o_ref, zero_ref, row_sem, zero_sem,
                     *, tile, n_tiles):
    rows = x_ref.shape[0]

    @pl.when(pl.program_id(0) == 0)
    def _():
        zero_ref[...] = jnp.zeros(zero_ref.shape, zero_ref.dtype)

        def zero_tile(first_row):
            return pltpu.make_async_copy(zero_ref, o_ref.at[pl.ds(first_row, tile)], zero_sem)

        def last_tile_of(e):
            return zero_tile(pl.multiple_of(pend_ref[e] - tile, tile))

        trailing = range(n_tiles - N_EXPERTS, n_tiles)
        for phase in ("start", "wait"):
            for e in range(N_EXPERTS):
                @pl.when(padded_ref[e] > 0)
                def _():
                    getattr(last_tile_of(e), phase)()
            for tl in trailing:
                @pl.when(tl * tile >= pend_ref[N_EXPERTS - 1])
                def _():
                    getattr(zero_tile(tl * tile), phase)()

    def row_copy(r, k):
        row = dest_ref[0, 0, r * TOP_K + k]
        return pltpu.make_async_copy(x_ref.at[pl.ds(r, 1)], o_ref.at[pl.ds(row, 1)], row_sem)

    def issue(r, carry):
        for k in range(TOP_K):
            row_copy(r, k).start()
        return carry

    def drain(r, carry):
        for k in range(TOP_K):
            row_copy(r, k).wait()
        return carry

    lax.fori_loop(0, rows, issue, 0, unroll=8)
    lax.fori_loop(0, rows, drain, 0, unroll=8)


def _dispatch(u2, dest, pend, padded, tile, n_tiles, tm):
    t, d = u2.shape
    dest3 = dest.reshape(t // tm, 1, TOP_K * tm)
    grid_spec = pltpu.PrefetchScalarGridSpec(
        num_scalar_prefetch=2,
        grid=(t // tm,),
        in_specs=[pl.BlockSpec((1, 1, TOP_K * tm), lambda i, pe, pa: (i, 0, 0), memory_space=pltpu.SMEM),
                  pl.BlockSpec((tm, d), lambda i, pe, pa: (i, 0))],
        out_specs=pl.BlockSpec(memory_space=pl.ANY),
        scratch_shapes=[pltpu.VMEM((tile, d), u2.dtype),
                        pltpu.SemaphoreType.DMA(()),
                        pltpu.SemaphoreType.DMA(())],
    )
    return pl.pallas_call(
        functools.partial(_dispatch_kernel, tile=tile, n_tiles=n_tiles),
        grid_spec=grid_spec,
        out_shape=jax.ShapeDtypeStruct((n_tiles * tile, d), u2.dtype),
        compiler_params=_cparams(("arbitrary",)),
        name="moe_dispatch",
    )(pend, padded, dest3, u2)


def _moe_ffn_kernel(te_ref, nu_ref, x_ref, wg_ref, wu_ref, wd_ref, y_ref, acc_ref, xb_ref):
    i = pl.program_id(0)
    f = pl.program_id(1)

    @pl.when(i < nu_ref[0])
    def _():
        @pl.when(f == 0)
        def _():
            acc_ref[...] = jnp.zeros(acc_ref.shape, F32)
            xb_ref[...] = x_ref[...].astype(xb_ref.dtype)

        acc_ref[...] += _swiglu_partial(xb_ref[...], wg_ref[...], wu_ref[...], wd_ref[...])

        @pl.when(f == pl.num_programs(1) - 1)
        def _():
            y_ref[...] = acc_ref[...].astype(y_ref.dtype)

    @pl.when(i >= nu_ref[0])
    def _():
        y_ref[...] = jnp.zeros(y_ref.shape, y_ref.dtype)


def _moe_ffn(x_sorted, tile_expert, n_used, wg, wu, wd, tm, tf):
    p, d = x_sorted.shape
    ff = wg.shape[2]
    nf = ff // tf

    def row(i, f, te, nu):
        return (jnp.minimum(i, nu[0] - 1), 0)

    def fcol(i, f, nu):
        return jnp.where(i < nu[0], f, nf - 1)

    grid_spec = pltpu.PrefetchScalarGridSpec(
        num_scalar_prefetch=2,
        grid=(p // tm, nf),
        in_specs=[pl.BlockSpec((tm, d), row),
                  pl.BlockSpec((None, d, tf), lambda i, f, te, nu: (te[i], 0, fcol(i, f, nu))),
                  pl.BlockSpec((None, d, tf), lambda i, f, te, nu: (te[i], 0, fcol(i, f, nu))),
                  pl.BlockSpec((None, tf, d), lambda i, f, te, nu: (te[i], fcol(i, f, nu), 0))],
        out_specs=pl.BlockSpec((tm, d), lambda i, f, te, nu: (i, 0)),
        scratch_shapes=[pltpu.VMEM((tm, d), F32), pltpu.VMEM((tm, d), wg.dtype)],
    )
    return pl.pallas_call(
        _moe_ffn_kernel,
        grid_spec=grid_spec,
        out_shape=jax.ShapeDtypeStruct((p, d), BF16),
        compiler_params=_cparams(("arbitrary", "arbitrary")),
        name="moe_ffn",
    )(tile_expert, n_used, x_sorted, wg, wu, wd)


def _moe(u2, route, wg, wu, wd, tm_route, tm, tf):
    t, d = u2.shape
    expert = route[:, 2:2 + TOP_K].astype(jnp.int32).reshape(-1)
    onehot = (expert[:, None] == jnp.arange(N_EXPERTS, dtype=jnp.int32)[None, :]).astype(jnp.int32)
    csum = jnp.cumsum(onehot, axis=0)
    rank = jnp.sum(csum * onehot, axis=1) - 1
    counts = csum[-1]
    padded = ((counts + tm - 1) // tm) * tm
    pend = jnp.cumsum(padded)
    pstart = pend - padded
    dest = pstart[expert] + rank
    n_tiles = (TOP_K * t) // tm + N_EXPERTS
    tile_start = jnp.arange(n_tiles, dtype=jnp.int32) * tm
    tile_expert = jnp.minimum(jnp.sum((tile_start[:, None] >= pend[None, :]).astype(jnp.int32), axis=1),
                              N_EXPERTS - 1).astype(jnp.int32)
    n_used = (pend[-1] // tm).astype(jnp.int32).reshape(1)
    x_sorted = _dispatch(u2, dest.astype(jnp.int32), pend.astype(jnp.int32), padded.astype(jnp.int32),
                         tm, n_tiles, tm_route)
    y_sorted = _moe_ffn(x_sorted, tile_expert, n_used, wg, wu, wd, tm, tf)
    dest2 = dest.reshape(t, TOP_K)
    y0 = y_sorted.at[dest2[:, 0]].get(mode="promise_in_bounds")
    y1 = y_sorted.at[dest2[:, 1]].get(mode="promise_in_bounds")
    return [y0, y1]


def _ple_kernel(*refs, n_y, gated, final):
    h_ref, y_refs = refs[0], refs[1:1 + n_y]
    rest = refs[1 + n_y:]
    if gated:
        r = rest[0][...]
        rest = rest[1:]
    gp_ref, wg_ref, p_ref, wp_ref, gn_ref = rest[:5]
    out_refs = rest[5:]
    h2 = h_ref[...]
    for k, y_ref in enumerate(y_refs):
        y = y_ref[...].astype(F32)
        h2 = h2 + (r[:, k:k + 1] * y if gated else y)
    u3 = _rms(h2, gp_ref[...], EPS).astype(BF16)
    gate = jax.nn.sigmoid(jnp.dot(u3, wg_ref[...], preferred_element_type=F32))
    emb = jnp.dot(p_ref[...].astype(BF16), wp_ref[...], preferred_element_type=F32)
    h3 = h2 + emb * gate
    if final:
        out_refs[0][...] = _rms(h3, gn_ref[...], EPS)
    else:
        out_refs[0][...] = h3
        out_refs[1][...] = _rms(h3, gn_ref[...], EPS).astype(out_refs[1].dtype)


def _ple(h1, ys, route, ple_gain, w_gate, p, w_proj, next_gain, final, tm):
    t, d = h1.shape
    pd = p.shape[1]
    blk = pl.BlockSpec((tm, d), lambda i: (i, 0))
    vec = _resident((1, d), lambda i: (0, 0))
    gated = route is not None
    in_specs = [blk] + [blk] * len(ys)
    args = [h1] + list(ys)
    if gated:
        in_specs.append(pl.BlockSpec((tm, LANES), lambda i: (i, 0)))
        args.append(route)
    in_specs += [vec, _resident((d, d), lambda i: (0, 0)), pl.BlockSpec((tm, pd), lambda i: (i, 0)),
                 _resident((pd, d), lambda i: (0, 0)), vec]
    args += [ple_gain.reshape(1, d), w_gate, p, w_proj, next_gain.reshape(1, d)]
    if final:
        out_specs = [blk]
        out_shape = [jax.ShapeDtypeStruct((t, d), F32)]
    else:
        out_specs = [blk, blk]
        out_shape = [jax.ShapeDtypeStruct((t, d), F32), jax.ShapeDtypeStruct((t, d), BF16)]
    return pl.pallas_call(
        functools.partial(_ple_kernel, n_y=len(ys), gated=gated, final=final),
        grid=(t // tm,),
        in_specs=in_specs,
        out_specs=out_specs,
        out_shape=out_shape,
        compiler_params=_cparams(("parallel",)),
        name="ple",
    )(*args)


def _tiles(t, seq, ff):
    return dict(
        tm_norm=min(512, t),
        tm_in=min(2048, t), tn_in=512,
        tq=min(512, seq), tk=min(512, seq), heads_per_step=2,
        ts=min(512, seq),
        tm_out=min(512, t),
        tm_ffn=min(1024, t), tf_ffn=min(512, ff),
        tm_moe=min(512, t), tf_moe=min(1024, ff),
        tm_ple=min(512, t),
    )


def _rotary_tables(positions):
    inv_freq = ROPE_THETA ** (-jnp.arange(0, ROPE_DIM, 2, dtype=F32) / ROPE_DIM)
    ang = positions.astype(F32).reshape(-1, 1) * inv_freq
    cos, sin = lax.optimization_barrier((jnp.cos(ang), jnp.sin(ang)))
    cos_rep = jnp.tile(cos, (1, LANES // ROPE_HALF))
    sin_rep = jnp.tile(sin, (1, LANES // ROPE_HALF))
    dim128 = (jnp.arange(LANES, dtype=jnp.int32) % HEAD_DIM)[None, :]
    c = jnp.where(dim128 < ROPE_DIM, cos_rep, 1.0)
    s = jnp.where(dim128 < ROPE_HALF, -sin_rep, jnp.where(dim128 < ROPE_DIM, sin_rep, 0.0))
    lane = jnp.arange(2 * LANES, dtype=jnp.int32)
    dim = lane % HEAD_DIM
    src = jnp.where(dim < ROPE_HALF, lane + ROPE_HALF, jnp.where(dim < ROPE_DIM, lane - ROPE_HALF, lane))
    swap = (lane[:, None] == src[None, :]).astype(BF16)
    return c, s, swap


def kernel(x, p, positions, mix_norm, w_in, lambda_q1, lambda_k1, lambda_q2, lambda_k2, subln_gain, w_pool, pool_scale, conv_dw, conv_dw_bias, conv_ln_gain, conv_ln_bias, w_conv_pw, w_out, ffn_norm, w_dense_gate, w_dense_up, w_dense_down, w_router, w_exp_gate, w_exp_up, w_exp_down, ple_norm, w_ple_gate, w_ple_proj, final_norm):
    batch, seq, d = x.shape
    depth = w_in.shape[0]
    t = batch * seq
    attn_w = (w_in.shape[2] - 2 * conv_dw.shape[2] - pool_scale.shape[1]) // 3
    n_heads = attn_w // V_DIM
    qk_cols = attn_w
    pool_col = 3 * attn_w
    ff = w_dense_gate.shape[2]
    ts = _tiles(t, seq, ff)

    rot_c, rot_s, rot_swap = _rotary_tables(positions)
    h = x.reshape(t, d)
    u = _rmsnorm(h, mix_norm[0], ts["tm_norm"])
    out = None
    experts_bf16 = None
    for i in range(depth):
        lam_init = 0.8 - 0.6 * math.exp(-0.3 * i)
        final = i == depth - 1
        z = _inproj(u, w_in, i, rot_c, rot_s, rot_swap, qk_cols, ts["tm_in"], ts["tn_in"])
        lam_params = jnp.stack([lambda_q1[i], lambda_k1[i], lambda_q2[i], lambda_k2[i]]).astype(F32)
        o_attn = _attention(z, lam_params, subln_gain[i], batch, seq, n_heads, lam_init, ts["tq"],
                            ts["tk"], ts["heads_per_step"])
        o_local = _local_mixers(z, w_pool[i].astype(BF16), pool_scale[i], conv_dw[i], conv_dw_bias[i],
                                conv_ln_gain[i], conv_ln_bias[i], w_conv_pw[i].astype(BF16),
                                seq, pool_col, ts["ts"])
        j = i // 2
        h1, u2 = _outproj(o_attn, o_local, w_out[i].astype(BF16), h, ffn_norm[i], ts["tm_out"],
                          BF16 if i % 2 == 0 else F32)
        if i % 2 == 0:
            casts = ()
            if i + 1 < depth:
                jn = (i + 1) // 2
                casts = ((w_exp_gate[jn], 2), (w_exp_up[jn], 2), (w_exp_down[jn], 1))
            y, experts_bf16 = _dense_ffn(u2, w_dense_gate[j].astype(BF16), w_dense_up[j].astype(BF16),
                                         w_dense_down[j].astype(BF16), ts["tm_ffn"], ts["tf_ffn"], casts)
            ys, route = [y], None
        else:
            if experts_bf16 is None:
                experts_bf16 = [w.astype(BF16) for w in (w_exp_gate[j], w_exp_up[j], w_exp_down[j])]
            route = _router(u2, w_router[j], ts["tm_out"])
            ys = _moe(u2, route, *experts_bf16, ts["tm_out"], ts["tm_moe"], ts["tf_moe"])
            experts_bf16 = None
        next_gain = final_norm if final else mix_norm[i + 1]
        res = _ple(h1, ys, route, ple_norm[i], w_ple_gate[i].astype(BF16), p[i].reshape(t, -1),
                   w_ple_proj[i].astype(BF16), next_gain, final, ts["tm_ple"])
        if final:
            out = res[0]
        else:
            h, u = res
    return out.reshape(batch, seq, d)
```

```python
import functools
import math

import jax
import jax.numpy as jnp
from jax import lax
from jax.experimental import pallas as pl
from jax.experimental.pallas import tpu as pltpu

F32 = jnp.float32
BF16 = jnp.bfloat16

EPS = 1e-6
SUBLN_EPS = 1e-5
LN_EPS = 1e-5
HEAD_DIM = 64
V_DIM = 2 * HEAD_DIM
ROPE_DIM = HEAD_DIM // 4
ROPE_HALF = ROPE_DIM // 2
ROPE_THETA = 500000.0
LOG2_E = math.log2(math.e)
POOL_WINDOWS = (2, 4, 8, 16)
POOL_GROUP_DIM = 128
CONV_KERNEL = 31
N_EXPERTS = 8
TOP_K = 2

LANES = 128
V7X_VMEM_BYTES = 64 * 1024 * 1024
VMEM_LIMIT = 62 * 1024 * 1024
POOL_HALO = 16
CONV_HALO = 32


def _cparams(sem):
    return pltpu.CompilerParams(dimension_semantics=sem, vmem_limit_bytes=VMEM_LIMIT)


def _resident(shape, index_map):
    return pl.BlockSpec(shape, index_map, pipeline_mode=pl.Buffered(1))


def _rms(x, gain, eps):
    return x * lax.rsqrt(jnp.mean(x * x, axis=-1, keepdims=True) + eps) * gain


def _rmsnorm_kernel(x_ref, g_ref, o_ref):
    o_ref[...] = _rms(x_ref[...], g_ref[...], EPS).astype(o_ref.dtype)


def _rmsnorm(x, gain, tm):
    t, d = x.shape
    return pl.pallas_call(
        _rmsnorm_kernel,
        grid=(t // tm,),
        in_specs=[pl.BlockSpec((tm, d), lambda i: (i, 0)), pl.BlockSpec((1, d), lambda i: (0, 0))],
        out_specs=pl.BlockSpec((tm, d), lambda i: (i, 0)),
        out_shape=jax.ShapeDtypeStruct((t, d), BF16),
        compiler_params=_cparams(("parallel",)),
        name="rmsnorm",
    )(x, gain.reshape(1, d))


def _inproj_kernel(u_ref, w_ref, c_ref, s_ref, swap_ref, z_ref, *, n_rot_blocks, n_q_blocks, q_scale):
    j = pl.program_id(1)
    u = u_ref[...]
    acc = jnp.dot(u, w_ref[...].astype(u.dtype), preferred_element_type=F32)
    tn = acc.shape[1]

    @pl.when(j < n_rot_blocks)
    def _():
        reps = tn // LANES
        c = jnp.concatenate([c_ref[...]] * reps, axis=1)
        s = jnp.concatenate([s_ref[...]] * reps, axis=1)
        pw = swap_ref.shape[0]
        ab = acc.astype(swap_ref.dtype)
        partner = jnp.concatenate(
            [jnp.dot(ab[:, b * pw:(b + 1) * pw], swap_ref[...], preferred_element_type=F32)
             for b in range(tn // pw)], axis=1)
        r = acc * c + partner * s
        scale = jnp.where(j < n_q_blocks, q_scale, 1.0).astype(F32)
        z_ref[...] = (r * scale).astype(z_ref.dtype)

    @pl.when(j >= n_rot_blocks)
    def _():
        z_ref[...] = acc.astype(z_ref.dtype)


def _inproj(u, w_all, layer, rot_c, rot_s, swap, qk_cols, tm, tn):
    t, d = u.shape
    n = w_all.shape[2]
    kern = functools.partial(_inproj_kernel, n_rot_blocks=2 * qk_cols // tn, n_q_blocks=qk_cols // tn,
                             q_scale=HEAD_DIM ** -0.5 * LOG2_E)
    tab = pl.BlockSpec((tm, LANES), lambda i, j: (i, 0))
    return pl.pallas_call(
        kern,
        grid=(t // tm, n // tn),
        in_specs=[pl.BlockSpec((tm, d), lambda i, j: (i, 0)),
                  pl.BlockSpec((None, d, tn), lambda i, j: (layer, 0, j)),
                  tab, tab, _resident(swap.shape, lambda i, j: (0, 0))],
        out_specs=pl.BlockSpec((tm, tn), lambda i, j: (i, j)),
        out_shape=jax.ShapeDtypeStruct((t, n), BF16),
        compiler_params=_cparams(("parallel", "arbitrary")),
        name="inproj",
    )(u, w_all, rot_c, rot_s, swap)


def _attn_kernel(lam_ref, gain_ref, q_ref, k_ref, v_ref, o_ref, qs_ref, vx_ref, m_ref, acc_ref, sa_ref,
                 sb_ref, *, tq, tk, hp, lam_init):
    qi = pl.program_id(2)
    per = tk // tq

    @pl.when(qi == 0)
    def _():
        for a in range(hp):
            vx_ref[a, :, 0:V_DIM] = v_ref[:, a * V_DIM:(a + 1) * V_DIM]
            vx_ref[a, :, V_DIM:2 * V_DIM] = jnp.ones((v_ref.shape[0], V_DIM), v_ref.dtype)

    lane = lax.broadcasted_iota(jnp.int32, (tq, LANES), 1)
    for a in range(hp):
        q = q_ref[:, a * LANES:(a + 1) * LANES]
        zero = jnp.zeros_like(q)
        qs_ref[a, 0:tq, :] = jnp.where(lane < HEAD_DIM, q, zero)
        qs_ref[a, tq:2 * tq, :] = jnp.where(lane >= HEAD_DIM, q, zero)
    m_ref[...] = jnp.full(m_ref.shape, -jnp.inf, F32)
    acc_ref[...] = jnp.zeros(acc_ref.shape, F32)

    def scores(j, s_ref):
        start = pl.multiple_of(j * tk, tk)
        for a in range(hp):
            k = k_ref[pl.ds(start, tk), a * LANES:(a + 1) * LANES]
            s_ref[a] = lax.dot_general(qs_ref[a], k, (((1,), (1,)), ((), ())),
                                       preferred_element_type=F32)

    def update(j, s_ref, masked):
        start = pl.multiple_of(j * tk, tk)
        for a in range(hp):
            vx = vx_ref[a, pl.ds(start, tk), :]
            s = s_ref[a]
            if masked:
                row = lax.broadcasted_iota(jnp.int32, s.shape, 0)
                col = lax.broadcasted_iota(jnp.int32, s.shape, 1)
                s = jnp.where(col <= (row & (tq - 1)) + (qi % per) * tq, s, -jnp.inf)
            m_old = m_ref[a]
            m_new = jnp.maximum(m_old, jnp.max(s, axis=-1, keepdims=True))
            alpha = jnp.exp2(m_old - m_new)
            p = jnp.exp2(s - jnp.concatenate([m_new] * (tk // LANES), axis=1))
            pv = jnp.dot(p.astype(vx.dtype), vx, preferred_element_type=F32)
            acc_ref[a] = jnp.concatenate([alpha, alpha], axis=1) * acc_ref[a] + pv
            m_ref[a] = m_new

    n_full = qi // per
    scores(0, sa_ref)

    def body(j, carry):
        def even():
            scores(j + 1, sb_ref)
            update(j, sa_ref, False)

        def odd():
            scores(j + 1, sa_ref)
            update(j, sb_ref, False)

        lax.cond((j & 1) == 0, even, odd)
        return carry

    lax.fori_loop(0, n_full, body, 0)
    lax.cond((n_full & 1) == 0, lambda: update(n_full, sa_ref, True), lambda: update(n_full, sb_ref, True))

    lp = lam_ref[...]
    lam = (jnp.exp(jnp.sum(lp[0:1] * lp[1:2], axis=-1, keepdims=True))
           - jnp.exp(jnp.sum(lp[2:3] * lp[3:4], axis=-1, keepdims=True)) + lam_init)
    for a in range(hp):
        o_all = acc_ref[a, :, 0:V_DIM] / acc_ref[a, :, V_DIM:2 * V_DIM]
        o = o_all[0:tq] - lam * o_all[tq:2 * tq]
        y = _rms(o, gain_ref[...], SUBLN_EPS) * (1.0 - lam_init)
        o_ref[:, a * V_DIM:(a + 1) * V_DIM] = y.astype(o_ref.dtype)


def _attention(z, lam_params, subln_gain, batch, seq, n_heads, lam_init, tq, tk, hp):
    t = z.shape[0]
    nq = seq // tq
    assert tq & (tq - 1) == 0 and tk % tq == 0 and seq % tk == 0 and n_heads % hp == 0
    kern = functools.partial(_attn_kernel, tq=tq, tk=tk, hp=hp, lam_init=lam_init)
    ng = n_heads // hp
    w = hp * LANES
    return pl.pallas_call(
        kern,
        grid=(batch, ng, nq),
        in_specs=[pl.BlockSpec(lam_params.shape, lambda b, h, i: (0, 0)),
                  pl.BlockSpec((1, V_DIM), lambda b, h, i: (0, 0)),
                  pl.BlockSpec((tq, w), lambda b, h, i: (b * nq + i, h)),
                  pl.BlockSpec((seq, w), lambda b, h, i: (b, ng + h)),
                  pl.BlockSpec((seq, w), lambda b, h, i: (b, 2 * ng + h))],
        out_specs=pl.BlockSpec((tq, w), lambda b, h, i: (b * nq + i, h)),
        out_shape=jax.ShapeDtypeStruct((t, n_heads * V_DIM), BF16),
        scratch_shapes=[pltpu.VMEM((hp, 2 * tq, LANES), BF16),
                        pltpu.VMEM((hp, seq, 2 * V_DIM), BF16),
                        pltpu.VMEM((hp, 2 * tq, LANES), F32),
                        pltpu.VMEM((hp, 2 * tq, 2 * V_DIM), F32),
                        pltpu.VMEM((hp, 2 * tq, tk), F32),
                        pltpu.VMEM((hp, 2 * tq, tk), F32)],
        compiler_params=_cparams(("parallel", "parallel", "arbitrary")),
        name="diff_attention",
    )(lam_params, subln_gain.reshape(1, V_DIM), z, z, z)


def _local_kernel(zp_ref, zph_ref, za_ref, zah_ref, zb_ref, zbh_ref, wpool_ref, pscale_ref,
                  dw_ref, dwb_ref, lng_ref, lnb_ref, wpw_ref, o_ref, pbuf_ref, cbuf_ref, sbuf_ref,
                  *, ts, tiles_per_seq):
    i = pl.program_id(0)
    tile_in_seq = i % tiles_per_seq
    first = tile_in_seq == 0
    pool_w = zp_ref.shape[1]

    zp = zp_ref[...].astype(F32)
    halo = zph_ref[...].astype(F32)
    pbuf_ref[0:POOL_HALO, :] = jnp.where(first, jnp.zeros_like(halo), halo)
    pbuf_ref[POOL_HALO:POOL_HALO + ts, :] = zp
    pos = tile_in_seq * ts + lax.broadcasted_iota(jnp.int32, (ts, 1), 0)
    for g, w in enumerate(POOL_WINDOWS):
        cols = slice(g * POOL_GROUP_DIM, (g + 1) * POOL_GROUP_DIM)
        zg = zp[:, cols]
        wsum = zg
        for k in range(1, w):
            wsum = wsum + pbuf_ref[POOL_HALO - k:POOL_HALO - k + ts, cols]
        count = jnp.minimum(pos + 1, w).astype(F32)
        d = (wsum / count - zg).astype(BF16)
        y = jnp.dot(d, wpool_ref[g], preferred_element_type=F32) * pscale_ref[:, cols]
        o_ref[:, cols] = y.astype(o_ref.dtype)

    c = za_ref[...].astype(F32) * jax.nn.sigmoid(zb_ref[...].astype(F32))
    ch = zah_ref[...].astype(F32) * jax.nn.sigmoid(zbh_ref[...].astype(F32))
    cbuf_ref[0:CONV_HALO, :] = jnp.where(first, jnp.zeros_like(ch), ch)
    cbuf_ref[CONV_HALO:CONV_HALO + ts, :] = c
    acc = jnp.zeros_like(c) + dwb_ref[...]
    base = CONV_HALO - (CONV_KERNEL - 1)
    sub = 8
    for b in range(sub):
        offs = [o for o in range(base, base + CONV_KERNEL) if o % sub == b]
        if not offs:
            continue
        span = offs[-1] - offs[0] + ts
        sbuf_ref[0:span, :] = cbuf_ref[offs[0]:offs[0] + span, :]
        for o in offs:
            k = o - base
            acc = acc + dw_ref[k:k + 1, :] * sbuf_ref[o - offs[0]:o - offs[0] + ts, :]
    mu = jnp.mean(acc, axis=-1, keepdims=True)
    xc = acc - mu
    yn = xc * lax.rsqrt(jnp.mean(xc * xc, axis=-1, keepdims=True) + LN_EPS) * lng_ref[...] + lnb_ref[...]
    sw = yn * jax.nn.sigmoid(yn)
    out = jnp.dot(sw.astype(BF16), wpw_ref[...], preferred_element_type=F32)
    o_ref[:, pool_w:] = out.astype(o_ref.dtype)


def _local_mixers(z, w_pool, pool_scale, dw, dw_b, ln_g, ln_b, w_pw, seq, pool_col, ts):
    t = z.shape[0]
    pool_w = pool_scale.shape[0]
    conv_w = dw.shape[1]
    assert pool_w == conv_w and pool_col % pool_w == 0
    pc = pool_col // pool_w
    hp = ts // POOL_HALO
    hc = ts // CONV_HALO
    kern = functools.partial(_local_kernel, ts=ts, tiles_per_seq=seq // ts)
    row = lambda shape: _resident(shape, lambda i: (0,) * len(shape))
    return pl.pallas_call(
        kern,
        grid=(t // ts,),
        in_specs=[pl.BlockSpec((ts, pool_w), lambda i: (i, pc)),
                  pl.BlockSpec((POOL_HALO, pool_w), lambda i: (jnp.maximum(i * hp - 1, 0), pc)),
                  pl.BlockSpec((ts, conv_w), lambda i: (i, pc + 1)),
                  pl.BlockSpec((CONV_HALO, conv_w), lambda i: (jnp.maximum(i * hc - 1, 0), pc + 1)),
                  pl.BlockSpec((ts, conv_w), lambda i: (i, pc + 2)),
                  pl.BlockSpec((CONV_HALO, conv_w), lambda i: (jnp.maximum(i * hc - 1, 0), pc + 2)),
                  row(w_pool.shape), row((1, pool_w)), row(dw.shape), row((1, conv_w)),
                  row((1, conv_w)), row((1, conv_w)), row(w_pw.shape)],
        out_specs=pl.BlockSpec((ts, pool_w + conv_w), lambda i: (i, 0)),
        out_shape=jax.ShapeDtypeStruct((t, pool_w + conv_w), BF16),
        scratch_shapes=[pltpu.VMEM((POOL_HALO + ts, pool_w), F32),
                        pltpu.VMEM((CONV_HALO + ts, conv_w), F32),
                        pltpu.VMEM((CONV_HALO + ts, conv_w), F32)],
        compiler_params=_cparams(("parallel",)),
        name="local_mixers",
    )(z, z, z, z, z, z, w_pool, pool_scale.reshape(1, pool_w), dw, dw_b.reshape(1, conv_w),
      ln_g.reshape(1, conv_w), ln_b.reshape(1, conv_w), w_pw)


def _top2_route(u, w_router_padded):
    u_hi = u.astype(BF16)
    u_lo = (u - u_hi.astype(F32)).astype(BF16)
    w_hi = w_router_padded.astype(BF16)
    w_lo = (w_router_padded - w_hi.astype(F32)).astype(BF16)
    logits = (jnp.dot(u_hi, w_hi, preferred_element_type=F32)
              + (jnp.dot(u_hi, w_lo, preferred_element_type=F32)
                 + jnp.dot(u_lo, w_hi, preferred_element_type=F32)))
    lane = lax.broadcasted_iota(jnp.int32, logits.shape, 1)
    neg = jnp.full(logits.shape, -jnp.inf, F32)
    lg = jnp.where(lane < N_EXPERTS, logits, neg)
    m1 = jnp.max(lg, axis=-1, keepdims=True)
    i1 = jnp.min(jnp.where(lg == m1, lane, LANES), axis=-1, keepdims=True)
    lg2 = jnp.where(lane == i1, neg, lg)
    m2 = jnp.max(lg2, axis=-1, keepdims=True)
    i2 = jnp.min(jnp.where(lg2 == m2, lane, LANES), axis=-1, keepdims=True)
    e2 = jnp.exp(m2 - m1)
    g1 = 1.0 / (1.0 + e2)
    g2 = e2 / (1.0 + e2)
    return jnp.where(lane == 0, g1, jnp.where(lane == 1, g2, jnp.where(
        lane == 2, i1.astype(F32), jnp.where(lane == 3, i2.astype(F32), 0.0))))


def _router_kernel(u_ref, w_ref, o_ref):
    o_ref[...] = _top2_route(u_ref[...], w_ref[...])


def _router(u2, w_router, tm):
    t, d = u2.shape
    w = jnp.zeros((d, LANES), F32).at[:, :N_EXPERTS].set(w_router)
    return pl.pallas_call(
        _router_kernel,
        grid=(t // tm,),
        in_specs=[pl.BlockSpec((tm, d), lambda i: (i, 0)),
                  _resident((d, LANES), lambda i: (0, 0))],
        out_specs=pl.BlockSpec((tm, LANES), lambda i: (i, 0)),
        out_shape=jax.ShapeDtypeStruct((t, LANES), F32),
        compiler_params=_cparams(("parallel",)),
        name="router",
    )(u2, w)


def _outproj_kernel(oa_ref, ol_ref, w_ref, h_ref, g_ref, h1_ref, u_ref):
    ka = oa_ref.shape[1]
    mix = jnp.dot(oa_ref[...], w_ref[0:ka, :], preferred_element_type=F32)
    mix = mix + jnp.dot(ol_ref[...], w_ref[ka:, :], preferred_element_type=F32)
    h1 = h_ref[...] + mix
    h1_ref[...] = h1
    u_ref[...] = _rms(h1, g_ref[...], EPS).astype(u_ref.dtype)


def _outproj(o_attn, o_local, w_out, h, gain, tm, u_dtype):
    t, d = h.shape
    ka, kl = o_attn.shape[1], o_local.shape[1]
    blk = pl.BlockSpec((tm, d), lambda i: (i, 0))
    return pl.pallas_call(
        _outproj_kernel,
        grid=(t // tm,),
        in_specs=[pl.BlockSpec((tm, ka), lambda i: (i, 0)),
                  pl.BlockSpec((tm, kl), lambda i: (i, 0)),
                  _resident((ka + kl, d), lambda i: (0, 0)),
                  blk,
                  _resident((1, d), lambda i: (0, 0))],
        out_specs=[blk, blk],
        out_shape=[jax.ShapeDtypeStruct((t, d), F32), jax.ShapeDtypeStruct((t, d), u_dtype)],
        compiler_params=_cparams(("parallel",)),
        name="outproj",
    )(o_attn, o_local, w_out, h, gain.reshape(1, d))


def _swiglu_partial(x, wg, wu, wd):
    g = jnp.dot(x, wg, preferred_element_type=F32)
    u = jnp.dot(x, wu, preferred_element_type=F32)
    hidden = (g * jax.nn.sigmoid(g)) * u
    return jnp.dot(hidden.astype(wd.dtype), wd, preferred_element_type=F32)


def _dense_ffn_kernel(*refs, n_cast):
    u_ref, wg_ref, wu_ref, wd_ref = refs[:4]
    cast_in = refs[4:4 + n_cast]
    y_ref = refs[4 + n_cast]
    cast_out = refs[5 + n_cast:5 + 2 * n_cast]
    acc_ref = refs[5 + 2 * n_cast]
    f = pl.program_id(1)

    @pl.when(f == 0)
    def _():
        acc_ref[...] = jnp.zeros(acc_ref.shape, F32)

    acc_ref[...] += _swiglu_partial(u_ref[...], wg_ref[...], wu_ref[...], wd_ref[...])

    for src, dst in zip(cast_in, cast_out):
        dst[...] = src[...].astype(dst.dtype)

    @pl.when(f == pl.num_programs(1) - 1)
    def _():
        y_ref[...] = acc_ref[...].astype(y_ref.dtype)


def _cast_slices(arrays_axes, n_steps, nf):
    specs = []
    for arr, axis in arrays_axes:
        e = arr.shape[0]
        if n_steps % e:
            return None
        per = n_steps // e
        if arr.shape[axis] % per:
            return None
        width = arr.shape[axis] // per
        align = LANES if axis == 2 else 16
        if width % align:
            return None
        block = tuple(None if a == 0 else (width if a == axis else arr.shape[a]) for a in range(3))

        def imap(i, f, per=per, axis=axis):
            s = i * nf + f
            return (s // per, s % per, 0) if axis == 1 else (s // per, 0, s % per)

        specs.append(pl.BlockSpec(block, imap))
    return specs


def _dense_ffn(u, wg, wu, wd, tm, tf, casts=()):
    t, d = u.shape
    ff = wg.shape[1]
    grid = (t // tm, ff // tf)
    cast_specs = _cast_slices(casts, grid[0] * grid[1], grid[1]) if casts else []
    if cast_specs is None:
        cast_specs, casts, unsupported = [], (), True
    else:
        unsupported = False
    n_cast = len(cast_specs)
    outs = pl.pallas_call(
        functools.partial(_dense_ffn_kernel, n_cast=n_cast),
        grid=grid,
        in_specs=[pl.BlockSpec((tm, d), lambda i, f: (i, 0)),
                  pl.BlockSpec((d, tf), lambda i, f: (0, f)),
                  pl.BlockSpec((d, tf), lambda i, f: (0, f)),
                  pl.BlockSpec((tf, d), lambda i, f: (f, 0))] + cast_specs,
        out_specs=[pl.BlockSpec((tm, d), lambda i, f: (i, 0))] + cast_specs,
        out_shape=[jax.ShapeDtypeStruct((t, d), BF16)]
        + [jax.ShapeDtypeStruct(a.shape, BF16) for a, _ in casts],
        scratch_shapes=[pltpu.VMEM((tm, d), F32)],
        compiler_params=_cparams(("parallel", "arbitrary")),
        name="dense_ffn",
    )(u, wg, wu, wd, *[a for a, _ in casts])
    return outs[0], (None if unsupported else list(outs[1:]))


def _dispatch_kernel(pend_ref, padded_ref, dest_ref, x_ref, o_ref, zero_ref, row_sem, zero_sem,
                     *, tile, n_tiles):
    rows = x_ref.shape[0]

    @pl.when(pl.program_id(0) == 0)
    def _():
        zero_ref[...] = jnp.zeros(zero_ref.shape, zero_ref.dtype)

        def zero_tile(first_row):
            return pltpu.make_async_copy(zero_ref, o_ref.at[pl.ds(first_row, tile)], zero_sem)

        def last_tile_of(e):
            return zero_tile(pl.multiple_of(pend_ref[e] - tile, tile))

        trailing = range(n_tiles - N_EXPERTS, n_tiles)
        for phase in ("start", "wait"):
            for e in range(N_EXPERTS):
                @pl.when(padded_ref[e] > 0)
                def _():
                    getattr(last_tile_of(e), phase)()
            for tl in trailing:
                @pl.when(tl * tile >= pend_ref[N_EXPERTS - 1])
                def _():
                    getattr(zero_tile(tl * tile), phase)()

    def row_copy(r, k):
        row = dest_ref[0, 0, r * TOP_K + k]
        return pltpu.make_async_copy(x_ref.at[pl.ds(r, 1)], o_ref.at[pl.ds(row, 1)], row_sem)

    def issue(r, carry):
        for k in range(TOP_K):
            row_copy(r, k).start(priority=k % 2)
        return carry

    def drain(r, carry):
        for k in range(TOP_K):
            row_copy(r, k).wait()
        return carry

    lax.fori_loop(0, rows, issue, 0, unroll=8)
    lax.fori_loop(0, rows, drain, 0, unroll=8)


def _dispatch(u2, dest, pend, padded, tile, n_tiles, tm):
    t, d = u2.shape
    dest3 = dest.reshape(t // tm, 1, TOP_K * tm)
    grid_spec = pltpu.PrefetchScalarGridSpec(
        num_scalar_prefetch=2,
        grid=(t // tm,),
        in_specs=[pl.BlockSpec((1, 1, TOP_K * tm), lambda i, pe, pa: (i, 0, 0), memory_space=pltpu.SMEM),
                  pl.BlockSpec((tm, d), lambda i, pe, pa: (i, 0))],
        out_specs=pl.BlockSpec(memory_space=pl.ANY),
        scratch_shapes=[pltpu.VMEM((tile, d), u2.dtype),
                        pltpu.SemaphoreType.DMA(()),
                        pltpu.SemaphoreType.DMA(())],
    )
    return pl.pallas_call(
        functools.partial(_dispatch_kernel, tile=tile, n_tiles=n_tiles),
        grid_spec=grid_spec,
        out_shape=jax.ShapeDtypeStruct((n_tiles * tile, d), u2.dtype),
        compiler_params=_cparams(("arbitrary",)),
        name="moe_dispatch",
    )(pend, padded, dest3, u2)


def _moe_ffn_kernel(te_ref, nu_ref, tr_ref, x_ref, wg_ref, wu_ref, wd_ref, y_ref, acc_ref, xb_ref):
    i = pl.program_id(0)
    f = pl.program_id(1)
    half = x_ref.shape[0] // 2

    @pl.when(i < nu_ref[0])
    def _():
        @pl.when(f == 0)
        def _():
            acc_ref[...] = jnp.zeros(acc_ref.shape, F32)
            xb_ref[...] = x_ref[...].astype(xb_ref.dtype)

        @pl.when(tr_ref[i] > half)
        def _():
            acc_ref[...] += _swiglu_partial(xb_ref[...], wg_ref[...], wu_ref[...], wd_ref[...])

        @pl.when(tr_ref[i] <= half)
        def _():
            acc_ref[0:half, :] += _swiglu_partial(xb_ref[0:half, :], wg_ref[...], wu_ref[...], wd_ref[...])

        @pl.when(f == pl.num_programs(1) - 1)
        def _():
            y_ref[...] = acc_ref[...].astype(y_ref.dtype)

    @pl.when(i >= nu_ref[0])
    def _():
        y_ref[...] = jnp.zeros(y_ref.shape, y_ref.dtype)


def _moe_ffn(x_sorted, tile_expert, n_used, tile_rows, wg, wu, wd, tm, tf):
    p, d = x_sorted.shape
    ff = wg.shape[2]
    nf = ff // tf

    def row(i, f, te, nu, tr):
        return (jnp.minimum(i, nu[0] - 1), 0)

    def fcol(i, f, nu):
        return jnp.where(i < nu[0], f, nf - 1)

    grid_spec = pltpu.PrefetchScalarGridSpec(
        num_scalar_prefetch=3,
        grid=(p // tm, nf),
        in_specs=[pl.BlockSpec((tm, d), row),
                  pl.BlockSpec((None, d, tf), lambda i, f, te, nu, tr: (te[i], 0, fcol(i, f, nu))),
                  pl.BlockSpec((None, d, tf), lambda i, f, te, nu, tr: (te[i], 0, fcol(i, f, nu))),
                  pl.BlockSpec((None, tf, d), lambda i, f, te, nu, tr: (te[i], fcol(i, f, nu), 0))],
        out_specs=pl.BlockSpec((tm, d), lambda i, f, te, nu, tr: (i, 0)),
        scratch_shapes=[pltpu.VMEM((tm, d), F32), pltpu.VMEM((tm, d), wg.dtype)],
    )
    return pl.pallas_call(
        _moe_ffn_kernel,
        grid_spec=grid_spec,
        out_shape=jax.ShapeDtypeStruct((p, d), BF16),
        compiler_params=_cparams(("arbitrary", "arbitrary")),
        name="moe_ffn",
    )(tile_expert, n_used, tile_rows, x_sorted, wg, wu, wd)


def _moe(u2, route, wg, wu, wd, tm_route, tm, tf):
    t, d = u2.shape
    expert = route[:, 2:2 + TOP_K].astype(jnp.int32).reshape(-1)
    onehot = (expert[:, None] == jnp.arange(N_EXPERTS, dtype=jnp.int32)[None, :]).astype(jnp.int32)
    csum = jnp.cumsum(onehot, axis=0)
    rank = jnp.sum(csum * onehot, axis=1) - 1
    counts = csum[-1]
    padded = ((counts + tm - 1) // tm) * tm
    pend = jnp.cumsum(padded)
    pstart = pend - padded
    dest = pstart[expert] + rank
    n_tiles = (TOP_K * t) // tm + N_EXPERTS
    tile_start = jnp.arange(n_tiles, dtype=jnp.int32) * tm
    tile_expert = jnp.minimum(jnp.sum((tile_start[:, None] >= pend[None, :]).astype(jnp.int32), axis=1),
                              N_EXPERTS - 1).astype(jnp.int32)
    n_used = (pend[-1] // tm).astype(jnp.int32).reshape(1)
    x_sorted = _dispatch(u2, dest.astype(jnp.int32), pend.astype(jnp.int32), padded.astype(jnp.int32),
                         tm, n_tiles, tm_route)
    tile_rows = jnp.clip(counts[tile_expert] - (tile_start - pstart[tile_expert]), 0, tm).astype(jnp.int32)
    y_sorted = _moe_ffn(x_sorted, tile_expert, n_used, tile_rows, wg, wu, wd, tm, tf)
    return y_sorted.at[dest].get(mode="promise_in_bounds").reshape(t, TOP_K * d)


def _ple_kernel(*refs, n_y, gated, final):
    h_ref, y_refs = refs[0], refs[1:1 + n_y]
    rest = refs[1 + n_y:]
    if gated:
        r = rest[0][...]
        rest = rest[1:]
    gp_ref, wg_ref, p_ref, wp_ref, gn_ref = rest[:5]
    out_refs = rest[5:]
    h2 = h_ref[...]
    for k, y_ref in enumerate(y_refs):
        y = y_ref[...].astype(F32)
        h2 = h2 + (r[:, k:k + 1] * y if gated else y)
    u3 = _rms(h2, gp_ref[...], EPS).astype(BF16)
    gate = jax.nn.sigmoid(jnp.dot(u3, wg_ref[...], preferred_element_type=F32))
    emb = jnp.dot(p_ref[...].astype(BF16), wp_ref[...], preferred_element_type=F32)
    h3 = h2 + emb * gate
    if final:
        out_refs[0][...] = _rms(h3, gn_ref[...], EPS)
    else:
        out_refs[0][...] = h3
        out_refs[1][...] = _rms(h3, gn_ref[...], EPS).astype(out_refs[1].dtype)


def _ple(h1, y, route, ple_gain, w_gate, p, w_proj, next_gain, final, tm):
    t, d = h1.shape
    pd = p.shape[1]
    n_y = y.shape[1] // d
    blk = pl.BlockSpec((tm, d), lambda i: (i, 0))
    vec = _resident((1, d), lambda i: (0, 0))
    gated = route is not None
    in_specs = [blk] + [pl.BlockSpec((tm, d), lambda i, k=k: (i, k)) for k in range(n_y)]
    args = [h1] + [y] * n_y
    if gated:
        in_specs.append(pl.BlockSpec((tm, LANES), lambda i: (i, 0)))
        args.append(route)
    in_specs += [vec, _resident((d, d), lambda i: (0, 0)), pl.BlockSpec((tm, pd), lambda i: (i, 0)),
                 _resident((pd, d), lambda i: (0, 0)), vec]
    args += [ple_gain.reshape(1, d), w_gate, p, w_proj, next_gain.reshape(1, d)]
    if final:
        out_specs = [blk]
        out_shape = [jax.ShapeDtypeStruct((t, d), F32)]
    else:
        out_specs = [blk, blk]
        out_shape = [jax.ShapeDtypeStruct((t, d), F32), jax.ShapeDtypeStruct((t, d), BF16)]
    return pl.pallas_call(
        functools.partial(_ple_kernel, n_y=n_y, gated=gated, final=final),
        grid=(t // tm,),
        in_specs=in_specs,
        out_specs=out_specs,
        out_shape=out_shape,
        compiler_params=_cparams(("parallel",)),
        name="ple",
    )(*args)


def _tiles(t, seq, ff):
    return dict(
        tm_norm=min(512, t),
        tm_in=min(2048, t), tn_in=512,
        tq=min(512, seq), tk=min(512, seq), heads_per_step=2,
        ts=min(512, seq),
        tm_out=min(512, t),
        tm_ffn=min(1024, t), tf_ffn=min(512, ff),
        tm_moe=min(512, t), tf_moe=min(1024, ff),
        tm_ple=min(512, t),
    )


def _rotary_tables(positions):
    inv_freq = ROPE_THETA ** (-jnp.arange(0, ROPE_DIM, 2, dtype=F32) / ROPE_DIM)
    ang = positions.astype(F32).reshape(-1, 1) * inv_freq
    cos, sin = lax.optimization_barrier((jnp.cos(ang), jnp.sin(ang)))
    cos_rep = jnp.tile(cos, (1, LANES // ROPE_HALF))
    sin_rep = jnp.tile(sin, (1, LANES // ROPE_HALF))
    dim128 = (jnp.arange(LANES, dtype=jnp.int32) % HEAD_DIM)[None, :]
    c = jnp.where(dim128 < ROPE_DIM, cos_rep, 1.0)
    s = jnp.where(dim128 < ROPE_HALF, -sin_rep, jnp.where(dim128 < ROPE_DIM, sin_rep, 0.0))
    lane = jnp.arange(2 * LANES, dtype=jnp.int32)
    dim = lane % HEAD_DIM
    src = jnp.where(dim < ROPE_HALF, lane + ROPE_HALF, jnp.where(dim < ROPE_DIM, lane - ROPE_HALF, lane))
    swap = (lane[:, None] == src[None, :]).astype(BF16)
    return c, s, swap


def kernel(x, p, positions, mix_norm, w_in, lambda_q1, lambda_k1, lambda_q2, lambda_k2, subln_gain, w_pool, pool_scale, conv_dw, conv_dw_bias, conv_ln_gain, conv_ln_bias, w_conv_pw, w_out, ffn_norm, w_dense_gate, w_dense_up, w_dense_down, w_router, w_exp_gate, w_exp_up, w_exp_down, ple_norm, w_ple_gate, w_ple_proj, final_norm):
    batch, seq, d = x.shape
    depth = w_in.shape[0]
    t = batch * seq
    attn_w = (w_in.shape[2] - 2 * conv_dw.shape[2] - pool_scale.shape[1]) // 3
    n_heads = attn_w // V_DIM
    qk_cols = attn_w
    pool_col = 3 * attn_w
    ff = w_dense_gate.shape[2]
    ts = _tiles(t, seq, ff)

    rot_c, rot_s, rot_swap = _rotary_tables(positions)
    h = x.reshape(t, d)
    u = _rmsnorm(h, mix_norm[0], ts["tm_norm"])
    out = None
    experts_bf16 = None
    for i in range(depth):
        lam_init = 0.8 - 0.6 * math.exp(-0.3 * i)
        final = i == depth - 1
        z = _inproj(u, w_in, i, rot_c, rot_s, rot_swap, qk_cols, ts["tm_in"], ts["tn_in"])
        lam_params = jnp.stack([lambda_q1[i], lambda_k1[i], lambda_q2[i], lambda_k2[i]]).astype(F32)
        o_attn = _attention(z, lam_params, subln_gain[i], batch, seq, n_heads, lam_init, ts["tq"],
                            ts["tk"], ts["heads_per_step"])
        o_local = _local_mixers(z, w_pool[i].astype(BF16), pool_scale[i], conv_dw[i], conv_dw_bias[i],
                                conv_ln_gain[i], conv_ln_bias[i], w_conv_pw[i].astype(BF16),
                                seq, pool_col, ts["ts"])
        j = i // 2
        h1, u2 = _outproj(o_attn, o_local, w_out[i].astype(BF16), h, ffn_norm[i], ts["tm_out"],
                          BF16 if i % 2 == 0 else F32)
        if i % 2 == 0:
            casts = ()
            if i + 1 < depth:
                jn = (i + 1) // 2
                casts = ((w_exp_gate[jn], 2), (w_exp_up[jn], 2), (w_exp_down[jn], 1))
            y, experts_bf16 = _dense_ffn(u2, w_dense_gate[j].astype(BF16), w_dense_up[j].astype(BF16),
                                         w_dense_down[j].astype(BF16), ts["tm_ffn"], ts["tf_ffn"], casts)
            route = None
        else:
            if experts_bf16 is None:
                experts_bf16 = [w.astype(BF16) for w in (w_exp_gate[j], w_exp_up[j], w_exp_down[j])]
            route = _router(u2, w_router[j], ts["tm_out"])
            y = _moe(u2, route, *experts_bf16, ts["tm_out"], ts["tm_moe"], ts["tf_moe"])
            experts_bf16 = None
        next_gain = final_norm if final else mix_norm[i + 1]
        res = _ple(h1, y, route, ple_norm[i], w_ple_gate[i].astype(BF16), p[i].reshape(t, -1),
                   w_ple_proj[i].astype(BF16), next_gain, final, ts["tm_ple"])
        if final:
            out = res[0]
        else:
            h, u = res
    return out.reshape(batch, seq, d)
```

```python
import functools
import math

import jax
import jax.numpy as jnp
from jax import lax
from jax.experimental import pallas as pl
from jax.experimental.pallas import tpu as pltpu

F32 = jnp.float32
BF16 = jnp.bfloat16

EPS = 1e-6
SUBLN_EPS = 1e-5
LN_EPS = 1e-5
HEAD_DIM = 64
V_DIM = 2 * HEAD_DIM
ROPE_DIM = HEAD_DIM // 4
ROPE_HALF = ROPE_DIM // 2
ROPE_THETA = 500000.0
LOG2_E = math.log2(math.e)
POOL_WINDOWS = (2, 4, 8, 16)
POOL_GROUP_DIM = 128
CONV_KERNEL = 31
N_EXPERTS = 8
TOP_K = 2

LANES = 128
V7X_VMEM_BYTES = 64 * 1024 * 1024
VMEM_LIMIT = 62 * 1024 * 1024
POOL_HALO = 16
CONV_HALO = 32


def _cparams(sem):
    return pltpu.CompilerParams(dimension_semantics=sem, vmem_limit_bytes=VMEM_LIMIT)


def _resident(shape, index_map):
    return pl.BlockSpec(shape, index_map, pipeline_mode=pl.Buffered(1))


def _rms(x, gain, eps):
    return x * lax.rsqrt(jnp.mean(x * x, axis=-1, keepdims=True) + eps) * gain


def _rmsnorm_kernel(x_ref, g_ref, o_ref):
    o_ref[...] = _rms(x_ref[...], g_ref[...], EPS).astype(o_ref.dtype)


def _rmsnorm(x, gain, tm):
    t, d = x.shape
    return pl.pallas_call(
        _rmsnorm_kernel,
        grid=(t // tm,),
        in_specs=[pl.BlockSpec((tm, d), lambda i: (i, 0)), pl.BlockSpec((1, d), lambda i: (0, 0))],
        out_specs=pl.BlockSpec((tm, d), lambda i: (i, 0)),
        out_shape=jax.ShapeDtypeStruct((t, d), BF16),
        compiler_params=_cparams(("parallel",)),
        name="rmsnorm",
    )(x, gain.reshape(1, d))


def _inproj_kernel(u_ref, w_ref, c_ref, s_ref, swap_ref, z_ref, *, n_rot_blocks, n_q_blocks, q_scale):
    j = pl.program_id(1)
    u = u_ref[...]
    acc = jnp.dot(u, w_ref[...].astype(u.dtype), preferred_element_type=F32)
    tn = acc.shape[1]

    @pl.when(j < n_rot_blocks)
    def _():
        reps = tn // LANES
        c = jnp.concatenate([c_ref[...]] * reps, axis=1)
        s = jnp.concatenate([s_ref[...]] * reps, axis=1)
        pw = swap_ref.shape[0]
        ab = acc.astype(swap_ref.dtype)
        partner = jnp.concatenate(
            [jnp.dot(ab[:, b * pw:(b + 1) * pw], swap_ref[...], preferred_element_type=F32)
             for b in range(tn // pw)], axis=1)
        r = acc * c + partner * s
        scale = jnp.where(j < n_q_blocks, q_scale, 1.0).astype(F32)
        z_ref[...] = (r * scale).astype(z_ref.dtype)

    @pl.when(j >= n_rot_blocks)
    def _():
        z_ref[...] = acc.astype(z_ref.dtype)


def _inproj(u, w_all, layer, rot_c, rot_s, swap, qk_cols, tm, tn):
    t, d = u.shape
    n = w_all.shape[2]
    kern = functools.partial(_inproj_kernel, n_rot_blocks=2 * qk_cols // tn, n_q_blocks=qk_cols // tn,
                             q_scale=HEAD_DIM ** -0.5 * LOG2_E)
    tab = pl.BlockSpec((tm, LANES), lambda i, j: (i, 0))
    return pl.pallas_call(
        kern,
        grid=(t // tm, n // tn),
        in_specs=[pl.BlockSpec((tm, d), lambda i, j: (i, 0)),
                  pl.BlockSpec((None, d, tn), lambda i, j: (layer, 0, j)),
                  tab, tab, _resident(swap.shape, lambda i, j: (0, 0))],
        out_specs=pl.BlockSpec((tm, tn), lambda i, j: (i, j)),
        out_shape=jax.ShapeDtypeStruct((t, n), BF16),
        compiler_params=_cparams(("parallel", "arbitrary")),
        name="inproj",
    )(u, w_all, rot_c, rot_s, swap)


def _attn_kernel(lam_ref, gain_ref, q_ref, k_ref, v_ref, o_ref, qs_ref, vx_ref, m_ref, acc_ref, sa_ref,
                 sb_ref, *, tq, tk, hp, lam_init):
    qi = pl.program_id(2)
    per = tk // tq

    @pl.when(qi == 0)
    def _():
        for a in range(hp):
            vx_ref[a, :, 0:V_DIM] = v_ref[:, a * V_DIM:(a + 1) * V_DIM]
            vx_ref[a, :, V_DIM:2 * V_DIM] = jnp.ones((v_ref.shape[0], V_DIM), v_ref.dtype)

    lane = lax.broadcasted_iota(jnp.int32, (tq, LANES), 1)
    for a in range(hp):
        q = q_ref[:, a * LANES:(a + 1) * LANES]
        zero = jnp.zeros_like(q)
        qs_ref[a, 0:tq, :] = jnp.where(lane < HEAD_DIM, q, zero)
        qs_ref[a, tq:2 * tq, :] = jnp.where(lane >= HEAD_DIM, q, zero)
    m_ref[...] = jnp.full(m_ref.shape, -jnp.inf, F32)
    acc_ref[...] = jnp.zeros(acc_ref.shape, F32)

    def scores(j, s_ref):
        start = pl.multiple_of(j * tk, tk)
        for a in range(hp):
            k = k_ref[pl.ds(start, tk), a * LANES:(a + 1) * LANES]
            s_ref[a] = lax.dot_general(qs_ref[a], k, (((1,), (1,)), ((), ())),
                                       preferred_element_type=F32)

    def update(j, s_ref, masked):
        start = pl.multiple_of(j * tk, tk)
        for a in range(hp):
            vx = vx_ref[a, pl.ds(start, tk), :]
            s = s_ref[a]
            if masked:
                row = lax.broadcasted_iota(jnp.int32, s.shape, 0)
                col = lax.broadcasted_iota(jnp.int32, s.shape, 1)
                s = jnp.where(col <= (row & (tq - 1)) + (qi % per) * tq, s, -jnp.inf)
            m_old = m_ref[a]
            m_new = jnp.maximum(m_old, jnp.max(s, axis=-1, keepdims=True))
            alpha = jnp.exp2(m_old - m_new)
            p = jnp.exp2(s - jnp.concatenate([m_new] * (tk // LANES), axis=1))
            pv = jnp.dot(p.astype(vx.dtype), vx, preferred_element_type=F32)
            acc_ref[a] = jnp.concatenate([alpha, alpha], axis=1) * acc_ref[a] + pv
            m_ref[a] = m_new

    n_full = qi // per
    scores(0, sa_ref)

    def body(j, carry):
        def even():
            scores(j + 1, sb_ref)
            update(j, sa_ref, False)

        def odd():
            scores(j + 1, sa_ref)
            update(j, sb_ref, False)

        lax.cond((j & 1) == 0, even, odd)
        return carry

    lax.fori_loop(0, n_full, body, 0)
    lax.cond((n_full & 1) == 0, lambda: update(n_full, sa_ref, True), lambda: update(n_full, sb_ref, True))

    lp = lam_ref[...]
    lam = (jnp.exp(jnp.sum(lp[0:1] * lp[1:2], axis=-1, keepdims=True))
           - jnp.exp(jnp.sum(lp[2:3] * lp[3:4], axis=-1, keepdims=True)) + lam_init)
    for a in range(hp):
        o_all = acc_ref[a, :, 0:V_DIM] / acc_ref[a, :, V_DIM:2 * V_DIM]
        o = o_all[0:tq] - lam * o_all[tq:2 * tq]
        y = _rms(o, gain_ref[...], SUBLN_EPS) * (1.0 - lam_init)
        o_ref[:, a * V_DIM:(a + 1) * V_DIM] = y.astype(o_ref.dtype)


def _attention(z, lam_params, subln_gain, batch, seq, n_heads, lam_init, tq, tk, hp):
    t = z.shape[0]
    nq = seq // tq
    assert tq & (tq - 1) == 0 and tk % tq == 0 and seq % tk == 0 and n_heads % hp == 0
    kern = functools.partial(_attn_kernel, tq=tq, tk=tk, hp=hp, lam_init=lam_init)
    ng = n_heads // hp
    w = hp * LANES
    return pl.pallas_call(
        kern,
        grid=(batch, ng, nq),
        in_specs=[pl.BlockSpec(lam_params.shape, lambda b, h, i: (0, 0)),
                  pl.BlockSpec((1, V_DIM), lambda b, h, i: (0, 0)),
                  pl.BlockSpec((tq, w), lambda b, h, i: (b * nq + i, h)),
                  pl.BlockSpec((seq, w), lambda b, h, i: (b, ng + h)),
                  pl.BlockSpec((seq, w), lambda b, h, i: (b, 2 * ng + h))],
        out_specs=pl.BlockSpec((tq, w), lambda b, h, i: (b * nq + i, h)),
        out_shape=jax.ShapeDtypeStruct((t, n_heads * V_DIM), BF16),
        scratch_shapes=[pltpu.VMEM((hp, 2 * tq, LANES), BF16),
                        pltpu.VMEM((hp, seq, 2 * V_DIM), BF16),
                        pltpu.VMEM((hp, 2 * tq, LANES), F32),
                        pltpu.VMEM((hp, 2 * tq, 2 * V_DIM), F32),
                        pltpu.VMEM((hp, 2 * tq, tk), F32),
                        pltpu.VMEM((hp, 2 * tq, tk), F32)],
        compiler_params=_cparams(("parallel", "parallel", "arbitrary")),
        name="diff_attention",
    )(lam_params, subln_gain.reshape(1, V_DIM), z, z, z)


def _local_kernel(zp_ref, zph_ref, za_ref, zah_ref, zb_ref, zbh_ref, wpool_ref, pscale_ref,
                  dw_ref, dwb_ref, lng_ref, lnb_ref, wpw_ref, o_ref, pbuf_ref, cbuf_ref, sbuf_ref,
                  *, ts, tiles_per_seq):
    i = pl.program_id(0)
    tile_in_seq = i % tiles_per_seq
    first = tile_in_seq == 0
    pool_w = zp_ref.shape[1]

    zp = zp_ref[...].astype(F32)
    halo = zph_ref[...].astype(F32)
    pbuf_ref[0:POOL_HALO, :] = jnp.where(first, jnp.zeros_like(halo), halo)
    pbuf_ref[POOL_HALO:POOL_HALO + ts, :] = zp
    pos = tile_in_seq * ts + lax.broadcasted_iota(jnp.int32, (ts, 1), 0)
    for g, w in enumerate(POOL_WINDOWS):
        cols = slice(g * POOL_GROUP_DIM, (g + 1) * POOL_GROUP_DIM)
        zg = zp[:, cols]
        wsum = zg
        for k in range(1, w):
            wsum = wsum + pbuf_ref[POOL_HALO - k:POOL_HALO - k + ts, cols]
        count = jnp.minimum(pos + 1, w).astype(F32)
        d = (wsum / count - zg).astype(BF16)
        y = jnp.dot(d, wpool_ref[g], preferred_element_type=F32) * pscale_ref[:, cols]
        o_ref[:, cols] = y.astype(o_ref.dtype)

    c = za_ref[...].astype(F32) * jax.nn.sigmoid(zb_ref[...].astype(F32))
    ch = zah_ref[...].astype(F32) * jax.nn.sigmoid(zbh_ref[...].astype(F32))
    cbuf_ref[0:CONV_HALO, :] = jnp.where(first, jnp.zeros_like(ch), ch)
    cbuf_ref[CONV_HALO:CONV_HALO + ts, :] = c
    acc = jnp.zeros_like(c) + dwb_ref[...]
    base = CONV_HALO - (CONV_KERNEL - 1)
    sub = 8
    for b in range(sub):
        offs = [o for o in range(base, base + CONV_KERNEL) if o % sub == b]
        if not offs:
            continue
        span = offs[-1] - offs[0] + ts
        sbuf_ref[0:span, :] = cbuf_ref[offs[0]:offs[0] + span, :]
        for o in offs:
            k = o - base
            acc = acc + dw_ref[k:k + 1, :] * sbuf_ref[o - offs[0]:o - offs[0] + ts, :]
    mu = jnp.mean(acc, axis=-1, keepdims=True)
    xc = acc - mu
    yn = xc * lax.rsqrt(jnp.mean(xc * xc, axis=-1, keepdims=True) + LN_EPS) * lng_ref[...] + lnb_ref[...]
    sw = yn * jax.nn.sigmoid(yn)
    out = jnp.dot(sw.astype(BF16), wpw_ref[...], preferred_element_type=F32)
    o_ref[:, pool_w:] = out.astype(o_ref.dtype)


def _local_mixers(z, w_pool, pool_scale, dw, dw_b, ln_g, ln_b, w_pw, seq, pool_col, ts):
    t = z.shape[0]
    pool_w = pool_scale.shape[0]
    conv_w = dw.shape[1]
    assert pool_w == conv_w and pool_col % pool_w == 0
    pc = pool_col // pool_w
    hp = ts // POOL_HALO
    hc = ts // CONV_HALO
    kern = functools.partial(_local_kernel, ts=ts, tiles_per_seq=seq // ts)
    row = lambda shape: _resident(shape, lambda i: (0,) * len(shape))
    return pl.pallas_call(
        kern,
        grid=(t // ts,),
        in_specs=[pl.BlockSpec((ts, pool_w), lambda i: (i, pc)),
                  pl.BlockSpec((POOL_HALO, pool_w), lambda i: (jnp.maximum(i * hp - 1, 0), pc)),
                  pl.BlockSpec((ts, conv_w), lambda i: (i, pc + 1)),
                  pl.BlockSpec((CONV_HALO, conv_w), lambda i: (jnp.maximum(i * hc - 1, 0), pc + 1)),
                  pl.BlockSpec((ts, conv_w), lambda i: (i, pc + 2)),
                  pl.BlockSpec((CONV_HALO, conv_w), lambda i: (jnp.maximum(i * hc - 1, 0), pc + 2)),
                  row(w_pool.shape), row((1, pool_w)), row(dw.shape), row((1, conv_w)),
                  row((1, conv_w)), row((1, conv_w)), row(w_pw.shape)],
        out_specs=pl.BlockSpec((ts, pool_w + conv_w), lambda i: (i, 0)),
        out_shape=jax.ShapeDtypeStruct((t, pool_w + conv_w), BF16),
        scratch_shapes=[pltpu.VMEM((POOL_HALO + ts, pool_w), F32),
                        pltpu.VMEM((CONV_HALO + ts, conv_w), F32),
                        pltpu.VMEM((CONV_HALO + ts, conv_w), F32)],
        compiler_params=_cparams(("parallel",)),
        name="local_mixers",
    )(z, z, z, z, z, z, w_pool, pool_scale.reshape(1, pool_w), dw, dw_b.reshape(1, conv_w),
      ln_g.reshape(1, conv_w), ln_b.reshape(1, conv_w), w_pw)


def _top2_route(u, w_router_padded):
    u_hi = u.astype(BF16)
    u_lo = (u - u_hi.astype(F32)).astype(BF16)
    w_hi = w_router_padded.astype(BF16)
    w_lo = (w_router_padded - w_hi.astype(F32)).astype(BF16)
    logits = (jnp.dot(u_hi, w_hi, preferred_element_type=F32)
              + (jnp.dot(u_hi, w_lo, preferred_element_type=F32)
                 + jnp.dot(u_lo, w_hi, preferred_element_type=F32)))
    lane = lax.broadcasted_iota(jnp.int32, logits.shape, 1)
    neg = jnp.full(logits.shape, -jnp.inf, F32)
    lg = jnp.where(lane < N_EXPERTS, logits, neg)
    m1 = jnp.max(lg, axis=-1, keepdims=True)
    i1 = jnp.min(jnp.where(lg == m1, lane, LANES), axis=-1, keepdims=True)
    lg2 = jnp.where(lane == i1, neg, lg)
    m2 = jnp.max(lg2, axis=-1, keepdims=True)
    i2 = jnp.min(jnp.where(lg2 == m2, lane, LANES), axis=-1, keepdims=True)
    e2 = jnp.exp(m2 - m1)
    g1 = 1.0 / (1.0 + e2)
    g2 = e2 / (1.0 + e2)
    return jnp.where(lane == 0, g1, jnp.where(lane == 1, g2, jnp.where(
        lane == 2, i1.astype(F32), jnp.where(lane == 3, i2.astype(F32), 0.0))))


def _router_kernel(u_ref, w_ref, o_ref):
    o_ref[...] = _top2_route(u_ref[...], w_ref[...])


def _router(u2, w_router, tm):
    t, d = u2.shape
    w = jnp.zeros((d, LANES), F32).at[:, :N_EXPERTS].set(w_router)
    return pl.pallas_call(
        _router_kernel,
        grid=(t // tm,),
        in_specs=[pl.BlockSpec((tm, d), lambda i: (i, 0)),
                  _resident((d, LANES), lambda i: (0, 0))],
        out_specs=pl.BlockSpec((tm, LANES), lambda i: (i, 0)),
        out_shape=jax.ShapeDtypeStruct((t, LANES), F32),
        compiler_params=_cparams(("parallel",)),
        name="router",
    )(u2, w)


def _outproj_kernel(oa_ref, ol_ref, w_ref, h_ref, g_ref, h1_ref, u_ref):
    ka = oa_ref.shape[1]
    mix = jnp.dot(oa_ref[...], w_ref[0:ka, :], preferred_element_type=F32)
    mix = mix + jnp.dot(ol_ref[...], w_ref[ka:, :], preferred_element_type=F32)
    h1 = h_ref[...] + mix
    h1_ref[...] = h1
    u_ref[...] = _rms(h1, g_ref[...], EPS).astype(u_ref.dtype)


def _outproj(o_attn, o_local, w_out, h, gain, tm, u_dtype):
    t, d = h.shape
    ka, kl = o_attn.shape[1], o_local.shape[1]
    blk = pl.BlockSpec((tm, d), lambda i: (i, 0))
    return pl.pallas_call(
        _outproj_kernel,
        grid=(t // tm,),
        in_specs=[pl.BlockSpec((tm, ka), lambda i: (i, 0)),
                  pl.BlockSpec((tm, kl), lambda i: (i, 0)),
                  _resident((ka + kl, d), lambda i: (0, 0)),
                  blk,
                  _resident((1, d), lambda i: (0, 0))],
        out_specs=[blk, blk],
        out_shape=[jax.ShapeDtypeStruct((t, d), F32), jax.ShapeDtypeStruct((t, d), u_dtype)],
        compiler_params=_cparams(("parallel",)),
        name="outproj",
    )(o_attn, o_local, w_out, h, gain.reshape(1, d))


def _swiglu_partial(x, wg, wu, wd):
    g = jnp.dot(x, wg, preferred_element_type=F32)
    u = jnp.dot(x, wu, preferred_element_type=F32)
    hidden = (g * jax.nn.sigmoid(g)) * u
    return jnp.dot(hidden.astype(wd.dtype), wd, preferred_element_type=F32)


def _dense_ffn_kernel(*refs, n_cast):
    u_ref, wg_ref, wu_ref, wd_ref = refs[:4]
    cast_in = refs[4:4 + n_cast]
    y_ref = refs[4 + n_cast]
    cast_out = refs[5 + n_cast:5 + 2 * n_cast]
    acc_ref = refs[5 + 2 * n_cast]
    f = pl.program_id(1)

    @pl.when(f == 0)
    def _():
        acc_ref[...] = jnp.zeros(acc_ref.shape, F32)

    acc_ref[...] += _swiglu_partial(u_ref[...], wg_ref[...], wu_ref[...], wd_ref[...])

    for src, dst in zip(cast_in, cast_out):
        dst[...] = src[...].astype(dst.dtype)

    @pl.when(f == pl.num_programs(1) - 1)
    def _():
        y_ref[...] = acc_ref[...].astype(y_ref.dtype)


def _cast_slices(arrays_axes, n_steps, nf):
    specs = []
    for arr, axis in arrays_axes:
        e = arr.shape[0]
        if n_steps % e:
            return None
        per = n_steps // e
        if arr.shape[axis] % per:
            return None
        width = arr.shape[axis] // per
        align = LANES if axis == 2 else 16
        if width % align:
            return None
        block = tuple(None if a == 0 else (width if a == axis else arr.shape[a]) for a in range(3))

        def imap(i, f, per=per, axis=axis):
            s = i * nf + f
            return (s // per, s % per, 0) if axis == 1 else (s // per, 0, s % per)

        specs.append(pl.BlockSpec(block, imap))
    return specs


def _dense_ffn(u, wg, wu, wd, tm, tf, casts=()):
    t, d = u.shape
    ff = wg.shape[1]
    grid = (t // tm, ff // tf)
    cast_specs = _cast_slices(casts, grid[0] * grid[1], grid[1]) if casts else []
    if cast_specs is None:
        cast_specs, casts, unsupported = [], (), True
    else:
        unsupported = False
    n_cast = len(cast_specs)
    outs = pl.pallas_call(
        functools.partial(_dense_ffn_kernel, n_cast=n_cast),
        grid=grid,
        in_specs=[pl.BlockSpec((tm, d), lambda i, f: (i, 0)),
                  pl.BlockSpec((d, tf), lambda i, f: (0, f)),
                  pl.BlockSpec((d, tf), lambda i, f: (0, f)),
                  pl.BlockSpec((tf, d), lambda i, f: (f, 0))] + cast_specs,
        out_specs=[pl.BlockSpec((tm, d), lambda i, f: (i, 0))] + cast_specs,
        out_shape=[jax.ShapeDtypeStruct((t, d), BF16)]
        + [jax.ShapeDtypeStruct(a.shape, BF16) for a, _ in casts],
        scratch_shapes=[pltpu.VMEM((tm, d), F32)],
        compiler_params=_cparams(("parallel", "arbitrary")),
        name="dense_ffn",
    )(u, wg, wu, wd, *[a for a, _ in casts])
    return outs[0], (None if unsupported else list(outs[1:]))


def _dispatch_kernel(pend_ref, padded_ref, dest_ref, x_ref, o_ref, zero_ref, row_sem, zero_sem,
                     *, tile, n_tiles):
    rows = x_ref.shape[0]

    @pl.when(pl.program_id(0) == 0)
    def _():
        zero_ref[...] = jnp.zeros(zero_ref.shape, zero_ref.dtype)

        def zero_tile(first_row):
            return pltpu.make_async_copy(zero_ref, o_ref.at[pl.ds(first_row, tile)], zero_sem)

        def last_tile_of(e):
            return zero_tile(pl.multiple_of(pend_ref[e] - tile, tile))

        trailing = range(n_tiles - N_EXPERTS, n_tiles)
        for phase in ("start", "wait"):
            for e in range(N_EXPERTS):
                @pl.when(padded_ref[e] > 0)
                def _():
                    getattr(last_tile_of(e), phase)()
            for tl in trailing:
                @pl.when(tl * tile >= pend_ref[N_EXPERTS - 1])
                def _():
                    getattr(zero_tile(tl * tile), phase)()

    def row_copy(r, k):
        row = dest_ref[0, 0, r * TOP_K + k]
        return pltpu.make_async_copy(x_ref.at[pl.ds(r, 1)], o_ref.at[pl.ds(row, 1)], row_sem)

    def issue(r, carry):
        for k in range(TOP_K):
            row_copy(r, k).start()
        return carry

    def drain(r, carry):
        for k in range(TOP_K):
            row_copy(r, k).wait()
        return carry

    lax.fori_loop(0, rows, issue, 0, unroll=8)
    lax.fori_loop(0, rows, drain, 0, unroll=8)


def _dispatch(u2, dest, pend, padded, tile, n_tiles, tm):
    t, d = u2.shape
    dest3 = dest.reshape(t // tm, 1, TOP_K * tm)
    grid_spec = pltpu.PrefetchScalarGridSpec(
        num_scalar_prefetch=2,
        grid=(t // tm,),
        in_specs=[pl.BlockSpec((1, 1, TOP_K * tm), lambda i, pe, pa: (i, 0, 0), memory_space=pltpu.SMEM),
                  pl.BlockSpec((tm, d), lambda i, pe, pa: (i, 0))],
        out_specs=pl.BlockSpec(memory_space=pl.ANY),
        scratch_shapes=[pltpu.VMEM((tile, d), u2.dtype),
                        pltpu.SemaphoreType.DMA(()),
                        pltpu.SemaphoreType.DMA(())],
    )
    return pl.pallas_call(
        functools.partial(_dispatch_kernel, tile=tile, n_tiles=n_tiles),
        grid_spec=grid_spec,
        out_shape=jax.ShapeDtypeStruct((n_tiles * tile, d), u2.dtype),
        compiler_params=_cparams(("arbitrary",)),
        name="moe_dispatch",
    )(pend, padded, dest3, u2)


def _moe_ffn_kernel(te_ref, nu_ref, tr_ref, x_ref, wg_ref, wu_ref, wd_ref, y_ref, acc_ref, xb_ref):
    i = pl.program_id(0)
    f = pl.program_id(1)
    half = x_ref.shape[0] // 2

    @pl.when(i < nu_ref[0])
    def _():
        @pl.when(f == 0)
        def _():
            acc_ref[...] = jnp.zeros(acc_ref.shape, F32)
            xb_ref[...] = x_ref[...].astype(xb_ref.dtype)

        @pl.when(tr_ref[i] > half)
        def _():
            acc_ref[...] += _swiglu_partial(xb_ref[...], wg_ref[...], wu_ref[...], wd_ref[...])

        @pl.when(tr_ref[i] <= half)
        def _():
            acc_ref[0:half, :] += _swiglu_partial(xb_ref[0:half, :], wg_ref[...], wu_ref[...], wd_ref[...])

        @pl.when(f == pl.num_programs(1) - 1)
        def _():
            y_ref[...] = acc_ref[...].astype(y_ref.dtype)

    @pl.when(i >= nu_ref[0])
    def _():
        y_ref[...] = jnp.zeros(y_ref.shape, y_ref.dtype)


def _moe_ffn(x_sorted, tile_expert, n_used, tile_rows, wg, wu, wd, tm, tf):
    p, d = x_sorted.shape
    ff = wg.shape[2]
    nf = ff // tf

    def row(i, f, te, nu, tr):
        return (jnp.minimum(i, nu[0] - 1), 0)

    def fcol(i, f, nu):
        return jnp.where(i < nu[0], f, nf - 1)

    grid_spec = pltpu.PrefetchScalarGridSpec(
        num_scalar_prefetch=3,
        grid=(p // tm, nf),
        in_specs=[pl.BlockSpec((tm, d), row),
                  pl.BlockSpec((None, d, tf), lambda i, f, te, nu, tr: (te[i], 0, fcol(i, f, nu))),
                  pl.BlockSpec((None, d, tf), lambda i, f, te, nu, tr: (te[i], 0, fcol(i, f, nu))),
                  pl.BlockSpec((None, tf, d), lambda i, f, te, nu, tr: (te[i], fcol(i, f, nu), 0))],
        out_specs=pl.BlockSpec((tm, d), lambda i, f, te, nu, tr: (i, 0)),
        scratch_shapes=[pltpu.VMEM((tm, d), F32), pltpu.VMEM((tm, d), wg.dtype)],
    )
    return pl.pallas_call(
        _moe_ffn_kernel,
        grid_spec=grid_spec,
        out_shape=jax.ShapeDtypeStruct((p, d), BF16),
        compiler_params=_cparams(("arbitrary", "arbitrary")),
        name="moe_ffn",
    )(tile_expert, n_used, tile_rows, x_sorted, wg, wu, wd)


def _moe(u2, route, wg, wu, wd, tm_route, tm, tf):
    t, d = u2.shape
    expert = route[:, 2:2 + TOP_K].astype(jnp.int32).reshape(-1)
    onehot = (expert[:, None] == jnp.arange(N_EXPERTS, dtype=jnp.int32)[None, :]).astype(jnp.int32)
    csum = jnp.cumsum(onehot, axis=0)
    rank = jnp.sum(csum * onehot, axis=1) - 1
    counts = csum[-1]
    padded = ((counts + tm - 1) // tm) * tm
    pend = jnp.cumsum(padded)
    pstart = pend - padded
    dest = pstart[expert] + rank
    n_tiles = (TOP_K * t) // tm + N_EXPERTS
    tile_start = jnp.arange(n_tiles, dtype=jnp.int32) * tm
    tile_expert = jnp.minimum(jnp.sum((tile_start[:, None] >= pend[None, :]).astype(jnp.int32), axis=1),
                              N_EXPERTS - 1).astype(jnp.int32)
    n_used = (pend[-1] // tm).astype(jnp.int32).reshape(1)
    x_sorted = _dispatch(u2, dest.astype(jnp.int32), pend.astype(jnp.int32), padded.astype(jnp.int32),
                         tm, n_tiles, tm_route)
    tile_rows = jnp.clip(counts[tile_expert] - (tile_start - pstart[tile_expert]), 0, tm).astype(jnp.int32)
    y_sorted = _moe_ffn(x_sorted, tile_expert, n_used, tile_rows, wg, wu, wd, tm, tf)
    dest2 = dest.reshape(t, TOP_K)
    return [y_sorted.at[dest2[:, k]].get(mode="promise_in_bounds") for k in range(TOP_K)]


def _ple_kernel(*refs, n_y, gated, final):
    h_ref, y_refs = refs[0], refs[1:1 + n_y]
    rest = refs[1 + n_y:]
    if gated:
        r = rest[0][...]
        rest = rest[1:]
    gp_ref, wg_ref, p_ref, wp_ref, gn_ref = rest[:5]
    out_refs = rest[5:]
    h2 = h_ref[...]
    for k, y_ref in enumerate(y_refs):
        y = y_ref[...].astype(F32)
        h2 = h2 + (r[:, k:k + 1] * y if gated else y)
    u3 = _rms(h2, gp_ref[...], EPS).astype(BF16)
    gate = jax.nn.sigmoid(jnp.dot(u3, wg_ref[...], preferred_element_type=F32))
    emb = jnp.dot(p_ref[...].astype(BF16), wp_ref[...], preferred_element_type=F32)
    h3 = h2 + emb * gate
    if final:
        out_refs[0][...] = _rms(h3, gn_ref[...], EPS)
    else:
        out_refs[0][...] = h3
        out_refs[1][...] = _rms(h3, gn_ref[...], EPS).astype(out_refs[1].dtype)


def _ple(h1, ys, route, ple_gain, w_gate, p, w_proj, next_gain, final, tm):
    t, d = h1.shape
    pd = p.shape[1]
    n_y = len(ys)
    blk = pl.BlockSpec((tm, d), lambda i: (i, 0))
    vec = _resident((1, d), lambda i: (0, 0))
    gated = route is not None
    in_specs = [blk] * (1 + n_y)
    args = [h1] + list(ys)
    if gated:
        in_specs.append(pl.BlockSpec((tm, LANES), lambda i: (i, 0)))
        args.append(route)
    in_specs += [vec, _resident((d, d), lambda i: (0, 0)), pl.BlockSpec((tm, pd), lambda i: (i, 0)),
                 _resident((pd, d), lambda i: (0, 0)), vec]
    args += [ple_gain.reshape(1, d), w_gate, p, w_proj, next_gain.reshape(1, d)]
    if final:
        out_specs = [blk]
        out_shape = [jax.ShapeDtypeStruct((t, d), F32)]
    else:
        out_specs = [blk, blk]
        out_shape = [jax.ShapeDtypeStruct((t, d), F32), jax.ShapeDtypeStruct((t, d), BF16)]
    return pl.pallas_call(
        functools.partial(_ple_kernel, n_y=n_y, gated=gated, final=final),
        grid=(t // tm,),
        in_specs=in_specs,
        out_specs=out_specs,
        out_shape=out_shape,
        compiler_params=_cparams(("parallel",)),
        name="ple",
    )(*args)


def _tiles(t, seq, ff):
    return dict(
        tm_norm=min(512, t),
        tm_in=min(2048, t), tn_in=512,
        tq=min(512, seq), tk=min(512, seq), heads_per_step=2,
        ts=min(512, seq),
        tm_out=min(512, t),
        tm_ffn=min(1024, t), tf_ffn=min(512, ff),
        tm_moe=min(512, t), tf_moe=min(1024, ff),
        tm_ple=min(512, t),
    )


def _rotary_tables(positions):
    inv_freq = ROPE_THETA ** (-jnp.arange(0, ROPE_DIM, 2, dtype=F32) / ROPE_DIM)
    ang = positions.astype(F32).reshape(-1, 1) * inv_freq
    cos, sin = lax.optimization_barrier((jnp.cos(ang), jnp.sin(ang)))
    cos_rep = jnp.tile(cos, (1, LANES // ROPE_HALF))
    sin_rep = jnp.tile(sin, (1, LANES // ROPE_HALF))
    dim128 = (jnp.arange(LANES, dtype=jnp.int32) % HEAD_DIM)[None, :]
    c = jnp.where(dim128 < ROPE_DIM, cos_rep, 1.0)
    s = jnp.where(dim128 < ROPE_HALF, -sin_rep, jnp.where(dim128 < ROPE_DIM, sin_rep, 0.0))
    lane = jnp.arange(2 * LANES, dtype=jnp.int32)
    dim = lane % HEAD_DIM
    src = jnp.where(dim < ROPE_HALF, lane + ROPE_HALF, jnp.where(dim < ROPE_DIM, lane - ROPE_HALF, lane))
    swap = (lane[:, None] == src[None, :]).astype(BF16)
    return c, s, swap


def kernel(x, p, positions, mix_norm, w_in, lambda_q1, lambda_k1, lambda_q2, lambda_k2, subln_gain, w_pool, pool_scale, conv_dw, conv_dw_bias, conv_ln_gain, conv_ln_bias, w_conv_pw, w_out, ffn_norm, w_dense_gate, w_dense_up, w_dense_down, w_router, w_exp_gate, w_exp_up, w_exp_down, ple_norm, w_ple_gate, w_ple_proj, final_norm):
    batch, seq, d = x.shape
    depth = w_in.shape[0]
    t = batch * seq
    attn_w = (w_in.shape[2] - 2 * conv_dw.shape[2] - pool_scale.shape[1]) // 3
    n_heads = attn_w // V_DIM
    qk_cols = attn_w
    pool_col = 3 * attn_w
    ff = w_dense_gate.shape[2]
    ts = _tiles(t, seq, ff)

    rot_c, rot_s, rot_swap = _rotary_tables(positions)
    h = x.reshape(t, d)
    u = _rmsnorm(h, mix_norm[0], ts["tm_norm"])
    out = None
    experts_bf16 = None
    for i in range(depth):
        lam_init = 0.8 - 0.6 * math.exp(-0.3 * i)
        final = i == depth - 1
        z = _inproj(u, w_in, i, rot_c, rot_s, rot_swap, qk_cols, ts["tm_in"], ts["tn_in"])
        lam_params = jnp.stack([lambda_q1[i], lambda_k1[i], lambda_q2[i], lambda_k2[i]]).astype(F32)
        o_attn = _attention(z, lam_params, subln_gain[i], batch, seq, n_heads, lam_init, ts["tq"],
                            ts["tk"], ts["heads_per_step"])
        o_local = _local_mixers(z, w_pool[i].astype(BF16), pool_scale[i], conv_dw[i], conv_dw_bias[i],
                                conv_ln_gain[i], conv_ln_bias[i], w_conv_pw[i].astype(BF16),
                                seq, pool_col, ts["ts"])
        j = i // 2
        h1, u2 = _outproj(o_attn, o_local, w_out[i].astype(BF16), h, ffn_norm[i], ts["tm_out"],
                          BF16 if i % 2 == 0 else F32)
        if i % 2 == 0:
            casts = ()
            if i + 1 < depth:
                jn = (i + 1) // 2
                casts = ((w_exp_gate[jn], 2), (w_exp_up[jn], 2), (w_exp_down[jn], 1))
            y, experts_bf16 = _dense_ffn(u2, w_dense_gate[j].astype(BF16), w_dense_up[j].astype(BF16),
                                         w_dense_down[j].astype(BF16), ts["tm_ffn"], ts["tf_ffn"], casts)
            ys, route = [y], None
        else:
            if experts_bf16 is None:
                experts_bf16 = [w.astype(BF16) for w in (w_exp_gate[j], w_exp_up[j], w_exp_down[j])]
            route = _router(u2, w_router[j], ts["tm_out"])
            ys = _moe(u2, route, *experts_bf16, ts["tm_out"], ts["tm_moe"], ts["tf_moe"])
            experts_bf16 = None
        next_gain = final_norm if final else mix_norm[i + 1]
        res = _ple(h1, ys, route, ple_norm[i], w_ple_gate[i].astype(BF16), p[i].reshape(t, -1),
                   w_ple_proj[i].astype(BF16), next_gain, final, ts["tm_ple"])
        if final:
            out = res[0]
        else:
            h, u = res
    return out.reshape(batch, seq, d)
```

```python
import functools
import math

import jax
import jax.numpy as jnp
from jax import lax
from jax.experimental import pallas as pl
from jax.experimental.pallas import tpu as pltpu

F32 = jnp.float32
BF16 = jnp.bfloat16

EPS = 1e-6
SUBLN_EPS = 1e-5
LN_EPS = 1e-5
HEAD_DIM = 64
V_DIM = 2 * HEAD_DIM
ROPE_DIM = HEAD_DIM // 4
ROPE_HALF = ROPE_DIM // 2
ROPE_THETA = 500000.0
LOG2_E = math.log2(math.e)
POOL_WINDOWS = (2, 4, 8, 16)
POOL_GROUP_DIM = 128
CONV_KERNEL = 31
N_EXPERTS = 8
TOP_K = 2

LANES = 128
SUBLANES = 8
V7X_VMEM_BYTES = 64 * 1024 * 1024
VMEM_LIMIT = V7X_VMEM_BYTES - 2 * 1024 * 1024
POOL_HALO = 16
CONV_HALO = 32


def _cparams(sem):
    return pltpu.CompilerParams(dimension_semantics=sem, vmem_limit_bytes=VMEM_LIMIT)


def _resident(shape, index_map):
    return pl.BlockSpec(shape, index_map, pipeline_mode=pl.Buffered(1))


def _rms(x, gain, eps):
    return x * lax.rsqrt(jnp.mean(x * x, axis=-1, keepdims=True) + eps) * gain


def _rmsnorm_kernel(x_ref, g_ref, o_ref):
    o_ref[...] = _rms(x_ref[...], g_ref[...], EPS).astype(o_ref.dtype)


def _rmsnorm(x, gain, tm):
    t, d = x.shape
    return pl.pallas_call(
        _rmsnorm_kernel,
        grid=(t // tm,),
        in_specs=[pl.BlockSpec((tm, d), lambda i: (i, 0)), pl.BlockSpec((1, d), lambda i: (0, 0))],
        out_specs=pl.BlockSpec((tm, d), lambda i: (i, 0)),
        out_shape=jax.ShapeDtypeStruct((t, d), BF16),
        compiler_params=_cparams(("parallel",)),
        name="rmsnorm",
    )(x, gain.reshape(1, d))


def _inproj_kernel(u_ref, w_ref, c_ref, s_ref, swap_ref, z_ref, *, n_rot_blocks, n_q_blocks, q_scale):
    j = pl.program_id(1)
    u = u_ref[...]
    acc = jnp.dot(u, w_ref[...].astype(u.dtype), preferred_element_type=F32)
    tn = acc.shape[1]

    @pl.when(j < n_rot_blocks)
    def _():
        reps = tn // LANES
        c = jnp.concatenate([c_ref[...]] * reps, axis=1)
        s = jnp.concatenate([s_ref[...]] * reps, axis=1)
        pw = swap_ref.shape[0]
        ab = acc.astype(swap_ref.dtype)
        partner = jnp.concatenate(
            [jnp.dot(ab[:, b * pw:(b + 1) * pw], swap_ref[...], preferred_element_type=F32)
             for b in range(tn // pw)], axis=1)
        r = acc * c + partner * s
        scale = jnp.where(j < n_q_blocks, q_scale, 1.0).astype(F32)
        z_ref[...] = (r * scale).astype(z_ref.dtype)

    @pl.when(j >= n_rot_blocks)
    def _():
        z_ref[...] = acc.astype(z_ref.dtype)


def _inproj(u, w_all, layer, rot_c, rot_s, swap, qk_cols, tm, tn):
    t, d = u.shape
    n = w_all.shape[2]
    kern = functools.partial(_inproj_kernel, n_rot_blocks=2 * qk_cols // tn, n_q_blocks=qk_cols // tn,
                             q_scale=HEAD_DIM ** -0.5 * LOG2_E)
    tab = pl.BlockSpec((tm, LANES), lambda i, j: (i, 0))
    return pl.pallas_call(
        kern,
        grid=(t // tm, n // tn),
        in_specs=[pl.BlockSpec((tm, d), lambda i, j: (i, 0)),
                  pl.BlockSpec((None, d, tn), lambda i, j: (layer, 0, j)),
                  tab, tab, _resident(swap.shape, lambda i, j: (0, 0))],
        out_specs=pl.BlockSpec((tm, tn), lambda i, j: (i, j)),
        out_shape=jax.ShapeDtypeStruct((t, n), BF16),
        compiler_params=_cparams(("parallel", "arbitrary")),
        name="inproj",
    )(u, w_all, rot_c, rot_s, swap)


def _attn_kernel(lam_ref, gain_ref, q_ref, k_ref, v_ref, o_ref, qs_ref, vx_ref, m_ref, acc_ref, sa_ref,
                 sb_ref, *, tq, tk, hp, lam_init):
    qi = pl.program_id(2)
    per = tk // tq

    @pl.when(qi == 0)
    def _():
        for a in range(hp):
            vx_ref[a, :, 0:V_DIM] = v_ref[:, a * V_DIM:(a + 1) * V_DIM]
            vx_ref[a, :, V_DIM:2 * V_DIM] = jnp.ones((v_ref.shape[0], V_DIM), v_ref.dtype)

    lane = lax.broadcasted_iota(jnp.int32, (tq, LANES), 1)
    for a in range(hp):
        q = q_ref[:, a * LANES:(a + 1) * LANES]
        zero = jnp.zeros_like(q)
        qs_ref[a, 0:tq, :] = jnp.where(lane < HEAD_DIM, q, zero)
        qs_ref[a, tq:2 * tq, :] = jnp.where(lane >= HEAD_DIM, q, zero)
    m_ref[...] = jnp.full(m_ref.shape, -jnp.inf, F32)
    acc_ref[...] = jnp.zeros(acc_ref.shape, F32)

    def scores(j, s_ref):
        start = pl.multiple_of(j * tk, tk)
        for a in range(hp):
            k = k_ref[pl.ds(start, tk), a * LANES:(a + 1) * LANES]
            s_ref[a] = lax.dot_general(qs_ref[a], k, (((1,), (1,)), ((), ())),
                                       preferred_element_type=F32)

    def update(j, s_ref, masked):
        start = pl.multiple_of(j * tk, tk)
        for a in range(hp):
            vx = vx_ref[a, pl.ds(start, tk), :]
            s = s_ref[a]
            if masked:
                row = lax.broadcasted_iota(jnp.int32, s.shape, 0)
                col = lax.broadcasted_iota(jnp.int32, s.shape, 1)
                s = jnp.where(col <= (row & (tq - 1)) + (qi % per) * tq, s, -jnp.inf)
            m_old = m_ref[a]
            m_new = jnp.maximum(m_old, jnp.max(s, axis=-1, keepdims=True))
            alpha = jnp.exp2(m_old - m_new)
            p = jnp.exp2(s - jnp.concatenate([m_new] * (tk // LANES), axis=1))
            pv = jnp.dot(p.astype(vx.dtype), vx, preferred_element_type=F32)
            acc_ref[a] = jnp.concatenate([alpha, alpha], axis=1) * acc_ref[a] + pv
            m_ref[a] = m_new

    n_full = qi // per
    scores(0, sa_ref)

    def body(j, carry):
        def even():
            scores(j + 1, sb_ref)
            update(j, sa_ref, False)

        def odd():
            scores(j + 1, sa_ref)
            update(j, sb_ref, False)

        lax.cond((j & 1) == 0, even, odd)
        return carry

    lax.fori_loop(0, n_full, body, 0)
    lax.cond((n_full & 1) == 0, lambda: update(n_full, sa_ref, True), lambda: update(n_full, sb_ref, True))

    lp = lam_ref[...]
    lam = (jnp.exp(jnp.sum(lp[0:1] * lp[1:2], axis=-1, keepdims=True))
           - jnp.exp(jnp.sum(lp[2:3] * lp[3:4], axis=-1, keepdims=True)) + lam_init)
    for a in range(hp):
        o_all = acc_ref[a, :, 0:V_DIM] / acc_ref[a, :, V_DIM:2 * V_DIM]
        o = o_all[0:tq] - lam * o_all[tq:2 * tq]
        y = _rms(o, gain_ref[...], SUBLN_EPS) * (1.0 - lam_init)
        o_ref[:, a * V_DIM:(a + 1) * V_DIM] = y.astype(o_ref.dtype)


def _attention(z, lam_params, subln_gain, batch, seq, n_heads, lam_init, tq, tk, hp):
    t = z.shape[0]
    nq = seq // tq
    assert tq & (tq - 1) == 0 and tk % tq == 0 and seq % tk == 0 and n_heads % hp == 0
    kern = functools.partial(_attn_kernel, tq=tq, tk=tk, hp=hp, lam_init=lam_init)
    ng = n_heads // hp
    w = hp * LANES
    return pl.pallas_call(
        kern,
        grid=(batch, ng, nq),
        in_specs=[pl.BlockSpec(lam_params.shape, lambda b, h, i: (0, 0)),
                  pl.BlockSpec((1, V_DIM), lambda b, h, i: (0, 0)),
                  pl.BlockSpec((tq, w), lambda b, h, i: (b * nq + i, h)),
                  pl.BlockSpec((seq, w), lambda b, h, i: (b, ng + h)),
                  pl.BlockSpec((seq, w), lambda b, h, i: (b, 2 * ng + h))],
        out_specs=pl.BlockSpec((tq, w), lambda b, h, i: (b * nq + i, h)),
        out_shape=jax.ShapeDtypeStruct((t, n_heads * V_DIM), BF16),
        scratch_shapes=[pltpu.VMEM((hp, 2 * tq, LANES), BF16),
                        pltpu.VMEM((hp, seq, 2 * V_DIM), BF16),
                        pltpu.VMEM((hp, 2 * tq, LANES), F32),
                        pltpu.VMEM((hp, 2 * tq, 2 * V_DIM), F32),
                        pltpu.VMEM((hp, 2 * tq, tk), F32),
                        pltpu.VMEM((hp, 2 * tq, tk), F32)],
        compiler_params=_cparams(("parallel", "parallel", "arbitrary")),
        name="diff_attention",
    )(lam_params, subln_gain.reshape(1, V_DIM), z, z, z)


def _local_kernel(zp_ref, zph_ref, za_ref, zah_ref, zb_ref, zbh_ref, wpool_ref, pscale_ref,
                  dw_ref, dwb_ref, lng_ref, lnb_ref, wpw_ref, o_ref, pbuf_ref, cbuf_ref, sbuf_ref,
                  *, ts, tiles_per_seq):
    i = pl.program_id(0)
    tile_in_seq = i % tiles_per_seq
    first = tile_in_seq == 0
    pool_w = zp_ref.shape[1]

    zp = zp_ref[...].astype(F32)
    halo = zph_ref[...].astype(F32)
    pbuf_ref[0:POOL_HALO, :] = jnp.where(first, jnp.zeros_like(halo), halo)
    pbuf_ref[POOL_HALO:POOL_HALO + ts, :] = zp
    pos = tile_in_seq * ts + lax.broadcasted_iota(jnp.int32, (ts, 1), 0)
    for g, w in enumerate(POOL_WINDOWS):
        cols = slice(g * POOL_GROUP_DIM, (g + 1) * POOL_GROUP_DIM)
        zg = zp[:, cols]
        wsum = zg
        for k in range(1, w):
            wsum = wsum + pbuf_ref[POOL_HALO - k:POOL_HALO - k + ts, cols]
        count = jnp.minimum(pos + 1, w).astype(F32)
        d = (wsum / count - zg).astype(BF16)
        y = jnp.dot(d, wpool_ref[g], preferred_element_type=F32) * pscale_ref[:, cols]
        o_ref[:, cols] = y.astype(o_ref.dtype)

    c = za_ref[...].astype(F32) * jax.nn.sigmoid(zb_ref[...].astype(F32))
    ch = zah_ref[...].astype(F32) * jax.nn.sigmoid(zbh_ref[...].astype(F32))
    cbuf_ref[0:CONV_HALO, :] = jnp.where(first, jnp.zeros_like(ch), ch)
    cbuf_ref[CONV_HALO:CONV_HALO + ts, :] = c
    acc = jnp.zeros_like(c) + dwb_ref[...]
    base = CONV_HALO - (CONV_KERNEL - 1)
    for b in range(SUBLANES):
        offs = [o for o in range(base, base + CONV_KERNEL) if o % SUBLANES == b]
        if not offs:
            continue
        span = offs[-1] - offs[0] + ts
        sbuf_ref[0:span, :] = cbuf_ref[offs[0]:offs[0] + span, :]
        for o in offs:
            k = o - base
            acc = acc + dw_ref[k:k + 1, :] * sbuf_ref[o - offs[0]:o - offs[0] + ts, :]
    mu = jnp.mean(acc, axis=-1, keepdims=True)
    xc = acc - mu
    yn = xc * lax.rsqrt(jnp.mean(xc * xc, axis=-1, keepdims=True) + LN_EPS) * lng_ref[...] + lnb_ref[...]
    sw = yn * jax.nn.sigmoid(yn)
    out = jnp.dot(sw.astype(BF16), wpw_ref[...], preferred_element_type=F32)
    o_ref[:, pool_w:] = out.astype(o_ref.dtype)


def _local_mixers(z, w_pool, pool_scale, dw, dw_b, ln_g, ln_b, w_pw, seq, pool_col, ts):
    t = z.shape[0]
    pool_w = pool_scale.shape[0]
    conv_w = dw.shape[1]
    assert pool_w == conv_w and pool_col % pool_w == 0
    pc = pool_col // pool_w
    hp = ts // POOL_HALO
    hc = ts // CONV_HALO
    kern = functools.partial(_local_kernel, ts=ts, tiles_per_seq=seq // ts)
    row = lambda shape: _resident(shape, lambda i: (0,) * len(shape))
    return pl.pallas_call(
        kern,
        grid=(t // ts,),
        in_specs=[pl.BlockSpec((ts, pool_w), lambda i: (i, pc)),
                  pl.BlockSpec((POOL_HALO, pool_w), lambda i: (jnp.maximum(i * hp - 1, 0), pc)),
                  pl.BlockSpec((ts, conv_w), lambda i: (i, pc + 1)),
                  pl.BlockSpec((CONV_HALO, conv_w), lambda i: (jnp.maximum(i * hc - 1, 0), pc + 1)),
                  pl.BlockSpec((ts, conv_w), lambda i: (i, pc + 2)),
                  pl.BlockSpec((CONV_HALO, conv_w), lambda i: (jnp.maximum(i * hc - 1, 0), pc + 2)),
                  row(w_pool.shape), row((1, pool_w)), row(dw.shape), row((1, conv_w)),
                  row((1, conv_w)), row((1, conv_w)), row(w_pw.shape)],
        out_specs=pl.BlockSpec((ts, pool_w + conv_w), lambda i: (i, 0)),
        out_shape=jax.ShapeDtypeStruct((t, pool_w + conv_w), BF16),
        scratch_shapes=[pltpu.VMEM((POOL_HALO + ts, pool_w), F32),
                        pltpu.VMEM((CONV_HALO + ts, conv_w), F32),
                        pltpu.VMEM((CONV_HALO + ts, conv_w), F32)],
        compiler_params=_cparams(("parallel",)),
        name="local_mixers",
    )(z, z, z, z, z, z, w_pool, pool_scale.reshape(1, pool_w), dw, dw_b.reshape(1, conv_w),
      ln_g.reshape(1, conv_w), ln_b.reshape(1, conv_w), w_pw)


def _top2_route(u, w_router_padded):
    u_hi = u.astype(BF16)
    u_lo = (u - u_hi.astype(F32)).astype(BF16)
    w_hi = w_router_padded.astype(BF16)
    w_lo = (w_router_padded - w_hi.astype(F32)).astype(BF16)
    logits = (jnp.dot(u_hi, w_hi, preferred_element_type=F32)
              + (jnp.dot(u_hi, w_lo, preferred_element_type=F32)
                 + jnp.dot(u_lo, w_hi, preferred_element_type=F32)))
    lane = lax.broadcasted_iota(jnp.int32, logits.shape, 1)
    neg = jnp.full(logits.shape, -jnp.inf, F32)
    lg = jnp.where(lane < N_EXPERTS, logits, neg)
    m1 = jnp.max(lg, axis=-1, keepdims=True)
    i1 = jnp.min(jnp.where(lg == m1, lane, LANES), axis=-1, keepdims=True)
    lg2 = jnp.where(lane == i1, neg, lg)
    m2 = jnp.max(lg2, axis=-1, keepdims=True)
    i2 = jnp.min(jnp.where(lg2 == m2, lane, LANES), axis=-1, keepdims=True)
    e2 = jnp.exp(m2 - m1)
    g1 = 1.0 / (1.0 + e2)
    g2 = e2 / (1.0 + e2)
    return jnp.where(lane == 0, g1, jnp.where(lane == 1, g2, jnp.where(
        lane == 2, i1.astype(F32), jnp.where(lane == 3, i2.astype(F32), 0.0))))


def _router_kernel(u_ref, w_ref, o_ref):
    o_ref[...] = _top2_route(u_ref[...], w_ref[...])


def _router(u2, w_router, tm):
    t, d = u2.shape
    w = jnp.zeros((d, LANES), F32).at[:, :N_EXPERTS].set(w_router)
    return pl.pallas_call(
        _router_kernel,
        grid=(t // tm,),
        in_specs=[pl.BlockSpec((tm, d), lambda i: (i, 0)),
                  _resident((d, LANES), lambda i: (0, 0))],
        out_specs=pl.BlockSpec((tm, LANES), lambda i: (i, 0)),
        out_shape=jax.ShapeDtypeStruct((t, LANES), F32),
        compiler_params=_cparams(("parallel",)),
        name="router",
    )(u2, w)


def _outproj_kernel(oa_ref, ol_ref, w_ref, h_ref, g_ref, h1_ref, u_ref):
    ka = oa_ref.shape[1]
    mix = jnp.dot(oa_ref[...], w_ref[0:ka, :], preferred_element_type=F32)
    mix = mix + jnp.dot(ol_ref[...], w_ref[ka:, :], preferred_element_type=F32)
    h1 = h_ref[...] + mix
    h1_ref[...] = h1
    u_ref[...] = _rms(h1, g_ref[...], EPS).astype(u_ref.dtype)


def _outproj(o_attn, o_local, w_out, h, gain, tm, u_dtype):
    t, d = h.shape
    ka, kl = o_attn.shape[1], o_local.shape[1]
    blk = pl.BlockSpec((tm, d), lambda i: (i, 0))
    return pl.pallas_call(
        _outproj_kernel,
        grid=(t // tm,),
        in_specs=[pl.BlockSpec((tm, ka), lambda i: (i, 0)),
                  pl.BlockSpec((tm, kl), lambda i: (i, 0)),
                  _resident((ka + kl, d), lambda i: (0, 0)),
                  blk,
                  _resident((1, d), lambda i: (0, 0))],
        out_specs=[blk, blk],
        out_shape=[jax.ShapeDtypeStruct((t, d), F32), jax.ShapeDtypeStruct((t, d), u_dtype)],
        compiler_params=_cparams(("parallel",)),
        name="outproj",
    )(o_attn, o_local, w_out, h, gain.reshape(1, d))


def _swiglu_partial(x, wg, wu, wd):
    g = jnp.dot(x, wg, preferred_element_type=F32)
    u = jnp.dot(x, wu, preferred_element_type=F32)
    hidden = (g * jax.nn.sigmoid(g)) * u
    return jnp.dot(hidden.astype(wd.dtype), wd, preferred_element_type=F32)


def _dense_ffn_kernel(*refs, n_cast):
    u_ref, wg_ref, wu_ref, wd_ref = refs[:4]
    cast_in = refs[4:4 + n_cast]
    y_ref = refs[4 + n_cast]
    cast_out = refs[5 + n_cast:5 + 2 * n_cast]
    acc_ref = refs[5 + 2 * n_cast]
    f = pl.program_id(1)

    @pl.when(f == 0)
    def _():
        acc_ref[...] = jnp.zeros(acc_ref.shape, F32)

    acc_ref[...] += _swiglu_partial(u_ref[...], wg_ref[...], wu_ref[...], wd_ref[...])

    for src, dst in zip(cast_in, cast_out):
        dst[...] = src[...].astype(dst.dtype)

    @pl.when(f == pl.num_programs(1) - 1)
    def _():
        y_ref[...] = acc_ref[...].astype(y_ref.dtype)


def _cast_slices(arrays_axes, n_steps, nf):
    specs = []
    for arr, axis in arrays_axes:
        e = arr.shape[0]
        if n_steps % e:
            return None
        per = n_steps // e
        if arr.shape[axis] % per:
            return None
        width = arr.shape[axis] // per
        align = LANES if axis == 2 else 16
        if width % align:
            return None
        block = tuple(None if a == 0 else (width if a == axis else arr.shape[a]) for a in range(3))

        def imap(i, f, per=per, axis=axis):
            s = i * nf + f
            return (s // per, s % per, 0) if axis == 1 else (s // per, 0, s % per)

        specs.append(pl.BlockSpec(block, imap))
    return specs


def _dense_ffn(u, wg, wu, wd, tm, tf, casts=()):
    t, d = u.shape
    ff = wg.shape[1]
    grid = (t // tm, ff // tf)
    cast_specs = _cast_slices(casts, grid[0] * grid[1], grid[1]) if casts else []
    if cast_specs is None:
        cast_specs, casts, unsupported = [], (), True
    else:
        unsupported = False
    n_cast = len(cast_specs)
    outs = pl.pallas_call(
        functools.partial(_dense_ffn_kernel, n_cast=n_cast),
        grid=grid,
        in_specs=[pl.BlockSpec((tm, d), lambda i, f: (i, 0)),
                  pl.BlockSpec((d, tf), lambda i, f: (0, f)),
                  pl.BlockSpec((d, tf), lambda i, f: (0, f)),
                  pl.BlockSpec((tf, d), lambda i, f: (f, 0))] + cast_specs,
        out_specs=[pl.BlockSpec((tm, d), lambda i, f: (i, 0))] + cast_specs,
        out_shape=[jax.ShapeDtypeStruct((t, d), BF16)]
        + [jax.ShapeDtypeStruct(a.shape, BF16) for a, _ in casts],
        scratch_shapes=[pltpu.VMEM((tm, d), F32)],
        compiler_params=_cparams(("parallel", "arbitrary")),
        name="dense_ffn",
    )(u, wg, wu, wd, *[a for a, _ in casts])
    return outs[0], (None if unsupported else list(outs[1:]))


def _dispatch_kernel(pend_ref, padded_ref, dest_ref, x_ref, o_ref, zero_ref, row_sem, zero_sem,
                     *, tile, n_tiles):
    rows = x_ref.shape[0]

    @pl.when(pl.program_id(0) == 0)
    def _():
        zero_ref[...] = jnp.zeros(zero_ref.shape, zero_ref.dtype)

        def zero_tile(first_row):
            return pltpu.make_async_copy(zero_ref, o_ref.at[pl.ds(first_row, tile)], zero_sem)

        def last_tile_of(e):
            return zero_tile(pl.multiple_of(pend_ref[e] - tile, tile))

        trailing = range(n_tiles - N_EXPERTS, n_tiles)
        for phase in ("start", "wait"):
            for e in range(N_EXPERTS):
                @pl.when(padded_ref[e] > 0)
                def _():
                    getattr(last_tile_of(e), phase)()
            for tl in trailing:
                @pl.when(tl * tile >= pend_ref[N_EXPERTS - 1])
                def _():
                    getattr(zero_tile(tl * tile), phase)()

    def row_copy(r, k):
        row = dest_ref[0, 0, r * TOP_K + k]
        return pltpu.make_async_copy(x_ref.at[pl.ds(r, 1)], o_ref.at[pl.ds(row, 1)], row_sem)

    def issue(r, carry):
        for k in range(TOP_K):
            row_copy(r, k).start()
        return carry

    def drain(r, carry):
        for k in range(TOP_K):
            row_copy(r, k).wait()
        return carry

    lax.fori_loop(0, rows, issue, 0, unroll=8)
    lax.fori_loop(0, rows, drain, 0, unroll=8)


def _dispatch(u2, dest, pend, padded, tile, n_tiles, tm):
    t, d = u2.shape
    dest3 = dest.reshape(t // tm, 1, TOP_K * tm)
    grid_spec = pltpu.PrefetchScalarGridSpec(
        num_scalar_prefetch=2,
        grid=(t // tm,),
        in_specs=[pl.BlockSpec((1, 1, TOP_K * tm), lambda i, pe, pa: (i, 0, 0), memory_space=pltpu.SMEM),
                  pl.BlockSpec((tm, d), lambda i, pe, pa: (i, 0))],
        out_specs=pl.BlockSpec(memory_space=pl.ANY),
        scratch_shapes=[pltpu.VMEM((tile, d), u2.dtype),
                        pltpu.SemaphoreType.DMA(()),
                        pltpu.SemaphoreType.DMA(())],
    )
    return pl.pallas_call(
        functools.partial(_dispatch_kernel, tile=tile, n_tiles=n_tiles),
        grid_spec=grid_spec,
        out_shape=jax.ShapeDtypeStruct((n_tiles * tile, d), u2.dtype),
        compiler_params=_cparams(("arbitrary",)),
        name="moe_dispatch",
    )(pend, padded, dest3, u2)


def _moe_ffn_kernel(te_ref, nu_ref, tr_ref, x_ref, wg_ref, wu_ref, wd_ref, y_ref, acc_ref, xb_ref):
    i = pl.program_id(0)
    f = pl.program_id(1)
    half = x_ref.shape[0] // 2

    @pl.when(i < nu_ref[0])
    def _():
        @pl.when(f == 0)
        def _():
            acc_ref[...] = jnp.zeros(acc_ref.shape, F32)
            xb_ref[...] = x_ref[...].astype(xb_ref.dtype)

        @pl.when(tr_ref[i] > half)
        def _():
            acc_ref[...] += _swiglu_partial(xb_ref[...], wg_ref[...], wu_ref[...], wd_ref[...])

        @pl.when(tr_ref[i] <= half)
        def _():
            acc_ref[0:half, :] += _swiglu_partial(xb_ref[0:half, :], wg_ref[...], wu_ref[...], wd_ref[...])

        @pl.when(f == pl.num_programs(1) - 1)
        def _():
            y_ref[...] = acc_ref[...].astype(y_ref.dtype)

    @pl.when(i >= nu_ref[0])
    def _():
        y_ref[...] = jnp.zeros(y_ref.shape, y_ref.dtype)


def _moe_ffn(x_sorted, tile_expert, n_used, tile_rows, wg, wu, wd, tm, tf):
    p, d = x_sorted.shape
    ff = wg.shape[2]
    nf = ff // tf

    def row(i, f, te, nu, tr):
        return (jnp.minimum(i, nu[0] - 1), 0)

    def fcol(i, f, nu):
        return jnp.where(i < nu[0], f, nf - 1)

    grid_spec = pltpu.PrefetchScalarGridSpec(
        num_scalar_prefetch=3,
        grid=(p // tm, nf),
        in_specs=[pl.BlockSpec((tm, d), row),
                  pl.BlockSpec((None, d, tf), lambda i, f, te, nu, tr: (te[i], 0, fcol(i, f, nu))),
                  pl.BlockSpec((None, d, tf), lambda i, f, te, nu, tr: (te[i], 0, fcol(i, f, nu))),
                  pl.BlockSpec((None, tf, d), lambda i, f, te, nu, tr: (te[i], fcol(i, f, nu), 0))],
        out_specs=pl.BlockSpec((tm, d), lambda i, f, te, nu, tr: (i, 0)),
        scratch_shapes=[pltpu.VMEM((tm, d), F32), pltpu.VMEM((tm, d), wg.dtype)],
    )
    return pl.pallas_call(
        _moe_ffn_kernel,
        grid_spec=grid_spec,
        out_shape=jax.ShapeDtypeStruct((p, d), BF16),
        compiler_params=_cparams(("arbitrary", "arbitrary")),
        name="moe_ffn",
    )(tile_expert, n_used, tile_rows, x_sorted, wg, wu, wd)


def _moe(u2, route, wg, wu, wd, tm_route, tm, tf):
    t, d = u2.shape
    expert = route[:, 2:2 + TOP_K].astype(jnp.int32).reshape(-1)
    onehot = (expert[:, None] == jnp.arange(N_EXPERTS, dtype=jnp.int32)[None, :]).astype(jnp.int32)
    csum = jnp.cumsum(onehot, axis=0)
    rank = jnp.sum(csum * onehot, axis=1) - 1
    counts = csum[-1]
    padded = ((counts + tm - 1) // tm) * tm
    pend = jnp.cumsum(padded)
    pstart = pend - padded
    dest = pstart[expert] + rank
    n_tiles = (TOP_K * t) // tm + N_EXPERTS
    tile_start = jnp.arange(n_tiles, dtype=jnp.int32) * tm
    tile_expert = jnp.minimum(jnp.sum((tile_start[:, None] >= pend[None, :]).astype(jnp.int32), axis=1),
                              N_EXPERTS - 1).astype(jnp.int32)
    n_used = (pend[-1] // tm).astype(jnp.int32).reshape(1)
    x_sorted = _dispatch(u2, dest.astype(jnp.int32), pend.astype(jnp.int32), padded.astype(jnp.int32),
                         tm, n_tiles, tm_route)
    tile_rows = jnp.clip(counts[tile_expert] - (tile_start - pstart[tile_expert]), 0, tm).astype(jnp.int32)
    y_sorted = _moe_ffn(x_sorted, tile_expert, n_used, tile_rows, wg, wu, wd, tm, tf)
    dest2 = dest.reshape(t, TOP_K)
    return [y_sorted.at[dest2[:, k]].get(mode="promise_in_bounds") for k in range(TOP_K)]


def _ple_kernel(*refs, n_y, gated, final):
    h_ref, y_refs = refs[0], refs[1:1 + n_y]
    rest = refs[1 + n_y:]
    if gated:
        r = rest[0][...]
        rest = rest[1:]
    gp_ref, wg_ref, p_ref, wp_ref, gn_ref = rest[:5]
    out_refs = rest[5:]
    h2 = h_ref[...]
    for k, y_ref in enumerate(y_refs):
        y = y_ref[...].astype(F32)
        h2 = h2 + (r[:, k:k + 1] * y if gated else y)
    u3 = _rms(h2, gp_ref[...], EPS).astype(BF16)
    gate = jax.nn.sigmoid(jnp.dot(u3, wg_ref[...], preferred_element_type=F32))
    emb = jnp.dot(p_ref[...].astype(BF16), wp_ref[...], preferred_element_type=F32)
    h3 = h2 + emb * gate
    if final:
        out_refs[0][...] = _rms(h3, gn_ref[...], EPS)
    else:
        out_refs[0][...] = h3
        out_refs[1][...] = _rms(h3, gn_ref[...], EPS).astype(out_refs[1].dtype)


def _ple(h1, ys, route, ple_gain, w_gate, p, w_proj, next_gain, final, tm):
    t, d = h1.shape
    pd = p.shape[1]
    n_y = len(ys)
    blk = pl.BlockSpec((tm, d), lambda i: (i, 0))
    vec = _resident((1, d), lambda i: (0, 0))
    gated = route is not None
    in_specs = [blk] * (1 + n_y)
    args = [h1] + list(ys)
    if gated:
        in_specs.append(pl.BlockSpec((tm, LANES), lambda i: (i, 0)))
        args.append(route)
    in_specs += [vec, _resident((d, d), lambda i: (0, 0)), pl.BlockSpec((tm, pd), lambda i: (i, 0)),
                 _resident((pd, d), lambda i: (0, 0)), vec]
    args += [ple_gain.reshape(1, d), w_gate, p, w_proj, next_gain.reshape(1, d)]
    if final:
        out_specs = [blk]
        out_shape = [jax.ShapeDtypeStruct((t, d), F32)]
    else:
        out_specs = [blk, blk]
        out_shape = [jax.ShapeDtypeStruct((t, d), F32), jax.ShapeDtypeStruct((t, d), BF16)]
    return pl.pallas_call(
        functools.partial(_ple_kernel, n_y=n_y, gated=gated, final=final),
        grid=(t // tm,),
        in_specs=in_specs,
        out_specs=out_specs,
        out_shape=out_shape,
        compiler_params=_cparams(("parallel",)),
        name="ple",
    )(*args)


def _tiles(t, seq, ff):
    return dict(
        tm_norm=min(512, t),
        tm_in=min(2048, t), tn_in=512,
        tq=min(512, seq), tk=min(512, seq), heads_per_step=2,
        ts=min(512, seq),
        tm_out=min(512, t),
        tm_ffn=min(1024, t), tf_ffn=min(512, ff),
        tm_moe=min(512, t), tf_moe=min(1024, ff),
        tm_ple=min(512, t),
    )


def _rotary_tables(positions):
    inv_freq = ROPE_THETA ** (-jnp.arange(0, ROPE_DIM, 2, dtype=F32) / ROPE_DIM)
    ang = positions.astype(F32).reshape(-1, 1) * inv_freq
    cos, sin = lax.optimization_barrier((jnp.cos(ang), jnp.sin(ang)))
    cos_rep = jnp.tile(cos, (1, LANES // ROPE_HALF))
    sin_rep = jnp.tile(sin, (1, LANES // ROPE_HALF))
    dim128 = (jnp.arange(LANES, dtype=jnp.int32) % HEAD_DIM)[None, :]
    c = jnp.where(dim128 < ROPE_DIM, cos_rep, 1.0)
    s = jnp.where(dim128 < ROPE_HALF, -sin_rep, jnp.where(dim128 < ROPE_DIM, sin_rep, 0.0))
    lane = jnp.arange(2 * LANES, dtype=jnp.int32)
    dim = lane % HEAD_DIM
    src = jnp.where(dim < ROPE_HALF, lane + ROPE_HALF, jnp.where(dim < ROPE_DIM, lane - ROPE_HALF, lane))
    swap = (lane[:, None] == src[None, :]).astype(BF16)
    return c, s, swap


def kernel(x, p, positions, mix_norm, w_in, lambda_q1, lambda_k1, lambda_q2, lambda_k2, subln_gain, w_pool, pool_scale, conv_dw, conv_dw_bias, conv_ln_gain, conv_ln_bias, w_conv_pw, w_out, ffn_norm, w_dense_gate, w_dense_up, w_dense_down, w_router, w_exp_gate, w_exp_up, w_exp_down, ple_norm, w_ple_gate, w_ple_proj, final_norm):
    batch, seq, d = x.shape
    depth = w_in.shape[0]
    t = batch * seq
    attn_w = (w_in.shape[2] - 2 * conv_dw.shape[2] - pool_scale.shape[1]) // 3
    n_heads = attn_w // V_DIM
    qk_cols = attn_w
    pool_col = 3 * attn_w
    ff = w_dense_gate.shape[2]
    ts = _tiles(t, seq, ff)

    rot_c, rot_s, rot_swap = _rotary_tables(positions)
    h = x.reshape(t, d)
    u = _rmsnorm(h, mix_norm[0], ts["tm_norm"])
    out = None
    experts_bf16 = None
    for i in range(depth):
        lam_init = 0.8 - 0.6 * math.exp(-0.3 * i)
        final = i == depth - 1
        z = _inproj(u, w_in, i, rot_c, rot_s, rot_swap, qk_cols, ts["tm_in"], ts["tn_in"])
        lam_params = jnp.stack([lambda_q1[i], lambda_k1[i], lambda_q2[i], lambda_k2[i]]).astype(F32)
        o_attn = _attention(z, lam_params, subln_gain[i], batch, seq, n_heads, lam_init, ts["tq"],
                            ts["tk"], ts["heads_per_step"])
        o_local = _local_mixers(z, w_pool[i].astype(BF16), pool_scale[i], conv_dw[i], conv_dw_bias[i],
                                conv_ln_gain[i], conv_ln_bias[i], w_conv_pw[i].astype(BF16),
                                seq, pool_col, ts["ts"])
        j = i // 2
        h1, u2 = _outproj(o_attn, o_local, w_out[i].astype(BF16), h, ffn_norm[i], ts["tm_out"],
                          BF16 if i % 2 == 0 else F32)
        if i % 2 == 0:
            casts = ()
            if i + 1 < depth:
                jn = (i + 1) // 2
                casts = ((w_exp_gate[jn], 2), (w_exp_up[jn], 2), (w_exp_down[jn], 1))
            y, experts_bf16 = _dense_ffn(u2, w_dense_gate[j].astype(BF16), w_dense_up[j].astype(BF16),
                                         w_dense_down[j].astype(BF16), ts["tm_ffn"], ts["tf_ffn"], casts)
            ys, route = [y], None
        else:
            if experts_bf16 is None:
                experts_bf16 = [w.astype(BF16) for w in (w_exp_gate[j], w_exp_up[j], w_exp_down[j])]
            route = _router(u2, w_router[j], ts["tm_out"])
            ys = _moe(u2, route, *experts_bf16, ts["tm_out"], ts["tm_moe"], ts["tf_moe"])
            experts_bf16 = None
        next_gain = final_norm if final else mix_norm[i + 1]
        res = _ple(h1, ys, route, ple_norm[i], w_ple_gate[i].astype(BF16), p[i].reshape(t, -1),
                   w_ple_proj[i].astype(BF16), next_gain, final, ts["tm_ple"])
        if final:
            out = res[0]
        else:
            h, u = res
    return out.reshape(batch, seq, d)
```

```python
import functools
import math

import jax
import jax.numpy as jnp
from jax import lax
from jax.experimental import pallas as pl
from jax.experimental.pallas import tpu as pltpu

F32 = jnp.float32
BF16 = jnp.bfloat16

EPS = 1e-6
SUBLN_EPS = 1e-5
LN_EPS = 1e-5
HEAD_DIM = 64
V_DIM = 2 * HEAD_DIM
ROPE_DIM = HEAD_DIM // 4
ROPE_HALF = ROPE_DIM // 2
ROPE_THETA = 500000.0
LOG2_E = math.log2(math.e)
POOL_WINDOWS = (2, 4, 8, 16)
POOL_GROUP_DIM = 128
CONV_KERNEL = 31
N_EXPERTS = 8
TOP_K = 2

LANES = 128
SUBLANES = 8
V7X_VMEM_BYTES = 64 * 1024 * 1024
VMEM_LIMIT = V7X_VMEM_BYTES - 2 * 1024 * 1024
POOL_HALO = 16
CONV_HALO = 32


def _cparams(sem):
    return pltpu.CompilerParams(dimension_semantics=sem, vmem_limit_bytes=VMEM_LIMIT)


def _resident(shape, index_map):
    return pl.BlockSpec(shape, index_map, pipeline_mode=pl.Buffered(1))


def _rms(x, gain, eps):
    return x * lax.rsqrt(jnp.mean(x * x, axis=-1, keepdims=True) + eps) * gain


def _rmsnorm_kernel(x_ref, g_ref, o_ref):
    o_ref[...] = _rms(x_ref[...], g_ref[...], EPS).astype(o_ref.dtype)


def _rmsnorm(x, gain, tm):
    t, d = x.shape
    return pl.pallas_call(
        _rmsnorm_kernel,
        grid=(t // tm,),
        in_specs=[pl.BlockSpec((tm, d), lambda i: (i, 0)), pl.BlockSpec((1, d), lambda i: (0, 0))],
        out_specs=pl.BlockSpec((tm, d), lambda i: (i, 0)),
        out_shape=jax.ShapeDtypeStruct((t, d), BF16),
        compiler_params=_cparams(("parallel",)),
        name="rmsnorm",
    )(x, gain.reshape(1, d))


def _inproj_kernel(u_ref, w_ref, c_ref, s_ref, swap_ref, z_ref, *, n_rot_blocks, n_q_blocks, q_scale):
    j = pl.program_id(1)
    u = u_ref[...]
    acc = jnp.dot(u, w_ref[...].astype(u.dtype), preferred_element_type=F32)
    tn = acc.shape[1]

    @pl.when(j < n_rot_blocks)
    def _():
        reps = tn // LANES
        c = jnp.concatenate([c_ref[...]] * reps, axis=1)
        s = jnp.concatenate([s_ref[...]] * reps, axis=1)
        pw = swap_ref.shape[0]
        ab = acc.astype(swap_ref.dtype)
        partner = jnp.concatenate(
            [jnp.dot(ab[:, b * pw:(b + 1) * pw], swap_ref[...], preferred_element_type=F32)
             for b in range(tn // pw)], axis=1)
        r = acc * c + partner * s
        scale = jnp.where(j < n_q_blocks, q_scale, 1.0).astype(F32)
        z_ref[...] = (r * scale).astype(z_ref.dtype)

    @pl.when(j >= n_rot_blocks)
    def _():
        z_ref[...] = acc.astype(z_ref.dtype)


def _inproj(u, w_all, layer, rot_c, rot_s, swap, qk_cols, tm, tn):
    t, d = u.shape
    n = w_all.shape[2]
    kern = functools.partial(_inproj_kernel, n_rot_blocks=2 * qk_cols // tn, n_q_blocks=qk_cols // tn,
                             q_scale=HEAD_DIM ** -0.5 * LOG2_E)
    tab = pl.BlockSpec((tm, LANES), lambda i, j: (i, 0))
    return pl.pallas_call(
        kern,
        grid=(t // tm, n // tn),
        in_specs=[pl.BlockSpec((tm, d), lambda i, j: (i, 0)),
                  pl.BlockSpec((None, d, tn), lambda i, j: (layer, 0, j)),
                  tab, tab, _resident(swap.shape, lambda i, j: (0, 0))],
        out_specs=pl.BlockSpec((tm, tn), lambda i, j: (i, j)),
        out_shape=jax.ShapeDtypeStruct((t, n), BF16),
        compiler_params=_cparams(("parallel", "arbitrary")),
        name="inproj",
    )(u, w_all, rot_c, rot_s, swap)


def _attn_kernel(lam_ref, gain_ref, q_ref, k_ref, v_ref, o_ref, qs_ref, vx_ref, m_ref, acc_ref, sa_ref,
                 sb_ref, *, tq, tk, hp, lam_init):
    qi = pl.program_id(2)
    per = tk // tq

    @pl.when(qi == 0)
    def _():
        for a in range(hp):
            vx_ref[a, :, 0:V_DIM] = v_ref[:, a * V_DIM:(a + 1) * V_DIM]
            vx_ref[a, :, V_DIM:2 * V_DIM] = jnp.ones((v_ref.shape[0], V_DIM), v_ref.dtype)

    lane = lax.broadcasted_iota(jnp.int32, (tq, LANES), 1)
    for a in range(hp):
        q = q_ref[:, a * LANES:(a + 1) * LANES]
        zero = jnp.zeros_like(q)
        qs_ref[a, 0:tq, :] = jnp.where(lane < HEAD_DIM, q, zero)
        qs_ref[a, tq:2 * tq, :] = jnp.where(lane >= HEAD_DIM, q, zero)
    m_ref[...] = jnp.full(m_ref.shape, -jnp.inf, F32)
    acc_ref[...] = jnp.zeros(acc_ref.shape, F32)

    def scores(j, s_ref):
        start = pl.multiple_of(j * tk, tk)
        for a in range(hp):
            k = k_ref[pl.ds(start, tk), a * LANES:(a + 1) * LANES]
            s_ref[a] = lax.dot_general(qs_ref[a], k, (((1,), (1,)), ((), ())),
                                       preferred_element_type=F32)

    def update(j, s_ref, width, diag_col=None):
        start = pl.multiple_of(j * tk, tk)
        for a in range(hp):
            vx = vx_ref[a, pl.ds(start, width), :]
            s = s_ref[a, :, 0:width]
            if diag_col is not None:
                row = lax.broadcasted_iota(jnp.int32, s.shape, 0)
                col = lax.broadcasted_iota(jnp.int32, s.shape, 1)
                s = jnp.where(col <= (row & (tq - 1)) + diag_col, s, -jnp.inf)
            m_old = m_ref[a]
            m_new = jnp.maximum(m_old, jnp.max(s, axis=-1, keepdims=True))
            alpha = jnp.exp2(m_old - m_new)
            p = jnp.exp2(s - jnp.concatenate([m_new] * (width // LANES), axis=1))
            pv = jnp.dot(p.astype(vx.dtype), vx, preferred_element_type=F32)
            acc_ref[a] = jnp.concatenate([alpha, alpha], axis=1) * acc_ref[a] + pv
            m_ref[a] = m_new

    n_full = qi // per
    scores(0, sa_ref)

    def body(j, carry):
        def even():
            scores(j + 1, sb_ref)
            update(j, sa_ref, tk)

        def odd():
            scores(j + 1, sa_ref)
            update(j, sb_ref, tk)

        lax.cond((j & 1) == 0, even, odd)
        return carry

    lax.fori_loop(0, n_full, body, 0)

    def diagonal_block(s_ref):
        for r in range(per):
            @pl.when(qi % per == r)
            def _():
                update(n_full, s_ref, (r + 1) * tq, r * tq)

    lax.cond((n_full & 1) == 0, lambda: diagonal_block(sa_ref), lambda: diagonal_block(sb_ref))

    lp = lam_ref[...]
    lam = (jnp.exp(jnp.sum(lp[0:1] * lp[1:2], axis=-1, keepdims=True))
           - jnp.exp(jnp.sum(lp[2:3] * lp[3:4], axis=-1, keepdims=True)) + lam_init)
    for a in range(hp):
        o_all = acc_ref[a, :, 0:V_DIM] / acc_ref[a, :, V_DIM:2 * V_DIM]
        o = o_all[0:tq] - lam * o_all[tq:2 * tq]
        y = _rms(o, gain_ref[...], SUBLN_EPS) * (1.0 - lam_init)
        o_ref[:, a * V_DIM:(a + 1) * V_DIM] = y.astype(o_ref.dtype)


def _attention(z, lam_params, subln_gain, batch, seq, n_heads, lam_init, tq, tk, hp):
    t = z.shape[0]
    nq = seq // tq
    assert tq & (tq - 1) == 0 and tk % tq == 0 and seq % tk == 0 and n_heads % hp == 0
    kern = functools.partial(_attn_kernel, tq=tq, tk=tk, hp=hp, lam_init=lam_init)
    ng = n_heads // hp
    w = hp * LANES
    return pl.pallas_call(
        kern,
        grid=(batch, ng, nq),
        in_specs=[pl.BlockSpec(lam_params.shape, lambda b, h, i: (0, 0)),
                  pl.BlockSpec((1, V_DIM), lambda b, h, i: (0, 0)),
                  pl.BlockSpec((tq, w), lambda b, h, i: (b * nq + i, h)),
                  pl.BlockSpec((seq, w), lambda b, h, i: (b, ng + h)),
                  pl.BlockSpec((seq, w), lambda b, h, i: (b, 2 * ng + h))],
        out_specs=pl.BlockSpec((tq, w), lambda b, h, i: (b * nq + i, h)),
        out_shape=jax.ShapeDtypeStruct((t, n_heads * V_DIM), BF16),
        scratch_shapes=[pltpu.VMEM((hp, 2 * tq, LANES), BF16),
                        pltpu.VMEM((hp, seq, 2 * V_DIM), BF16),
                        pltpu.VMEM((hp, 2 * tq, LANES), F32),
                        pltpu.VMEM((hp, 2 * tq, 2 * V_DIM), F32),
                        pltpu.VMEM((hp, 2 * tq, tk), F32),
                        pltpu.VMEM((hp, 2 * tq, tk), F32)],
        compiler_params=_cparams(("parallel", "parallel", "arbitrary")),
        name="diff_attention",
    )(lam_params, subln_gain.reshape(1, V_DIM), z, z, z)


def _local_kernel(zp_ref, zph_ref, za_ref, zah_ref, zb_ref, zbh_ref, wpool_ref, pscale_ref,
                  dw_ref, dwb_ref, lng_ref, lnb_ref, wpw_ref, o_ref, pbuf_ref, cbuf_ref, sbuf_ref,
                  *, ts, tiles_per_seq):
    i = pl.program_id(0)
    tile_in_seq = i % tiles_per_seq
    first = tile_in_seq == 0
    pool_w = zp_ref.shape[1]

    zp = zp_ref[...].astype(F32)
    halo = zph_ref[...].astype(F32)
    pbuf_ref[0:POOL_HALO, :] = jnp.where(first, jnp.zeros_like(halo), halo)
    pbuf_ref[POOL_HALO:POOL_HALO + ts, :] = zp
    pos = tile_in_seq * ts + lax.broadcasted_iota(jnp.int32, (ts, 1), 0)
    for g, w in enumerate(POOL_WINDOWS):
        cols = slice(g * POOL_GROUP_DIM, (g + 1) * POOL_GROUP_DIM)
        zg = zp[:, cols]
        wsum = zg
        for k in range(1, w):
            wsum = wsum + pbuf_ref[POOL_HALO - k:POOL_HALO - k + ts, cols]
        count = jnp.minimum(pos + 1, w).astype(F32)
        d = (wsum / count - zg).astype(BF16)
        y = jnp.dot(d, wpool_ref[g], preferred_element_type=F32) * pscale_ref[:, cols]
        o_ref[:, cols] = y.astype(o_ref.dtype)

    c = za_ref[...].astype(F32) * jax.nn.sigmoid(zb_ref[...].astype(F32))
    ch = zah_ref[...].astype(F32) * jax.nn.sigmoid(zbh_ref[...].astype(F32))
    cbuf_ref[0:CONV_HALO, :] = jnp.where(first, jnp.zeros_like(ch), ch)
    cbuf_ref[CONV_HALO:CONV_HALO + ts, :] = c
    acc = jnp.zeros_like(c) + dwb_ref[...]
    base = CONV_HALO - (CONV_KERNEL - 1)
    for b in range(SUBLANES):
        offs = [o for o in range(base, base + CONV_KERNEL) if o % SUBLANES == b]
        if not offs:
            continue
        span = offs[-1] - offs[0] + ts
        sbuf_ref[0:span, :] = cbuf_ref[offs[0]:offs[0] + span, :]
        for o in offs:
            k = o - base
            acc = acc + dw_ref[k:k + 1, :] * sbuf_ref[o - offs[0]:o - offs[0] + ts, :]
    mu = jnp.mean(acc, axis=-1, keepdims=True)
    xc = acc - mu
    yn = xc * lax.rsqrt(jnp.mean(xc * xc, axis=-1, keepdims=True) + LN_EPS) * lng_ref[...] + lnb_ref[...]
    sw = yn * jax.nn.sigmoid(yn)
    out = jnp.dot(sw.astype(BF16), wpw_ref[...], preferred_element_type=F32)
    o_ref[:, pool_w:] = out.astype(o_ref.dtype)


def _local_mixers(z, w_pool, pool_scale, dw, dw_b, ln_g, ln_b, w_pw, seq, pool_col, ts):
    t = z.shape[0]
    pool_w = pool_scale.shape[0]
    conv_w = dw.shape[1]
    assert pool_w == conv_w and pool_col % pool_w == 0
    pc = pool_col // pool_w
    hp = ts // POOL_HALO
    hc = ts // CONV_HALO
    kern = functools.partial(_local_kernel, ts=ts, tiles_per_seq=seq // ts)
    row = lambda shape: _resident(shape, lambda i: (0,) * len(shape))
    return pl.pallas_call(
        kern,
        grid=(t // ts,),
        in_specs=[pl.BlockSpec((ts, pool_w), lambda i: (i, pc)),
                  pl.BlockSpec((POOL_HALO, pool_w), lambda i: (jnp.maximum(i * hp - 1, 0), pc)),
                  pl.BlockSpec((ts, conv_w), lambda i: (i, pc + 1)),
                  pl.BlockSpec((CONV_HALO, conv_w), lambda i: (jnp.maximum(i * hc - 1, 0), pc + 1)),
                  pl.BlockSpec((ts, conv_w), lambda i: (i, pc + 2)),
                  pl.BlockSpec((CONV_HALO, conv_w), lambda i: (jnp.maximum(i * hc - 1, 0), pc + 2)),
                  row(w_pool.shape), row((1, pool_w)), row(dw.shape), row((1, conv_w)),
                  row((1, conv_w)), row((1, conv_w)), row(w_pw.shape)],
        out_specs=pl.BlockSpec((ts, pool_w + conv_w), lambda i: (i, 0)),
        out_shape=jax.ShapeDtypeStruct((t, pool_w + conv_w), BF16),
        scratch_shapes=[pltpu.VMEM((POOL_HALO + ts, pool_w), F32),
                        pltpu.VMEM((CONV_HALO + ts, conv_w), F32),
                        pltpu.VMEM((CONV_HALO + ts, conv_w), F32)],
        compiler_params=_cparams(("parallel",)),
        name="local_mixers",
    )(z, z, z, z, z, z, w_pool, pool_scale.reshape(1, pool_w), dw, dw_b.reshape(1, conv_w),
      ln_g.reshape(1, conv_w), ln_b.reshape(1, conv_w), w_pw)


def _top2_route(u, w_router_padded):
    u_hi = u.astype(BF16)
    u_lo = (u - u_hi.astype(F32)).astype(BF16)
    w_hi = w_router_padded.astype(BF16)
    w_lo = (w_router_padded - w_hi.astype(F32)).astype(BF16)
    logits = (jnp.dot(u_hi, w_hi, preferred_element_type=F32)
              + (jnp.dot(u_hi, w_lo, preferred_element_type=F32)
                 + jnp.dot(u_lo, w_hi, preferred_element_type=F32)))
    lane = lax.broadcasted_iota(jnp.int32, logits.shape, 1)
    neg = jnp.full(logits.shape, -jnp.inf, F32)
    lg = jnp.where(lane < N_EXPERTS, logits, neg)
    m1 = jnp.max(lg, axis=-1, keepdims=True)
    i1 = jnp.min(jnp.where(lg == m1, lane, LANES), axis=-1, keepdims=True)
    lg2 = jnp.where(lane == i1, neg, lg)
    m2 = jnp.max(lg2, axis=-1, keepdims=True)
    i2 = jnp.min(jnp.where(lg2 == m2, lane, LANES), axis=-1, keepdims=True)
    e2 = jnp.exp(m2 - m1)
    g1 = 1.0 / (1.0 + e2)
    g2 = e2 / (1.0 + e2)
    return jnp.where(lane == 0, g1, jnp.where(lane == 1, g2, jnp.where(
        lane == 2, i1.astype(F32), jnp.where(lane == 3, i2.astype(F32), 0.0))))


def _router_kernel(u_ref, w_ref, o_ref):
    o_ref[...] = _top2_route(u_ref[...], w_ref[...])


def _router(u2, w_router, tm):
    t, d = u2.shape
    w = jnp.zeros((d, LANES), F32).at[:, :N_EXPERTS].set(w_router)
    return pl.pallas_call(
        _router_kernel,
        grid=(t // tm,),
        in_specs=[pl.BlockSpec((tm, d), lambda i: (i, 0)),
                  _resident((d, LANES), lambda i: (0, 0))],
        out_specs=pl.BlockSpec((tm, LANES), lambda i: (i, 0)),
        out_shape=jax.ShapeDtypeStruct((t, LANES), F32),
        compiler_params=_cparams(("parallel",)),
        name="router",
    )(u2, w)


def _outproj_kernel(oa_ref, ol_ref, w_ref, h_ref, g_ref, h1_ref, u_ref):
    ka = oa_ref.shape[1]
    mix = jnp.dot(oa_ref[...], w_ref[0:ka, :], preferred_element_type=F32)
    mix = mix + jnp.dot(ol_ref[...], w_ref[ka:, :], preferred_element_type=F32)
    h1 = h_ref[...] + mix
    h1_ref[...] = h1
    u_ref[...] = _rms(h1, g_ref[...], EPS).astype(u_ref.dtype)


def _outproj(o_attn, o_local, w_out, h, gain, tm, u_dtype):
    t, d = h.shape
    ka, kl = o_attn.shape[1], o_local.shape[1]
    blk = pl.BlockSpec((tm, d), lambda i: (i, 0))
    return pl.pallas_call(
        _outproj_kernel,
        grid=(t // tm,),
        in_specs=[pl.BlockSpec((tm, ka), lambda i: (i, 0)),
                  pl.BlockSpec((tm, kl), lambda i: (i, 0)),
                  _resident((ka + kl, d), lambda i: (0, 0)),
                  blk,
                  _resident((1, d), lambda i: (0, 0))],
        out_specs=[blk, blk],
        out_shape=[jax.ShapeDtypeStruct((t, d), F32), jax.ShapeDtypeStruct((t, d), u_dtype)],
        compiler_params=_cparams(("parallel",)),
        name="outproj",
    )(o_attn, o_local, w_out, h, gain.reshape(1, d))


def _swiglu_partial(x, wg, wu, wd):
    g = jnp.dot(x, wg, preferred_element_type=F32)
    u = jnp.dot(x, wu, preferred_element_type=F32)
    hidden = (g * jax.nn.sigmoid(g)) * u
    return jnp.dot(hidden.astype(wd.dtype), wd, preferred_element_type=F32)


def _dense_ffn_kernel(*refs, n_cast):
    u_ref, wg_ref, wu_ref, wd_ref = refs[:4]
    cast_in = refs[4:4 + n_cast]
    y_ref = refs[4 + n_cast]
    cast_out = refs[5 + n_cast:5 + 2 * n_cast]
    acc_ref = refs[5 + 2 * n_cast]
    f = pl.program_id(1)

    @pl.when(f == 0)
    def _():
        acc_ref[...] = jnp.zeros(acc_ref.shape, F32)

    acc_ref[...] += _swiglu_partial(u_ref[...], wg_ref[...], wu_ref[...], wd_ref[...])

    for src, dst in zip(cast_in, cast_out):
        dst[...] = src[...].astype(dst.dtype)

    @pl.when(f == pl.num_programs(1) - 1)
    def _():
        y_ref[...] = acc_ref[...].astype(y_ref.dtype)


def _cast_slices(arrays_axes, n_steps, nf):
    specs = []
    for arr, axis in arrays_axes:
        e = arr.shape[0]
        if n_steps % e:
            return None
        per = n_steps // e
        if arr.shape[axis] % per:
            return None
        width = arr.shape[axis] // per
        align = LANES if axis == 2 else 16
        if width % align:
            return None
        block = tuple(None if a == 0 else (width if a == axis else arr.shape[a]) for a in range(3))

        def imap(i, f, per=per, axis=axis):
            s = i * nf + f
            return (s // per, s % per, 0) if axis == 1 else (s // per, 0, s % per)

        specs.append(pl.BlockSpec(block, imap))
    return specs


def _dense_ffn(u, wg, wu, wd, tm, tf, casts=()):
    t, d = u.shape
    ff = wg.shape[1]
    grid = (t // tm, ff // tf)
    cast_specs = _cast_slices(casts, grid[0] * grid[1], grid[1]) if casts else []
    if cast_specs is None:
        cast_specs, casts, unsupported = [], (), True
    else:
        unsupported = False
    n_cast = len(cast_specs)
    outs = pl.pallas_call(
        functools.partial(_dense_ffn_kernel, n_cast=n_cast),
        grid=grid,
        in_specs=[pl.BlockSpec((tm, d), lambda i, f: (i, 0)),
                  pl.BlockSpec((d, tf), lambda i, f: (0, f)),
                  pl.BlockSpec((d, tf), lambda i, f: (0, f)),
                  pl.BlockSpec((tf, d), lambda i, f: (f, 0))] + cast_specs,
        out_specs=[pl.BlockSpec((tm, d), lambda i, f: (i, 0))] + cast_specs,
        out_shape=[jax.ShapeDtypeStruct((t, d), BF16)]
        + [jax.ShapeDtypeStruct(a.shape, BF16) for a, _ in casts],
        scratch_shapes=[pltpu.VMEM((tm, d), F32)],
        compiler_params=_cparams(("parallel", "arbitrary")),
        name="dense_ffn",
    )(u, wg, wu, wd, *[a for a, _ in casts])
    return outs[0], (None if unsupported else list(outs[1:]))


def _dispatch_kernel(pend_ref, padded_ref, dest_ref, x_ref, o_ref, zero_ref, row_sem, zero_sem,
                     *, tile, n_tiles):
    rows = x_ref.shape[0]

    @pl.when(pl.program_id(0) == 0)
    def _():
        zero_ref[...] = jnp.zeros(zero_ref.shape, zero_ref.dtype)

        def zero_tile(first_row):
            return pltpu.make_async_copy(zero_ref, o_ref.at[pl.ds(first_row, tile)], zero_sem)

        def last_tile_of(e):
            return zero_tile(pl.multiple_of(pend_ref[e] - tile, tile))

        trailing = range(n_tiles - N_EXPERTS, n_tiles)
        for phase in ("start", "wait"):
            for e in range(N_EXPERTS):
                @pl.when(padded_ref[e] > 0)
                def _():
                    getattr(last_tile_of(e), phase)()
            for tl in trailing:
                @pl.when(tl * tile >= pend_ref[N_EXPERTS - 1])
                def _():
                    getattr(zero_tile(tl * tile), phase)()

    def row_copy(r, k):
        row = dest_ref[0, 0, r * TOP_K + k]
        return pltpu.make_async_copy(x_ref.at[pl.ds(r, 1)], o_ref.at[pl.ds(row, 1)], row_sem)

    def issue(r, carry):
        for k in range(TOP_K):
            row_copy(r, k).start()
        return carry

    def drain(r, carry):
        for k in range(TOP_K):
            row_copy(r, k).wait()
        return carry

    lax.fori_loop(0, rows, issue, 0, unroll=8)
    lax.fori_loop(0, rows, drain, 0, unroll=8)


def _dispatch(u2, dest, pend, padded, tile, n_tiles, tm):
    t, d = u2.shape
    dest3 = dest.reshape(t // tm, 1, TOP_K * tm)
    grid_spec = pltpu.PrefetchScalarGridSpec(
        num_scalar_prefetch=2,
        grid=(t // tm,),
        in_specs=[pl.BlockSpec((1, 1, TOP_K * tm), lambda i, pe, pa: (i, 0, 0), memory_space=pltpu.SMEM),
                  pl.BlockSpec((tm, d), lambda i, pe, pa: (i, 0))],
        out_specs=pl.BlockSpec(memory_space=pl.ANY),
        scratch_shapes=[pltpu.VMEM((tile, d), u2.dtype),
                        pltpu.SemaphoreType.DMA(()),
                        pltpu.SemaphoreType.DMA(())],
    )
    return pl.pallas_call(
        functools.partial(_dispatch_kernel, tile=tile, n_tiles=n_tiles),
        grid_spec=grid_spec,
        out_shape=jax.ShapeDtypeStruct((n_tiles * tile, d), u2.dtype),
        compiler_params=_cparams(("arbitrary",)),
        name="moe_dispatch",
    )(pend, padded, dest3, u2)


def _moe_ffn_kernel(te_ref, nu_ref, tr_ref, x_ref, wg_ref, wu_ref, wd_ref, y_ref, acc_ref, xb_ref):
    i = pl.program_id(0)
    f = pl.program_id(1)
    half = x_ref.shape[0] // 2

    @pl.when(i < nu_ref[0])
    def _():
        @pl.when(f == 0)
        def _():
            acc_ref[...] = jnp.zeros(acc_ref.shape, F32)
            xb_ref[...] = x_ref[...].astype(xb_ref.dtype)

        @pl.when(tr_ref[i] > half)
        def _():
            acc_ref[...] += _swiglu_partial(xb_ref[...], wg_ref[...], wu_ref[...], wd_ref[...])

        @pl.when(tr_ref[i] <= half)
        def _():
            acc_ref[0:half, :] += _swiglu_partial(xb_ref[0:half, :], wg_ref[...], wu_ref[...], wd_ref[...])

        @pl.when(f == pl.num_programs(1) - 1)
        def _():
            y_ref[...] = acc_ref[...].astype(y_ref.dtype)

    @pl.when(i >= nu_ref[0])
    def _():
        y_ref[...] = jnp.zeros(y_ref.shape, y_ref.dtype)


def _moe_ffn(x_sorted, tile_expert, n_used, tile_rows, wg, wu, wd, tm, tf):
    p, d = x_sorted.shape
    ff = wg.shape[2]
    nf = ff // tf

    def row(i, f, te, nu, tr):
        return (jnp.minimum(i, nu[0] - 1), 0)

    def fcol(i, f, nu):
        return jnp.where(i < nu[0], f, nf - 1)

    grid_spec = pltpu.PrefetchScalarGridSpec(
        num_scalar_prefetch=3,
        grid=(p // tm, nf),
        in_specs=[pl.BlockSpec((tm, d), row),
                  pl.BlockSpec((None, d, tf), lambda i, f, te, nu, tr: (te[i], 0, fcol(i, f, nu))),
                  pl.BlockSpec((None, d, tf), lambda i, f, te, nu, tr: (te[i], 0, fcol(i, f, nu))),
                  pl.BlockSpec((None, tf, d), lambda i, f, te, nu, tr: (te[i], fcol(i, f, nu), 0))],
        out_specs=pl.BlockSpec((tm, d), lambda i, f, te, nu, tr: (i, 0)),
        scratch_shapes=[pltpu.VMEM((tm, d), F32), pltpu.VMEM((tm, d), wg.dtype)],
    )
    return pl.pallas_call(
        _moe_ffn_kernel,
        grid_spec=grid_spec,
        out_shape=jax.ShapeDtypeStruct((p, d), BF16),
        compiler_params=_cparams(("arbitrary", "arbitrary")),
        name="moe_ffn",
    )(tile_expert, n_used, tile_rows, x_sorted, wg, wu, wd)


def _moe(u2, route, wg, wu, wd, tm_route, tm, tf):
    t, d = u2.shape
    expert = route[:, 2:2 + TOP_K].astype(jnp.int32).reshape(-1)
    onehot = (expert[:, None] == jnp.arange(N_EXPERTS, dtype=jnp.int32)[None, :]).astype(jnp.int32)
    csum = jnp.cumsum(onehot, axis=0)
    rank = jnp.sum(csum * onehot, axis=1) - 1
    counts = csum[-1]
    padded = ((counts + tm - 1) // tm) * tm
    pend = jnp.cumsum(padded)
    pstart = pend - padded
    dest = pstart[expert] + rank
    n_tiles = (TOP_K * t) // tm + N_EXPERTS
    tile_start = jnp.arange(n_tiles, dtype=jnp.int32) * tm
    tile_expert = jnp.minimum(jnp.sum((tile_start[:, None] >= pend[None, :]).astype(jnp.int32), axis=1),
                              N_EXPERTS - 1).astype(jnp.int32)
    n_used = (pend[-1] // tm).astype(jnp.int32).reshape(1)
    x_sorted = _dispatch(u2, dest.astype(jnp.int32), pend.astype(jnp.int32), padded.astype(jnp.int32),
                         tm, n_tiles, tm_route)
    tile_rows = jnp.clip(counts[tile_expert] - (tile_start - pstart[tile_expert]), 0, tm).astype(jnp.int32)
    y_sorted = _moe_ffn(x_sorted, tile_expert, n_used, tile_rows, wg, wu, wd, tm, tf)
    dest2 = dest.reshape(t, TOP_K)
    return [y_sorted.at[dest2[:, k]].get(mode="promise_in_bounds") for k in range(TOP_K)]


def _ple_kernel(*refs, n_y, gated, final):
    h_ref, y_refs = refs[0], refs[1:1 + n_y]
    rest = refs[1 + n_y:]
    if gated:
        r = rest[0][...]
        rest = rest[1:]
    gp_ref, wg_ref, p_ref, wp_ref, gn_ref = rest[:5]
    out_refs = rest[5:]
    h2 = h_ref[...]
    for k, y_ref in enumerate(y_refs):
        y = y_ref[...].astype(F32)
        h2 = h2 + (r[:, k:k + 1] * y if gated else y)
    u3 = _rms(h2, gp_ref[...], EPS).astype(BF16)
    gate = jax.nn.sigmoid(jnp.dot(u3, wg_ref[...], preferred_element_type=F32))
    emb = jnp.dot(p_ref[...].astype(BF16), wp_ref[...], preferred_element_type=F32)
    h3 = h2 + emb * gate
    if final:
        out_refs[0][...] = _rms(h3, gn_ref[...], EPS)
    else:
        out_refs[0][...] = h3
        out_refs[1][...] = _rms(h3, gn_ref[...], EPS).astype(out_refs[1].dtype)


def _ple(h1, ys, route, ple_gain, w_gate, p, w_proj, next_gain, final, tm):
    t, d = h1.shape
    pd = p.shape[1]
    n_y = len(ys)
    blk = pl.BlockSpec((tm, d), lambda i: (i, 0))
    vec = _resident((1, d), lambda i: (0, 0))
    gated = route is not None
    in_specs = [blk] * (1 + n_y)
    args = [h1] + list(ys)
    if gated:
        in_specs.append(pl.BlockSpec((tm, LANES), lambda i: (i, 0)))
        args.append(route)
    in_specs += [vec, _resident((d, d), lambda i: (0, 0)), pl.BlockSpec((tm, pd), lambda i: (i, 0)),
                 _resident((pd, d), lambda i: (0, 0)), vec]
    args += [ple_gain.reshape(1, d), w_gate, p, w_proj, next_gain.reshape(1, d)]
    if final:
        out_specs = [blk]
        out_shape = [jax.ShapeDtypeStruct((t, d), F32)]
    else:
        out_specs = [blk, blk]
        out_shape = [jax.ShapeDtypeStruct((t, d), F32), jax.ShapeDtypeStruct((t, d), BF16)]
    return pl.pallas_call(
        functools.partial(_ple_kernel, n_y=n_y, gated=gated, final=final),
        grid=(t // tm,),
        in_specs=in_specs,
        out_specs=out_specs,
        out_shape=out_shape,
        compiler_params=_cparams(("parallel",)),
        name="ple",
    )(*args)


def _tiles(t, seq, ff):
    return dict(
        tm_norm=min(512, t),
        tm_in=min(2048, t), tn_in=512,
        tq=min(512, seq), tk=min(1024, seq), heads_per_step=2,
        ts=min(512, seq),
        tm_out=min(512, t),
        tm_ffn=min(1024, t), tf_ffn=min(512, ff),
        tm_moe=min(512, t), tf_moe=min(1024, ff),
        tm_ple=min(512, t),
    )


def _rotary_tables(positions):
    inv_freq = ROPE_THETA ** (-jnp.arange(0, ROPE_DIM, 2, dtype=F32) / ROPE_DIM)
    ang = positions.astype(F32).reshape(-1, 1) * inv_freq
    cos, sin = lax.optimization_barrier((jnp.cos(ang), jnp.sin(ang)))
    cos_rep = jnp.tile(cos, (1, LANES // ROPE_HALF))
    sin_rep = jnp.tile(sin, (1, LANES // ROPE_HALF))
    dim128 = (jnp.arange(LANES, dtype=jnp.int32) % HEAD_DIM)[None, :]
    c = jnp.where(dim128 < ROPE_DIM, cos_rep, 1.0)
    s = jnp.where(dim128 < ROPE_HALF, -sin_rep, jnp.where(dim128 < ROPE_DIM, sin_rep, 0.0))
    lane = jnp.arange(2 * LANES, dtype=jnp.int32)
    dim = lane % HEAD_DIM
    src = jnp.where(dim < ROPE_HALF, lane + ROPE_HALF, jnp.where(dim < ROPE_DIM, lane - ROPE_HALF, lane))
    swap = (lane[:, None] == src[None, :]).astype(BF16)
    return c, s, swap


def kernel(x, p, positions, mix_norm, w_in, lambda_q1, lambda_k1, lambda_q2, lambda_k2, subln_gain, w_pool, pool_scale, conv_dw, conv_dw_bias, conv_ln_gain, conv_ln_bias, w_conv_pw, w_out, ffn_norm, w_dense_gate, w_dense_up, w_dense_down, w_router, w_exp_gate, w_exp_up, w_exp_down, ple_norm, w_ple_gate, w_ple_proj, final_norm):
    batch, seq, d = x.shape
    depth = w_in.shape[0]
    t = batch * seq
    attn_w = (w_in.shape[2] - 2 * conv_dw.shape[2] - pool_scale.shape[1]) // 3
    n_heads = attn_w // V_DIM
    qk_cols = attn_w
    pool_col = 3 * attn_w
    ff = w_dense_gate.shape[2]
    ts = _tiles(t, seq, ff)

    rot_c, rot_s, rot_swap = _rotary_tables(positions)
    h = x.reshape(t, d)
    u = _rmsnorm(h, mix_norm[0], ts["tm_norm"])
    out = None
    experts_bf16 = None
    for i in range(depth):
        lam_init = 0.8 - 0.6 * math.exp(-0.3 * i)
        final = i == depth - 1
        z = _inproj(u, w_in, i, rot_c, rot_s, rot_swap, qk_cols, ts["tm_in"], ts["tn_in"])
        lam_params = jnp.stack([lambda_q1[i], lambda_k1[i], lambda_q2[i], lambda_k2[i]]).astype(F32)
        o_attn = _attention(z, lam_params, subln_gain[i], batch, seq, n_heads, lam_init, ts["tq"],
                            ts["tk"], ts["heads_per_step"])
        o_local = _local_mixers(z, w_pool[i].astype(BF16), pool_scale[i], conv_dw[i], conv_dw_bias[i],
                                conv_ln_gain[i], conv_ln_bias[i], w_conv_pw[i].astype(BF16),
                                seq, pool_col, ts["ts"])
        j = i // 2
        h1, u2 = _outproj(o_attn, o_local, w_out[i].astype(BF16), h, ffn_norm[i], ts["tm_out"],
                          BF16 if i % 2 == 0 else F32)
        if i % 2 == 0:
            casts = ()
            if i + 1 < depth:
                jn = (i + 1) // 2
                casts = ((w_exp_gate[jn], 2), (w_exp_up[jn], 2), (w_exp_down[jn], 1))
            y, experts_bf16 = _dense_ffn(u2, w_dense_gate[j].astype(BF16), w_dense_up[j].astype(BF16),
                                         w_dense_down[j].astype(BF16), ts["tm_ffn"], ts["tf_ffn"], casts)
            ys, route = [y], None
        else:
            if experts_bf16 is None:
                experts_bf16 = [w.astype(BF16) for w in (w_exp_gate[j], w_exp_up[j], w_exp_down[j])]
            route = _router(u2, w_router[j], ts["tm_out"])
            ys = _moe(u2, route, *experts_bf16, ts["tm_out"], ts["tm_moe"], ts["tf_moe"])
            experts_bf16 = None
        next_gain = final_norm if final else mix_norm[i + 1]
        res = _ple(h1, ys, route, ple_norm[i], w_ple_gate[i].astype(BF16), p[i].reshape(t, -1),
                   w_ple_proj[i].astype(BF16), next_gain, final, ts["tm_ple"])
        if final:
            out = res[0]
        else:
            h, u = res
    return out.reshape(batch, seq, d)
```

```python
import functools
import math

import jax
import jax.numpy as jnp
from jax import lax
from jax.experimental import pallas as pl
from jax.experimental.pallas import tpu as pltpu

F32 = jnp.float32
BF16 = jnp.bfloat16

EPS = 1e-6
SUBLN_EPS = 1e-5
LN_EPS = 1e-5
HEAD_DIM = 64
V_DIM = 2 * HEAD_DIM
ROPE_DIM = HEAD_DIM // 4
ROPE_HALF = ROPE_DIM // 2
ROPE_THETA = 500000.0
LOG2_E = math.log2(math.e)
POOL_WINDOWS = (2, 4, 8, 16)
POOL_GROUP_DIM = 128
CONV_KERNEL = 31
N_EXPERTS = 8
TOP_K = 2

LANES = 128
SUBLANES = 8
V7X_VMEM_BYTES = 64 * 1024 * 1024
VMEM_LIMIT = V7X_VMEM_BYTES - 2 * 1024 * 1024
POOL_HALO = 16
CONV_HALO = 32


def _cparams(sem):
    return pltpu.CompilerParams(dimension_semantics=sem, vmem_limit_bytes=VMEM_LIMIT)


def _resident(shape, index_map):
    return pl.BlockSpec(shape, index_map, pipeline_mode=pl.Buffered(1))


def _rms(x, gain, eps):
    return x * lax.rsqrt(jnp.mean(x * x, axis=-1, keepdims=True) + eps) * gain


def _rmsnorm_kernel(x_ref, g_ref, o_ref):
    o_ref[...] = _rms(x_ref[...], g_ref[...], EPS).astype(o_ref.dtype)


def _rmsnorm(x, gain, tm):
    t, d = x.shape
    return pl.pallas_call(
        _rmsnorm_kernel,
        grid=(t // tm,),
        in_specs=[pl.BlockSpec((tm, d), lambda i: (i, 0)), pl.BlockSpec((1, d), lambda i: (0, 0))],
        out_specs=pl.BlockSpec((tm, d), lambda i: (i, 0)),
        out_shape=jax.ShapeDtypeStruct((t, d), BF16),
        compiler_params=_cparams(("parallel",)),
        name="rmsnorm",
    )(x, gain.reshape(1, d))


def _inproj_kernel(u_ref, w_ref, c_ref, s_ref, swap_ref, z_ref, *, n_rot_blocks, n_q_blocks, q_scale):
    j = pl.program_id(1)
    u = u_ref[...]
    acc = jnp.dot(u, w_ref[...].astype(u.dtype), preferred_element_type=F32)
    tn = acc.shape[1]

    @pl.when(j < n_rot_blocks)
    def _():
        reps = tn // LANES
        c = jnp.concatenate([c_ref[...]] * reps, axis=1)
        s = jnp.concatenate([s_ref[...]] * reps, axis=1)
        pw = swap_ref.shape[0]
        ab = acc.astype(swap_ref.dtype)
        partner = jnp.concatenate(
            [jnp.dot(ab[:, b * pw:(b + 1) * pw], swap_ref[...], preferred_element_type=F32)
             for b in range(tn // pw)], axis=1)
        r = acc * c + partner * s
        scale = jnp.where(j < n_q_blocks, q_scale, 1.0).astype(F32)
        z_ref[...] = (r * scale).astype(z_ref.dtype)

    @pl.when(j >= n_rot_blocks)
    def _():
        z_ref[...] = acc.astype(z_ref.dtype)


def _inproj(u, w_all, layer, rot_c, rot_s, swap, qk_cols, tm, tn):
    t, d = u.shape
    n = w_all.shape[2]
    kern = functools.partial(_inproj_kernel, n_rot_blocks=2 * qk_cols // tn, n_q_blocks=qk_cols // tn,
                             q_scale=HEAD_DIM ** -0.5 * LOG2_E)
    tab = pl.BlockSpec((tm, LANES), lambda i, j: (i, 0))
    return pl.pallas_call(
        kern,
        grid=(t // tm, n // tn),
        in_specs=[pl.BlockSpec((tm, d), lambda i, j: (i, 0)),
                  pl.BlockSpec((None, d, tn), lambda i, j: (layer, 0, j)),
                  tab, tab, _resident(swap.shape, lambda i, j: (0, 0))],
        out_specs=pl.BlockSpec((tm, tn), lambda i, j: (i, j)),
        out_shape=jax.ShapeDtypeStruct((t, n), BF16),
        compiler_params=_cparams(("parallel", "arbitrary")),
        name="inproj",
    )(u, w_all, rot_c, rot_s, swap)


def _attn_kernel(lam_ref, gain_ref, q_ref, k_ref, v_ref, o_ref, qs_ref, vx_ref, m_ref, acc_ref, sa_ref,
                 sb_ref, *, tq, tk, hp, lam_init):
    qi = pl.program_id(2)
    per = tk // tq

    @pl.when(qi == 0)
    def _():
        for a in range(hp):
            vx_ref[a, :, 0:V_DIM] = v_ref[:, a * V_DIM:(a + 1) * V_DIM]
            vx_ref[a, :, V_DIM:2 * V_DIM] = jnp.ones((v_ref.shape[0], V_DIM), v_ref.dtype)

    lane = lax.broadcasted_iota(jnp.int32, (tq, LANES), 1)
    for a in range(hp):
        q = q_ref[:, a * LANES:(a + 1) * LANES]
        zero = jnp.zeros_like(q)
        qs_ref[a, 0:tq, :] = jnp.where(lane < HEAD_DIM, q, zero)
        qs_ref[a, tq:2 * tq, :] = jnp.where(lane >= HEAD_DIM, q, zero)
    m_ref[...] = jnp.full(m_ref.shape, -jnp.inf, F32)
    acc_ref[...] = jnp.zeros(acc_ref.shape, F32)

    def scores(j, s_ref):
        start = pl.multiple_of(j * tk, tk)
        for a in range(hp):
            k = k_ref[pl.ds(start, tk), a * LANES:(a + 1) * LANES]
            s_ref[a] = lax.dot_general(qs_ref[a], k, (((1,), (1,)), ((), ())),
                                       preferred_element_type=F32)

    def update(j, s_ref, width, diag_col=None):
        start = pl.multiple_of(j * tk, tk)
        for a in range(hp):
            vx = vx_ref[a, pl.ds(start, width), :]
            s = s_ref[a, :, 0:width]
            if diag_col is not None:
                row = lax.broadcasted_iota(jnp.int32, s.shape, 0)
                col = lax.broadcasted_iota(jnp.int32, s.shape, 1)
                s = jnp.where(col <= (row & (tq - 1)) + diag_col, s, -jnp.inf)
            m_old = m_ref[a]
            m_new = jnp.maximum(m_old, jnp.max(s, axis=-1, keepdims=True))
            alpha = jnp.exp2(m_old - m_new)
            p = jnp.exp2(s - jnp.concatenate([m_new] * (width // LANES), axis=1))
            pv = jnp.dot(p.astype(vx.dtype), vx, preferred_element_type=F32)
            acc_ref[a] = jnp.concatenate([alpha, alpha], axis=1) * acc_ref[a] + pv
            m_ref[a] = m_new

    n_full = qi // per
    scores(0, sa_ref)

    def body(j, carry):
        def even():
            scores(j + 1, sb_ref)
            update(j, sa_ref, tk)

        def odd():
            scores(j + 1, sa_ref)
            update(j, sb_ref, tk)

        lax.cond((j & 1) == 0, even, odd)
        return carry

    lax.fori_loop(0, n_full, body, 0)

    def diagonal_block(s_ref):
        for r in range(per):
            @pl.when(qi % per == r)
            def _():
                update(n_full, s_ref, (r + 1) * tq, r * tq)

    lax.cond((n_full & 1) == 0, lambda: diagonal_block(sa_ref), lambda: diagonal_block(sb_ref))

    lp = lam_ref[...]
    lam = (jnp.exp(jnp.sum(lp[0:1] * lp[1:2], axis=-1, keepdims=True))
           - jnp.exp(jnp.sum(lp[2:3] * lp[3:4], axis=-1, keepdims=True)) + lam_init)
    for a in range(hp):
        o_all = acc_ref[a, :, 0:V_DIM] / acc_ref[a, :, V_DIM:2 * V_DIM]
        o = o_all[0:tq] - lam * o_all[tq:2 * tq]
        y = _rms(o, gain_ref[...], SUBLN_EPS) * (1.0 - lam_init)
        o_ref[:, a * V_DIM:(a + 1) * V_DIM] = y.astype(o_ref.dtype)


def _attention(z, lam_params, subln_gain, batch, seq, n_heads, lam_init, tq, tk, hp):
    t = z.shape[0]
    nq = seq // tq
    assert tq & (tq - 1) == 0 and tk % tq == 0 and seq % tk == 0 and n_heads % hp == 0
    kern = functools.partial(_attn_kernel, tq=tq, tk=tk, hp=hp, lam_init=lam_init)
    ng = n_heads // hp
    w = hp * LANES
    return pl.pallas_call(
        kern,
        grid=(batch, ng, nq),
        in_specs=[pl.BlockSpec(lam_params.shape, lambda b, h, i: (0, 0)),
                  pl.BlockSpec((1, V_DIM), lambda b, h, i: (0, 0)),
                  pl.BlockSpec((tq, w), lambda b, h, i: (b * nq + i, h)),
                  pl.BlockSpec((seq, w), lambda b, h, i: (b, ng + h)),
                  pl.BlockSpec((seq, w), lambda b, h, i: (b, 2 * ng + h))],
        out_specs=pl.BlockSpec((tq, w), lambda b, h, i: (b * nq + i, h)),
        out_shape=jax.ShapeDtypeStruct((t, n_heads * V_DIM), BF16),
        scratch_shapes=[pltpu.VMEM((hp, 2 * tq, LANES), BF16),
                        pltpu.VMEM((hp, seq, 2 * V_DIM), BF16),
                        pltpu.VMEM((hp, 2 * tq, LANES), F32),
                        pltpu.VMEM((hp, 2 * tq, 2 * V_DIM), F32),
                        pltpu.VMEM((hp, 2 * tq, tk), F32),
                        pltpu.VMEM((hp, 2 * tq, tk), F32)],
        compiler_params=_cparams(("parallel", "parallel", "arbitrary")),
        name="diff_attention",
    )(lam_params, subln_gain.reshape(1, V_DIM), z, z, z)


def _local_kernel(zp_ref, zph_ref, za_ref, zah_ref, zb_ref, zbh_ref, wpool_ref, pscale_ref,
                  dw_ref, dwb_ref, lng_ref, lnb_ref, wpw_ref, o_ref, pbuf_ref, cbuf_ref, sbuf_ref,
                  *, ts, tiles_per_seq):
    i = pl.program_id(0)
    tile_in_seq = i % tiles_per_seq
    first = tile_in_seq == 0
    pool_w = zp_ref.shape[1]

    zp = zp_ref[...].astype(F32)
    halo = zph_ref[...].astype(F32)
    pbuf_ref[0:POOL_HALO, :] = jnp.where(first, jnp.zeros_like(halo), halo)
    pbuf_ref[POOL_HALO:POOL_HALO + ts, :] = zp
    pos = tile_in_seq * ts + lax.broadcasted_iota(jnp.int32, (ts, 1), 0)
    for g, w in enumerate(POOL_WINDOWS):
        cols = slice(g * POOL_GROUP_DIM, (g + 1) * POOL_GROUP_DIM)
        zg = zp[:, cols]
        wsum = zg
        for k in range(1, w):
            wsum = wsum + pbuf_ref[POOL_HALO - k:POOL_HALO - k + ts, cols]
        count = jnp.minimum(pos + 1, w).astype(F32)
        d = (wsum / count - zg).astype(BF16)
        y = jnp.dot(d, wpool_ref[g], preferred_element_type=F32) * pscale_ref[:, cols]
        o_ref[:, cols] = y.astype(o_ref.dtype)

    c = za_ref[...].astype(F32) * jax.nn.sigmoid(zb_ref[...].astype(F32))
    ch = zah_ref[...].astype(F32) * jax.nn.sigmoid(zbh_ref[...].astype(F32))
    cbuf_ref[0:CONV_HALO, :] = jnp.where(first, jnp.zeros_like(ch), ch)
    cbuf_ref[CONV_HALO:CONV_HALO + ts, :] = c
    acc = jnp.zeros_like(c) + dwb_ref[...]
    base = CONV_HALO - (CONV_KERNEL - 1)
    for b in range(SUBLANES):
        offs = [o for o in range(base, base + CONV_KERNEL) if o % SUBLANES == b]
        if not offs:
            continue
        span = offs[-1] - offs[0] + ts
        sbuf_ref[0:span, :] = cbuf_ref[offs[0]:offs[0] + span, :]
        for o in offs:
            k = o - base
            acc = acc + dw_ref[k:k + 1, :] * sbuf_ref[o - offs[0]:o - offs[0] + ts, :]
    mu = jnp.mean(acc, axis=-1, keepdims=True)
    xc = acc - mu
    yn = xc * lax.rsqrt(jnp.mean(xc * xc, axis=-1, keepdims=True) + LN_EPS) * lng_ref[...] + lnb_ref[...]
    sw = yn * jax.nn.sigmoid(yn)
    out = jnp.dot(sw.astype(BF16), wpw_ref[...], preferred_element_type=F32)
    o_ref[:, pool_w:] = out.astype(o_ref.dtype)


def _local_mixers(z, w_pool, pool_scale, dw, dw_b, ln_g, ln_b, w_pw, seq, pool_col, ts):
    t = z.shape[0]
    pool_w = pool_scale.shape[0]
    conv_w = dw.shape[1]
    assert pool_w == conv_w and pool_col % pool_w == 0
    pc = pool_col // pool_w
    hp = ts // POOL_HALO
    hc = ts // CONV_HALO
    kern = functools.partial(_local_kernel, ts=ts, tiles_per_seq=seq // ts)
    row = lambda shape: _resident(shape, lambda i: (0,) * len(shape))
    return pl.pallas_call(
        kern,
        grid=(t // ts,),
        in_specs=[pl.BlockSpec((ts, pool_w), lambda i: (i, pc)),
                  pl.BlockSpec((POOL_HALO, pool_w), lambda i: (jnp.maximum(i * hp - 1, 0), pc)),
                  pl.BlockSpec((ts, conv_w), lambda i: (i, pc + 1)),
                  pl.BlockSpec((CONV_HALO, conv_w), lambda i: (jnp.maximum(i * hc - 1, 0), pc + 1)),
                  pl.BlockSpec((ts, conv_w), lambda i: (i, pc + 2)),
                  pl.BlockSpec((CONV_HALO, conv_w), lambda i: (jnp.maximum(i * hc - 1, 0), pc + 2)),
                  row(w_pool.shape), row((1, pool_w)), row(dw.shape), row((1, conv_w)),
                  row((1, conv_w)), row((1, conv_w)), row(w_pw.shape)],
        out_specs=pl.BlockSpec((ts, pool_w + conv_w), lambda i: (i, 0)),
        out_shape=jax.ShapeDtypeStruct((t, pool_w + conv_w), BF16),
        scratch_shapes=[pltpu.VMEM((POOL_HALO + ts, pool_w), F32),
                        pltpu.VMEM((CONV_HALO + ts, conv_w), F32),
                        pltpu.VMEM((CONV_HALO + ts, conv_w), F32)],
        compiler_params=_cparams(("parallel",)),
        name="local_mixers",
    )(z, z, z, z, z, z, w_pool, pool_scale.reshape(1, pool_w), dw, dw_b.reshape(1, conv_w),
      ln_g.reshape(1, conv_w), ln_b.reshape(1, conv_w), w_pw)


def _top2_route(u, w_router_padded):
    u_hi = u.astype(BF16)
    u_lo = (u - u_hi.astype(F32)).astype(BF16)
    w_hi = w_router_padded.astype(BF16)
    w_lo = (w_router_padded - w_hi.astype(F32)).astype(BF16)
    logits = (jnp.dot(u_hi, w_hi, preferred_element_type=F32)
              + (jnp.dot(u_hi, w_lo, preferred_element_type=F32)
                 + jnp.dot(u_lo, w_hi, preferred_element_type=F32)))
    lane = lax.broadcasted_iota(jnp.int32, logits.shape, 1)
    neg = jnp.full(logits.shape, -jnp.inf, F32)
    lg = jnp.where(lane < N_EXPERTS, logits, neg)
    m1 = jnp.max(lg, axis=-1, keepdims=True)
    i1 = jnp.min(jnp.where(lg == m1, lane, LANES), axis=-1, keepdims=True)
    lg2 = jnp.where(lane == i1, neg, lg)
    m2 = jnp.max(lg2, axis=-1, keepdims=True)
    i2 = jnp.min(jnp.where(lg2 == m2, lane, LANES), axis=-1, keepdims=True)
    e2 = jnp.exp(m2 - m1)
    g1 = 1.0 / (1.0 + e2)
    g2 = e2 / (1.0 + e2)
    return jnp.where(lane == 0, g1, jnp.where(lane == 1, g2, jnp.where(
        lane == 2, i1.astype(F32), jnp.where(lane == 3, i2.astype(F32), 0.0))))


def _router_kernel(u_ref, w_ref, o_ref):
    o_ref[...] = _top2_route(u_ref[...], w_ref[...])


def _router(u2, w_router, tm):
    t, d = u2.shape
    w = jnp.zeros((d, LANES), F32).at[:, :N_EXPERTS].set(w_router)
    return pl.pallas_call(
        _router_kernel,
        grid=(t // tm,),
        in_specs=[pl.BlockSpec((tm, d), lambda i: (i, 0)),
                  _resident((d, LANES), lambda i: (0, 0))],
        out_specs=pl.BlockSpec((tm, LANES), lambda i: (i, 0)),
        out_shape=jax.ShapeDtypeStruct((t, LANES), F32),
        compiler_params=_cparams(("parallel",)),
        name="router",
    )(u2, w)


def _outproj_kernel(oa_ref, ol_ref, w_ref, h_ref, g_ref, h1_ref, u_ref):
    ka = oa_ref.shape[1]
    mix = jnp.dot(oa_ref[...], w_ref[0:ka, :].astype(BF16), preferred_element_type=F32)
    mix = mix + jnp.dot(ol_ref[...], w_ref[ka:, :].astype(BF16), preferred_element_type=F32)
    h1 = h_ref[...] + mix
    h1_ref[...] = h1
    u_ref[...] = _rms(h1, g_ref[...], EPS).astype(u_ref.dtype)


def _outproj(o_attn, o_local, w_out, h, gain, tm, u_dtype):
    t, d = h.shape
    ka, kl = o_attn.shape[1], o_local.shape[1]
    blk = pl.BlockSpec((tm, d), lambda i: (i, 0))
    return pl.pallas_call(
        _outproj_kernel,
        grid=(t // tm,),
        in_specs=[pl.BlockSpec((tm, ka), lambda i: (i, 0)),
                  pl.BlockSpec((tm, kl), lambda i: (i, 0)),
                  _resident((ka + kl, d), lambda i: (0, 0)),
                  blk,
                  _resident((1, d), lambda i: (0, 0))],
        out_specs=[blk, blk],
        out_shape=[jax.ShapeDtypeStruct((t, d), F32), jax.ShapeDtypeStruct((t, d), u_dtype)],
        compiler_params=_cparams(("parallel",)),
        name="outproj",
    )(o_attn, o_local, w_out, h, gain.reshape(1, d))


def _swiglu_partial(x, wg, wu, wd):
    g = jnp.dot(x, wg, preferred_element_type=F32)
    u = jnp.dot(x, wu, preferred_element_type=F32)
    hidden = (g * jax.nn.sigmoid(g)) * u
    return jnp.dot(hidden.astype(wd.dtype), wd, preferred_element_type=F32)


def _dense_ffn_kernel(*refs, n_cast):
    u_ref, wg_ref, wu_ref, wd_ref = refs[:4]
    cast_in = refs[4:4 + n_cast]
    y_ref = refs[4 + n_cast]
    cast_out = refs[5 + n_cast:5 + 2 * n_cast]
    acc_ref = refs[5 + 2 * n_cast]
    f = pl.program_id(1)

    @pl.when(f == 0)
    def _():
        acc_ref[...] = jnp.zeros(acc_ref.shape, F32)

    acc_ref[...] += _swiglu_partial(u_ref[...], wg_ref[...], wu_ref[...], wd_ref[...])

    for src, dst in zip(cast_in, cast_out):
        dst[...] = src[...].astype(dst.dtype)

    @pl.when(f == pl.num_programs(1) - 1)
    def _():
        y_ref[...] = acc_ref[...].astype(y_ref.dtype)


def _cast_slices(arrays_axes, n_steps, nf):
    specs = []
    for arr, axis in arrays_axes:
        e = arr.shape[0]
        if n_steps % e:
            return None
        per = n_steps // e
        if arr.shape[axis] % per:
            return None
        width = arr.shape[axis] // per
        align = LANES if axis == 2 else 16
        if width % align:
            return None
        block = tuple(None if a == 0 else (width if a == axis else arr.shape[a]) for a in range(3))

        def imap(i, f, per=per, axis=axis):
            s = i * nf + f
            return (s // per, s % per, 0) if axis == 1 else (s // per, 0, s % per)

        specs.append(pl.BlockSpec(block, imap))
    return specs


def _dense_ffn(u, wg, wu, wd, tm, tf, casts=()):
    t, d = u.shape
    ff = wg.shape[1]
    grid = (t // tm, ff // tf)
    cast_specs = _cast_slices(casts, grid[0] * grid[1], grid[1]) if casts else []
    if cast_specs is None:
        cast_specs, casts, unsupported = [], (), True
    else:
        unsupported = False
    n_cast = len(cast_specs)
    outs = pl.pallas_call(
        functools.partial(_dense_ffn_kernel, n_cast=n_cast),
        grid=grid,
        in_specs=[pl.BlockSpec((tm, d), lambda i, f: (i, 0)),
                  pl.BlockSpec((d, tf), lambda i, f: (0, f)),
                  pl.BlockSpec((d, tf), lambda i, f: (0, f)),
                  pl.BlockSpec((tf, d), lambda i, f: (f, 0))] + cast_specs,
        out_specs=[pl.BlockSpec((tm, d), lambda i, f: (i, 0))] + cast_specs,
        out_shape=[jax.ShapeDtypeStruct((t, d), BF16)]
        + [jax.ShapeDtypeStruct(a.shape, BF16) for a, _ in casts],
        scratch_shapes=[pltpu.VMEM((tm, d), F32)],
        compiler_params=_cparams(("parallel", "arbitrary")),
        name="dense_ffn",
    )(u, wg, wu, wd, *[a for a, _ in casts])
    return outs[0], (None if unsupported else list(outs[1:]))


def _dispatch_kernel(pend_ref, padded_ref, dest_ref, x_ref, o_ref, zero_ref, row_sem, zero_sem,
                     *, tile, n_tiles):
    rows = x_ref.shape[0]

    @pl.when(pl.program_id(0) == 0)
    def _():
        zero_ref[...] = jnp.zeros(zero_ref.shape, zero_ref.dtype)

        def zero_tile(first_row):
            return pltpu.make_async_copy(zero_ref, o_ref.at[pl.ds(first_row, tile)], zero_sem)

        def last_tile_of(e):
            return zero_tile(pl.multiple_of(pend_ref[e] - tile, tile))

        trailing = range(n_tiles - N_EXPERTS, n_tiles)
        for phase in ("start", "wait"):
            for e in range(N_EXPERTS):
                @pl.when(padded_ref[e] > 0)
                def _():
                    getattr(last_tile_of(e), phase)()
            for tl in trailing:
                @pl.when(tl * tile >= pend_ref[N_EXPERTS - 1])
                def _():
                    getattr(zero_tile(tl * tile), phase)()

    def row_copy(r, k):
        row = dest_ref[0, 0, r * TOP_K + k]
        return pltpu.make_async_copy(x_ref.at[pl.ds(r, 1)], o_ref.at[pl.ds(row, 1)], row_sem)

    def issue(r, carry):
        for k in range(TOP_K):
            row_copy(r, k).start()
        return carry

    def drain(r, carry):
        for k in range(TOP_K):
            row_copy(r, k).wait()
        return carry

    lax.fori_loop(0, rows, issue, 0, unroll=8)
    lax.fori_loop(0, rows, drain, 0, unroll=8)


def _dispatch(u2, dest, pend, padded, tile, n_tiles, tm):
    t, d = u2.shape
    dest3 = dest.reshape(t // tm, 1, TOP_K * tm)
    grid_spec = pltpu.PrefetchScalarGridSpec(
        num_scalar_prefetch=2,
        grid=(t // tm,),
        in_specs=[pl.BlockSpec((1, 1, TOP_K * tm), lambda i, pe, pa: (i, 0, 0), memory_space=pltpu.SMEM),
                  pl.BlockSpec((tm, d), lambda i, pe, pa: (i, 0))],
        out_specs=pl.BlockSpec(memory_space=pl.ANY),
        scratch_shapes=[pltpu.VMEM((tile, d), u2.dtype),
                        pltpu.SemaphoreType.DMA(()),
                        pltpu.SemaphoreType.DMA(())],
    )
    return pl.pallas_call(
        functools.partial(_dispatch_kernel, tile=tile, n_tiles=n_tiles),
        grid_spec=grid_spec,
        out_shape=jax.ShapeDtypeStruct((n_tiles * tile, d), u2.dtype),
        compiler_params=_cparams(("arbitrary",)),
        name="moe_dispatch",
    )(pend, padded, dest3, u2)


def _moe_ffn_kernel(te_ref, nu_ref, tr_ref, x_ref, wg_ref, wu_ref, wd_ref, y_ref, acc_ref, xb_ref):
    i = pl.program_id(0)
    f = pl.program_id(1)
    half = x_ref.shape[0] // 2

    @pl.when(i < nu_ref[0])
    def _():
        @pl.when(f == 0)
        def _():
            acc_ref[...] = jnp.zeros(acc_ref.shape, F32)
            xb_ref[...] = x_ref[...].astype(xb_ref.dtype)

        @pl.when(tr_ref[i] > half)
        def _():
            acc_ref[...] += _swiglu_partial(xb_ref[...], wg_ref[...], wu_ref[...], wd_ref[...])

        @pl.when(tr_ref[i] <= half)
        def _():
            acc_ref[0:half, :] += _swiglu_partial(xb_ref[0:half, :], wg_ref[...], wu_ref[...], wd_ref[...])

        @pl.when(f == pl.num_programs(1) - 1)
        def _():
            y_ref[...] = acc_ref[...].astype(y_ref.dtype)

    @pl.when(i >= nu_ref[0])
    def _():
        y_ref[...] = jnp.zeros(y_ref.shape, y_ref.dtype)


def _moe_ffn(x_sorted, tile_expert, n_used, tile_rows, wg, wu, wd, tm, tf):
    p, d = x_sorted.shape
    ff = wg.shape[2]
    nf = ff // tf

    def row(i, f, te, nu, tr):
        return (jnp.minimum(i, nu[0] - 1), 0)

    def fcol(i, f, nu):
        return jnp.where(i < nu[0], f, nf - 1)

    grid_spec = pltpu.PrefetchScalarGridSpec(
        num_scalar_prefetch=3,
        grid=(p // tm, nf),
        in_specs=[pl.BlockSpec((tm, d), row),
                  pl.BlockSpec((None, d, tf), lambda i, f, te, nu, tr: (te[i], 0, fcol(i, f, nu))),
                  pl.BlockSpec((None, d, tf), lambda i, f, te, nu, tr: (te[i], 0, fcol(i, f, nu))),
                  pl.BlockSpec((None, tf, d), lambda i, f, te, nu, tr: (te[i], fcol(i, f, nu), 0))],
        out_specs=pl.BlockSpec((tm, d), lambda i, f, te, nu, tr: (i, 0)),
        scratch_shapes=[pltpu.VMEM((tm, d), F32), pltpu.VMEM((tm, d), wg.dtype)],
    )
    return pl.pallas_call(
        _moe_ffn_kernel,
        grid_spec=grid_spec,
        out_shape=jax.ShapeDtypeStruct((p, d), BF16),
        compiler_params=_cparams(("arbitrary", "arbitrary")),
        name="moe_ffn",
    )(tile_expert, n_used, tile_rows, x_sorted, wg, wu, wd)


def _moe(u2, route, wg, wu, wd, tm_route, tm, tf):
    t, d = u2.shape
    expert = route[:, 2:2 + TOP_K].astype(jnp.int32).reshape(-1)
    onehot = (expert[:, None] == jnp.arange(N_EXPERTS, dtype=jnp.int32)[None, :]).astype(jnp.int32)
    csum = jnp.cumsum(onehot, axis=0)
    rank = jnp.sum(csum * onehot, axis=1) - 1
    counts = csum[-1]
    padded = ((counts + tm - 1) // tm) * tm
    pend = jnp.cumsum(padded)
    pstart = pend - padded
    dest = pstart[expert] + rank
    n_tiles = (TOP_K * t) // tm + N_EXPERTS
    tile_start = jnp.arange(n_tiles, dtype=jnp.int32) * tm
    tile_expert = jnp.minimum(jnp.sum((tile_start[:, None] >= pend[None, :]).astype(jnp.int32), axis=1),
                              N_EXPERTS - 1).astype(jnp.int32)
    n_used = (pend[-1] // tm).astype(jnp.int32).reshape(1)
    x_sorted = _dispatch(u2, dest.astype(jnp.int32), pend.astype(jnp.int32), padded.astype(jnp.int32),
                         tm, n_tiles, tm_route)
    tile_rows = jnp.clip(counts[tile_expert] - (tile_start - pstart[tile_expert]), 0, tm).astype(jnp.int32)
    y_sorted = _moe_ffn(x_sorted, tile_expert, n_used, tile_rows, wg, wu, wd, tm, tf)
    dest2 = dest.reshape(t, TOP_K)
    return [y_sorted.at[dest2[:, k]].get(mode="promise_in_bounds") for k in range(TOP_K)]


def _ple_kernel(*refs, n_y, gated, final):
    h_ref, y_refs = refs[0], refs[1:1 + n_y]
    rest = refs[1 + n_y:]
    if gated:
        r = rest[0][...]
        rest = rest[1:]
    gp_ref, wg_ref, p_ref, wp_ref, gn_ref = rest[:5]
    out_refs = rest[5:]
    h2 = h_ref[...]
    for k, y_ref in enumerate(y_refs):
        y = y_ref[...].astype(F32)
        h2 = h2 + (r[:, k:k + 1] * y if gated else y)
    u3 = _rms(h2, gp_ref[...], EPS).astype(BF16)
    gate = jax.nn.sigmoid(jnp.dot(u3, wg_ref[...].astype(BF16), preferred_element_type=F32))
    emb = jnp.dot(p_ref[...].astype(BF16), wp_ref[...].astype(BF16), preferred_element_type=F32)
    h3 = h2 + emb * gate
    if final:
        out_refs[0][...] = _rms(h3, gn_ref[...], EPS)
    else:
        out_refs[0][...] = h3
        out_refs[1][...] = _rms(h3, gn_ref[...], EPS).astype(out_refs[1].dtype)


def _ple(h1, ys, route, ple_gain, w_gate, p, w_proj, next_gain, final, tm):
    t, d = h1.shape
    pd = p.shape[1]
    n_y = len(ys)
    blk = pl.BlockSpec((tm, d), lambda i: (i, 0))
    vec = _resident((1, d), lambda i: (0, 0))
    gated = route is not None
    in_specs = [blk] * (1 + n_y)
    args = [h1] + list(ys)
    if gated:
        in_specs.append(pl.BlockSpec((tm, LANES), lambda i: (i, 0)))
        args.append(route)
    in_specs += [vec, _resident((d, d), lambda i: (0, 0)), pl.BlockSpec((tm, pd), lambda i: (i, 0)),
                 _resident((pd, d), lambda i: (0, 0)), vec]
    args += [ple_gain.reshape(1, d), w_gate, p, w_proj, next_gain.reshape(1, d)]
    if final:
        out_specs = [blk]
        out_shape = [jax.ShapeDtypeStruct((t, d), F32)]
    else:
        out_specs = [blk, blk]
        out_shape = [jax.ShapeDtypeStruct((t, d), F32), jax.ShapeDtypeStruct((t, d), BF16)]
    return pl.pallas_call(
        functools.partial(_ple_kernel, n_y=n_y, gated=gated, final=final),
        grid=(t // tm,),
        in_specs=in_specs,
        out_specs=out_specs,
        out_shape=out_shape,
        compiler_params=_cparams(("parallel",)),
        name="ple",
    )(*args)


def _tiles(t, seq, ff):
    return dict(
        tm_norm=min(512, t),
        tm_in=min(2048, t), tn_in=512,
        tq=min(512, seq), tk=min(1024, seq), heads_per_step=2,
        ts=min(512, seq),
        tm_out=min(512, t),
        tm_ffn=min(1024, t), tf_ffn=min(512, ff),
        tm_moe=min(512, t), tf_moe=min(1024, ff),
        tm_ple=min(512, t),
    )


def _rotary_tables(positions):
    inv_freq = ROPE_THETA ** (-jnp.arange(0, ROPE_DIM, 2, dtype=F32) / ROPE_DIM)
    ang = positions.astype(F32).reshape(-1, 1) * inv_freq
    cos, sin = lax.optimization_barrier((jnp.cos(ang), jnp.sin(ang)))
    cos_rep = jnp.tile(cos, (1, LANES // ROPE_HALF))
    sin_rep = jnp.tile(sin, (1, LANES // ROPE_HALF))
    dim128 = (jnp.arange(LANES, dtype=jnp.int32) % HEAD_DIM)[None, :]
    c = jnp.where(dim128 < ROPE_DIM, cos_rep, 1.0)
    s = jnp.where(dim128 < ROPE_HALF, -sin_rep, jnp.where(dim128 < ROPE_DIM, sin_rep, 0.0))
    lane = jnp.arange(2 * LANES, dtype=jnp.int32)
    dim = lane % HEAD_DIM
    src = jnp.where(dim < ROPE_HALF, lane + ROPE_HALF, jnp.where(dim < ROPE_DIM, lane - ROPE_HALF, lane))
    swap = (lane[:, None] == src[None, :]).astype(BF16)
    return c, s, swap


def kernel(x, p, positions, mix_norm, w_in, lambda_q1, lambda_k1, lambda_q2, lambda_k2, subln_gain, w_pool, pool_scale, conv_dw, conv_dw_bias, conv_ln_gain, conv_ln_bias, w_conv_pw, w_out, ffn_norm, w_dense_gate, w_dense_up, w_dense_down, w_router, w_exp_gate, w_exp_up, w_exp_down, ple_norm, w_ple_gate, w_ple_proj, final_norm):
    batch, seq, d = x.shape
    depth = w_in.shape[0]
    t = batch * seq
    attn_w = (w_in.shape[2] - 2 * conv_dw.shape[2] - pool_scale.shape[1]) // 3
    n_heads = attn_w // V_DIM
    qk_cols = attn_w
    pool_col = 3 * attn_w
    ff = w_dense_gate.shape[2]
    ts = _tiles(t, seq, ff)

    rot_c, rot_s, rot_swap = _rotary_tables(positions)
    h = x.reshape(t, d)
    u = _rmsnorm(h, mix_norm[0], ts["tm_norm"])
    out = None
    experts_bf16 = None
    for i in range(depth):
        lam_init = 0.8 - 0.6 * math.exp(-0.3 * i)
        final = i == depth - 1
        z = _inproj(u, w_in, i, rot_c, rot_s, rot_swap, qk_cols, ts["tm_in"], ts["tn_in"])
        lam_params = jnp.stack([lambda_q1[i], lambda_k1[i], lambda_q2[i], lambda_k2[i]]).astype(F32)
        o_attn = _attention(z, lam_params, subln_gain[i], batch, seq, n_heads, lam_init, ts["tq"],
                            ts["tk"], ts["heads_per_step"])
        o_local = _local_mixers(z, w_pool[i].astype(BF16), pool_scale[i], conv_dw[i], conv_dw_bias[i],
                                conv_ln_gain[i], conv_ln_bias[i], w_conv_pw[i].astype(BF16),
                                seq, pool_col, ts["ts"])
        j = i // 2
        h1, u2 = _outproj(o_attn, o_local, w_out[i], h, ffn_norm[i], ts["tm_out"],
                          BF16 if i % 2 == 0 else F32)
        if i % 2 == 0:
            casts = ()
            if i + 1 < depth:
                jn = (i + 1) // 2
                casts = ((w_exp_gate[jn], 2), (w_exp_up[jn], 2), (w_exp_down[jn], 1))
            y, experts_bf16 = _dense_ffn(u2, w_dense_gate[j].astype(BF16), w_dense_up[j].astype(BF16),
                                         w_dense_down[j].astype(BF16), ts["tm_ffn"], ts["tf_ffn"], casts)
            ys, route = [y], None
        else:
            if experts_bf16 is None:
                experts_bf16 = [w.astype(BF16) for w in (w_exp_gate[j], w_exp_up[j], w_exp_down[j])]
            route = _router(u2, w_router[j], ts["tm_out"])
            ys = _moe(u2, route, *experts_bf16, ts["tm_out"], ts["tm_moe"], ts["tf_moe"])
            experts_bf16 = None
        next_gain = final_norm if final else mix_norm[i + 1]
        res = _ple(h1, ys, route, ple_norm[i], w_ple_gate[i], p[i].reshape(t, -1),
                   w_ple_proj[i], next_gain, final, ts["tm_ple"])
        if final:
            out = res[0]
        else:
            h, u = res
    return out.reshape(batch, seq, d)
```
